```python
import math
import jax, jax.numpy as jnp
from jax import lax
import numpy as np

D_MODEL = 1024
BATCH = 2
SEQ = 16384
DEPTH = 1
DEC_BATCH = 8
DEC_SEQ = 64
PAST_LEN = 4096

CHUNK = 64
CONV_DIM = 512
CONV_TAPS = 31
CONV_STATE = CONV_TAPS - 1
N_HEADS = 4
HEAD_DIM = 64
V_DIM = 2 * HEAD_DIM
ATTN_DIM = N_HEADS * V_DIM
QK_DIM = N_HEADS * 2 * HEAD_DIM
N_GROUPS = 4
EXPERTS_PER_GROUP = 4
N_EXPERTS = N_GROUPS * EXPERTS_PER_GROUP
TOP_K_INNER = 2
EXPERT_HIDDEN = 512
Q_BLOCK = 128
EPS = 1e-6
SCALE = HEAD_DIM ** -0.5
COL_GLU = 2 * CONV_DIM
COL_GATE = 2 * D_MODEL
IN_COLS = COL_GLU + 2 * QK_DIM + ATTN_DIM + COL_GATE
SPLIT_POINTS = (COL_GLU, COL_GLU + QK_DIM, COL_GLU + 2 * QK_DIM, COL_GLU + 2 * QK_DIM + ATTN_DIM)

kernel_name = 'chunk_causal_conformer_diffattn_hmoe'


def lambda_init_of(layer):
    return 0.8 - 0.6 * math.exp(-0.3 * layer)


def rms_norm(x, g):
    xf = x.astype(jnp.float32)
    y = xf * lax.rsqrt(jnp.mean(xf * xf, axis=-1, keepdims=True) + EPS)
    return (y * g.astype(jnp.float32)).astype(x.dtype)


def layer_norm(x, g, b):
    xf = x.astype(jnp.float32)
    mu = jnp.mean(xf, axis=-1, keepdims=True)
    var = jnp.mean(jnp.square(xf - mu), axis=-1, keepdims=True)
    y = (xf - mu) * lax.rsqrt(var + EPS) * g.astype(jnp.float32) + b.astype(jnp.float32)
    return y.astype(x.dtype)


def conv_branch(glu_in, history, lp):
    a, gate = jnp.split(glu_in, 2, axis=-1)
    u = a * jax.nn.sigmoid(gate)
    padded = jnp.concatenate([history, u], axis=1)
    y = lax.conv_general_dilated(padded, lp['conv_dw'][:, None, :], (1,), 'VALID',
                                 dimension_numbers=('NWC', 'WIO', 'NWC'),
                                 feature_group_count=CONV_DIM)
    y = y + lp['conv_db']
    y = jax.nn.silu(layer_norm(y, lp['conv_ln_g'], lp['conv_ln_b']))
    return y @ lp['w_conv_out'], padded[:, -CONV_STATE:]


def diff_attn(q, k, v, allowed, lam, lam_init, subln_g):
    s = jnp.einsum('bqhmd,bkhmd->bhmqk', q, k).astype(jnp.float32) * SCALE
    if allowed is not None:
        s = jnp.where(allowed, s, -jnp.inf)
    p = jax.nn.softmax(s, axis=-1)
    a = p[:, :, 0] - lam * p[:, :, 1]
    o = jnp.einsum('bhqk,bkhe->bqhe', a.astype(v.dtype), v)
    return rms_norm(o, subln_g) * (1.0 - lam_init)


def prompt_attention(q, k, v, lam, lam_init, subln_g):
    b, s = q.shape[0], q.shape[1]
    nb = s // Q_BLOCK
    qb = q.reshape(b, nb, Q_BLOCK, N_HEADS, 2, HEAD_DIM).transpose(1, 0, 2, 3, 4, 5)
    k_chunk = jnp.arange(s) // CHUNK

    def block(args):
        qi, i = args
        q_chunk = (i * Q_BLOCK + jnp.arange(Q_BLOCK)) // CHUNK
        allowed = k_chunk[None, :] <= q_chunk[:, None]
        return diff_attn(qi, k, v, allowed, lam, lam_init, subln_g)

    o = lax.map(block, (qb, jnp.arange(nb)))
    return o.transpose(1, 0, 2, 3, 4).reshape(b, s, ATTN_DIM)


def token_mix(h, conv_hist, k_past, v_past, lp, lam, lam_init):
    b, t, _ = h.shape
    u = h @ lp['w_in']
    glu_in, q, k, v, gates = jnp.split(u, SPLIT_POINTS, axis=-1)
    q = rms_norm(q.reshape(b, t, N_HEADS, 2, HEAD_DIM), lp['q_norm_g'])
    k = rms_norm(k.reshape(b, t, N_HEADS, 2, HEAD_DIM), lp['k_norm_g'])
    v = v.reshape(b, t, N_HEADS, V_DIM)
    conv_out, conv_state = conv_branch(glu_in, conv_hist, lp)
    if k_past is None:
        attn = prompt_attention(q, k, v, lam, lam_init, lp['subln_g'])
    else:
        k_all = jnp.concatenate([k_past, k], axis=1)
        v_all = jnp.concatenate([v_past, v], axis=1)
        attn = diff_attn(q, k_all, v_all, None, lam, lam_init, lp['subln_g']).reshape(b, t, ATTN_DIM)
    attn_out = attn @ lp['w_attn_out']
    g_conv, g_attn = jnp.split(gates, 2, axis=-1)
    merged = jax.nn.sigmoid(g_conv) * conv_out + jax.nn.sigmoid(g_attn) * attn_out
    return merged @ lp['w_o'], k, v, conv_state


def hier_moe(h, lp):
    shp = h.shape
    t = h.reshape(-1, D_MODEL)
    gl = (t @ lp['w_group']).astype(jnp.float32) + lp['b_group'].astype(jnp.float32)
    gp = jax.nn.softmax(gl, axis=-1)
    gi = jnp.argmax(gl, axis=-1)
    g_w = jnp.take_along_axis(gp, gi[:, None], axis=-1)
    el = jnp.einsum('td,gde->tge', t, lp['w_router']).astype(jnp.float32) + lp['b_router'].astype(jnp.float32)
    el = jnp.take_along_axis(el, gi[:, None, None], axis=1)[:, 0]
    tv, ti = lax.top_k(el, TOP_K_INNER)
    ew = jax.nn.softmax(tv, axis=-1) * g_w
    eid = gi[:, None] * EXPERTS_PER_GROUP + ti
    comb = jnp.sum(jax.nn.one_hot(eid, N_EXPERTS, dtype=jnp.float32) * ew[..., None], axis=1)
    comb = comb.astype(t.dtype)
    out = jnp.zeros_like(t)
    for e in range(N_EXPERTS):
        he = jax.nn.silu(t @ lp['w_gate'][e]) * (t @ lp['w_up'][e])
        out = out + comb[:, e:e + 1] * (he @ lp['w_down'][e])
    return out.reshape(shp)


def setup_inputs(seed: int = 0) -> dict:
    key = jax.random.key(seed)
    ks = jax.random.split(key, 32)
    f32 = jnp.float32
    nrm = lambda k, shape, s: jax.random.normal(k, shape, f32) * s
    gain = lambda k, shape: 1.0 + 0.01 * jax.random.normal(k, shape, f32)
    return {
        'x_prompt': nrm(ks[0], (BATCH, SEQ, D_MODEL), 1.0),
        'x_sample': nrm(ks[1], (DEC_BATCH, DEC_SEQ, D_MODEL), 1.0),
        'cache_k': nrm(ks[2], (DEPTH, DEC_BATCH, PAST_LEN, N_HEADS, 2, HEAD_DIM), 1.0),
        'cache_v': nrm(ks[3], (DEPTH, DEC_BATCH, PAST_LEN, N_HEADS, V_DIM), 1.0),
        'state_conv': nrm(ks[4], (DEPTH, DEC_BATCH, CONV_STATE, CONV_DIM), 0.5),
        'norm1_g': gain(ks[5], (DEPTH, D_MODEL)),
        'w_in': nrm(ks[6], (DEPTH, D_MODEL, IN_COLS), D_MODEL ** -0.5),
        'conv_dw': nrm(ks[7], (DEPTH, CONV_TAPS, CONV_DIM), CONV_TAPS ** -0.5),
        'conv_db': nrm(ks[8], (DEPTH, CONV_DIM), 0.01),
        'conv_ln_g': gain(ks[9], (DEPTH, CONV_DIM)),
        'conv_ln_b': nrm(ks[10], (DEPTH, CONV_DIM), 0.01),
        'w_conv_out': nrm(ks[11], (DEPTH, CONV_DIM, D_MODEL), CONV_DIM ** -0.5),
        'q_norm_g': gain(ks[12], (DEPTH, 2, HEAD_DIM)),
        'k_norm_g': gain(ks[13], (DEPTH, 2, HEAD_DIM)),
        'lambda_q1': nrm(ks[14], (DEPTH, HEAD_DIM), 0.1),
        'lambda_k1': nrm(ks[15], (DEPTH, HEAD_DIM), 0.1),
        'lambda_q2': nrm(ks[16], (DEPTH, HEAD_DIM), 0.1),
        'lambda_k2': nrm(ks[17], (DEPTH, HEAD_DIM), 0.1),
        'subln_g': gain(ks[18], (DEPTH, V_DIM)),
        'w_attn_out': nrm(ks[19], (DEPTH, ATTN_DIM, D_MODEL), ATTN_DIM ** -0.5),
        'w_o': nrm(ks[20], (DEPTH, D_MODEL, D_MODEL), D_MODEL ** -0.5),
        'norm2_g': gain(ks[21], (DEPTH, D_MODEL)),
        'w_group': nrm(ks[22], (DEPTH, D_MODEL, N_GROUPS), D_MODEL ** -0.5),
        'b_group': nrm(ks[23], (DEPTH, N_GROUPS), 0.01),
        'w_router': nrm(ks[24], (DEPTH, N_GROUPS, D_MODEL, EXPERTS_PER_GROUP), D_MODEL ** -0.5),
        'b_router': nrm(ks[25], (DEPTH, N_GROUPS, EXPERTS_PER_GROUP), 0.01),
        'w_gate': nrm(ks[26], (DEPTH, N_EXPERTS, D_MODEL, EXPERT_HIDDEN), D_MODEL ** -0.5),
        'w_up': nrm(ks[27], (DEPTH, N_EXPERTS, D_MODEL, EXPERT_HIDDEN), D_MODEL ** -0.5),
        'w_down': nrm(ks[28], (DEPTH, N_EXPERTS, EXPERT_HIDDEN, D_MODEL), EXPERT_HIDDEN ** -0.5),
    }


def reference(x_prompt, x_sample, cache_k, cache_v, state_conv, norm1_g, w_in, conv_dw, conv_db,
              conv_ln_g, conv_ln_b, w_conv_out, q_norm_g, k_norm_g, lambda_q1, lambda_k1,
              lambda_q2, lambda_k2, subln_g, w_attn_out, w_o, norm2_g, w_group, b_group,
              w_router, b_router, w_gate, w_up, w_down):
    xp, xs = x_prompt, x_sample
    kp_l, vp_l, cp_l, ks_l, vs_l, cs_l = [], [], [], [], [], []
    for l in range(DEPTH):
        lp = {'w_in': w_in[l], 'conv_dw': conv_dw[l], 'conv_db': conv_db[l],
              'conv_ln_g': conv_ln_g[l], 'conv_ln_b': conv_ln_b[l], 'w_conv_out': w_conv_out[l],
              'q_norm_g': q_norm_g[l], 'k_norm_g': k_norm_g[l], 'subln_g': subln_g[l],
              'w_attn_out': w_attn_out[l], 'w_o': w_o[l], 'w_group': w_group[l],
              'b_group': b_group[l], 'w_router': w_router[l], 'b_router': b_router[l],
              'w_gate': w_gate[l], 'w_up': w_up[l], 'w_down': w_down[l]}
        lam_init = lambda_init_of(l)
        lam = (jnp.exp(jnp.sum((lambda_q1[l] * lambda_k1[l]).astype(jnp.float32)))
               - jnp.exp(jnp.sum((lambda_q2[l] * lambda_k2[l]).astype(jnp.float32))) + lam_init)
        hp = rms_norm(xp, norm1_g[l])
        zero_hist = jnp.zeros((xp.shape[0], CONV_STATE, CONV_DIM), xp.dtype)
        mix_p, kp, vp, cp = token_mix(hp, zero_hist, None, None, lp, lam, lam_init)
        xp = xp + mix_p
        xp = xp + hier_moe(rms_norm(xp, norm2_g[l]), lp)
        hs = rms_norm(xs, norm1_g[l])
        mix_s, ksn, vsn, csn = token_mix(hs, state_conv[l], cache_k[l], cache_v[l], lp, lam, lam_init)
        xs = xs + mix_s
        xs = xs + hier_moe(rms_norm(xs, norm2_g[l]), lp)
        kp_l.append(kp); vp_l.append(vp); cp_l.append(cp)
        ks_l.append(ksn); vs_l.append(vsn); cs_l.append(csn)
    return (xp, xs, jnp.stack(kp_l), jnp.stack(vp_l), jnp.stack(cp_l),
            jnp.stack(ks_l), jnp.stack(vs_l), jnp.stack(cs_l))
```

```python
import functools
import math

import jax
import jax.numpy as jnp
from jax import lax
from jax.experimental import pallas as pl
from jax.experimental.pallas import tpu as pltpu

D_MODEL = 1024
CHUNK = 64
CONV_DIM = 512
CONV_TAPS = 31
CONV_STATE = CONV_TAPS - 1
N_HEADS = 4
HEAD_DIM = 64
V_DIM = 2 * HEAD_DIM
ATTN_DIM = N_HEADS * V_DIM
QK_DIM = N_HEADS * 2 * HEAD_DIM
N_GROUPS = 4
EXPERTS_PER_GROUP = 4
N_EXPERTS = N_GROUPS * EXPERTS_PER_GROUP
TOP_K_INNER = 2
EXPERT_HIDDEN = 512
EPS = 1e-6
SCALE = HEAD_DIM ** -0.5
COL_GLU = 2 * CONV_DIM
COL_Q = COL_GLU
COL_K = COL_Q + QK_DIM
COL_V = COL_K + QK_DIM
COL_GC = COL_V + ATTN_DIM
COL_GA = COL_GC + D_MODEL
IN_COLS = COL_GA + D_MODEL

LANES = 128
HIST_ROWS = 32
HIST_PAD = HIST_ROWS - CONV_STATE
CONV_ROW_CHUNK = 64
VMEM_LIMIT = 56 * 1024 * 1024

BF16 = jnp.bfloat16
F32 = jnp.float32


def _sigmoid(x):
    return 1.0 / (1.0 + jnp.exp(-x))


def _const_spec(shape):
    n = len(shape)
    return pl.BlockSpec(shape, lambda *_: (0,) * n)


def _in_proj_kernel(x_ref, hist_ref, g1_ref, w_in_ref, dw_ref, db_ref, lng_ref, lnb_ref, wco_ref, gq_ref, gk_ref,
                    bd_ref, q_ref, kf_ref, kb_ref, vf_ref, vb_ref, p_ref, g_ref, cs_ref, cbuf, ybuf):
    t = pl.program_id(1)
    tm = x_ref.shape[1]
    x = x_ref[0]
    ms = jnp.mean(x * x, axis=-1, keepdims=True)
    h = (x * lax.rsqrt(ms + EPS) * g1_ref[...]).astype(BF16)

    def proj(lo, hi):
        return jnp.dot(h, w_in_ref[:, lo:hi], preferred_element_type=F32)

    u = proj(0, CONV_DIM) * _sigmoid(proj(CONV_DIM, COL_GLU))

    @pl.when(t == 0)
    def _():
        cbuf[0:HIST_ROWS, :] = hist_ref[0]

    @pl.when(t > 0)
    def _():
        cbuf[0:HIST_ROWS, :] = cbuf[tm:tm + HIST_ROWS, :]

    cbuf[HIST_ROWS:HIST_ROWS + tm, :] = u
    cs_ref[0] = cbuf[tm:tm + HIST_ROWS, :]

    rc = min(CONV_ROW_CHUNK, tm)
    for c in range(tm // rc):
        base = c * rc + HIST_PAD
        acc = dw_ref[0:1, :] * cbuf[base:base + rc, :]
        for j in range(1, CONV_TAPS):
            acc = acc + dw_ref[j:j + 1, :] * cbuf[base + j:base + j + rc, :]
        y = acc + db_ref[...]
        mu = jnp.mean(y, axis=-1, keepdims=True)
        yc = y - mu
        var = jnp.mean(yc * yc, axis=-1, keepdims=True)
        yn = yc * lax.rsqrt(var + EPS) * lng_ref[...] + lnb_ref[...]
        ybuf[c * rc:(c + 1) * rc, :] = (yn * _sigmoid(yn)).astype(BF16)
    conv_out = jnp.dot(ybuf[...], wco_ref[...], preferred_element_type=F32)
    p_ref[0] = (_sigmoid(proj(COL_GC, COL_GA)) * conv_out).astype(BF16)
    g_ref[0] = _sigmoid(proj(COL_GA, IN_COLS)).astype(BF16)

    def head_norm(z, gain_ref):
        ss = jnp.dot((z * z).astype(BF16), bd_ref[...], preferred_element_type=F32) * (1.0 / HEAD_DIM)
        return z * lax.rsqrt(ss + EPS) * gain_ref[...]

    q_ref[0] = (head_norm(proj(COL_Q, COL_K), gq_ref) * SCALE).astype(BF16)
    kn = head_norm(proj(COL_K, COL_V), gk_ref)
    kf_ref[0] = kn
    kb_ref[0] = kn.astype(BF16)
    v = proj(COL_V, COL_GC)
    vf_ref[0] = v
    vb_ref[0] = v.astype(BF16)


def _in_proj(x, hist, lw, tm):
    b, s, _ = x.shape
    grid = (b, s // tm)
    tok = lambda width: pl.BlockSpec((1, tm, width), lambda i, j: (i, j, 0))
    out_shape = (
        jax.ShapeDtypeStruct((b, s, QK_DIM), BF16),
        jax.ShapeDtypeStruct((b, s, QK_DIM), F32),
        jax.ShapeDtypeStruct((b, s, QK_DIM), BF16),
        jax.ShapeDtypeStruct((b, s, ATTN_DIM), F32),
        jax.ShapeDtypeStruct((b, s, ATTN_DIM), BF16),
        jax.ShapeDtypeStruct((b, s, D_MODEL), BF16),
        jax.ShapeDtypeStruct((b, s, D_MODEL), BF16),
        jax.ShapeDtypeStruct((b, HIST_ROWS, CONV_DIM), F32),
    )
    return pl.pallas_call(
        _in_proj_kernel,
        grid=grid,
        in_specs=[
            tok(D_MODEL),
            pl.BlockSpec((1, HIST_ROWS, CONV_DIM), lambda i, j: (i, 0, 0)),
            _const_spec((1, D_MODEL)),
            _const_spec((D_MODEL, IN_COLS)),
            _const_spec((CONV_TAPS, CONV_DIM)),
            _const_spec((1, CONV_DIM)),
            _const_spec((1, CONV_DIM)),
            _const_spec((1, CONV_DIM)),
            _const_spec((CONV_DIM, D_MODEL)),
            _const_spec((1, QK_DIM)),
            _const_spec((1, QK_DIM)),
            _const_spec((QK_DIM, QK_DIM)),
        ],
        out_specs=(tok(QK_DIM), tok(QK_DIM), tok(QK_DIM), tok(ATTN_DIM), tok(ATTN_DIM), tok(D_MODEL), tok(D_MODEL),
                   pl.BlockSpec((1, HIST_ROWS, CONV_DIM), lambda i, j: (i, 0, 0))),
        out_shape=out_shape,
        scratch_shapes=[pltpu.VMEM((HIST_ROWS + tm, CONV_DIM), F32), pltpu.VMEM((tm, CONV_DIM), BF16)],
        compiler_params=pltpu.CompilerParams(dimension_semantics=("arbitrary", "arbitrary"),
                                             vmem_limit_bytes=VMEM_LIMIT),
        name="in_proj",
    )(x, hist, lw["g1"], lw["w_in"], lw["conv_dw"], lw["conv_db"], lw["conv_ln_g"], lw["conv_ln_b"],
      lw["w_conv_out"], lw["gq"], lw["gk"], lw["bd"])


def _lambda(lq1_ref, lk1_ref, lq2_ref, lk2_ref, lam_init):
    a = jnp.sum(lq1_ref[...] * lk1_ref[...], axis=-1, keepdims=True)
    b = jnp.sum(lq2_ref[...] * lk2_ref[...], axis=-1, keepdims=True)
    return jnp.exp(a) - jnp.exp(b) + lam_init


def _split_maps(q):
    lane = lax.broadcasted_iota(jnp.int32, q.shape, 1)
    zero = jnp.zeros_like(q)
    return jnp.where(lane < HEAD_DIM, q, zero), jnp.where(lane >= HEAD_DIM, q, zero)


def _scores(qm, k):
    return lax.dot_general(qm, k, (((1,), (1,)), ((), ())), preferred_element_type=F32)


def _sub_ln(o1, o2, lam, sg_ref, lam_init):
    o = o1 - lam * o2
    ms = jnp.mean(o * o, axis=-1, keepdims=True)
    return o * lax.rsqrt(ms + EPS) * sg_ref[...] * (1.0 - lam_init)


def _attn_prompt_kernel(q_ref, k_ref, v_ref, lq1_ref, lk1_ref, lq2_ref, lk2_ref, sg_ref, o_ref,
                        qs1, qs2, m1, l1, a1, m2, l2, a2, *, lam_init):
    qi = pl.program_id(2)
    tq = q_ref.shape[1]
    q1, q2 = _split_maps(q_ref[0])
    qs1[...] = q1
    qs2[...] = q2
    for m, l, a in ((m1, l1, a1), (m2, l2, a2)):
        m[...] = jnp.full(m.shape, -jnp.inf, F32)
        l[...] = jnp.zeros(l.shape, F32)
        a[...] = jnp.zeros(a.shape, F32)

    def block(kstart, masked):
        k = k_ref[0, pl.ds(kstart, tq), :]
        v = v_ref[0, pl.ds(kstart, tq), :]
        for qs, m, l, a in ((qs1, m1, l1, a1), (qs2, m2, l2, a2)):
            s = _scores(qs[...], k)
            if masked:
                row = lax.broadcasted_iota(jnp.int32, s.shape, 0) // CHUNK
                col = lax.broadcasted_iota(jnp.int32, s.shape, 1) // CHUNK
                s = jnp.where(col <= row, s, -jnp.inf)
            m_prev = m[...]
            m_new = jnp.maximum(m_prev, jnp.max(s, axis=-1, keepdims=True))
            alpha = jnp.exp(m_prev - m_new)
            p = jnp.exp(s - m_new)
            l[...] = alpha * l[...] + jnp.sum(p, axis=-1, keepdims=True)
            a[...] = alpha * a[...] + jnp.dot(p.astype(BF16), v, preferred_element_type=F32)
            m[...] = m_new

    def body(i, carry):
        block(pl.multiple_of(i * tq, tq), False)
        return carry

    lax.fori_loop(0, qi, body, 0)
    block(pl.multiple_of(qi * tq, tq), True)

    lam = _lambda(lq1_ref, lk1_ref, lq2_ref, lk2_ref, lam_init)
    o_ref[0] = _sub_ln(a1[...] / l1[...], a2[...] / l2[...], lam, sg_ref, lam_init).astype(BF16)


def _attn_prompt(q, k, v, lw, lam_init, tq):
    b, s, _ = q.shape
    grid = (b, N_HEADS, s // tq)
    qspec = pl.BlockSpec((1, tq, LANES), lambda i, h, j: (i, j, h))
    kvspec = pl.BlockSpec((1, s, LANES), lambda i, h, j: (i, 0, h))
    vec = _const_spec((1, HEAD_DIM))
    return pl.pallas_call(
        functools.partial(_attn_prompt_kernel, lam_init=lam_init),
        grid=grid,
        in_specs=[qspec, kvspec, kvspec, vec, vec, vec, vec, _const_spec((1, V_DIM))],
        out_specs=qspec,
        out_shape=jax.ShapeDtypeStruct((b, s, ATTN_DIM), BF16),
        scratch_shapes=[pltpu.VMEM((tq, LANES), BF16), pltpu.VMEM((tq, LANES), BF16),
                        pltpu.VMEM((tq, 1), F32), pltpu.VMEM((tq, 1), F32), pltpu.VMEM((tq, V_DIM), F32),
                        pltpu.VMEM((tq, 1), F32), pltpu.VMEM((tq, 1), F32), pltpu.VMEM((tq, V_DIM), F32)],
        compiler_params=pltpu.CompilerParams(dimension_semantics=("arbitrary", "arbitrary", "arbitrary"),
                                             vmem_limit_bytes=VMEM_LIMIT),
        name="attn_prompt",
    )(q, k, v, lw["lq1"], lw["lk1"], lw["lq2"], lw["lk2"], lw["subln_g"])


def _attn_sample_kernel(q_ref, kc_ref, vc_ref, kn_ref, vn_ref, lq1_ref, lk1_ref, lq2_ref, lk2_ref, sg_ref, o_ref,
                        *, lam_init):
    q1, q2 = _split_maps(q_ref[0])
    kc = kc_ref[0].astype(BF16)
    vc = vc_ref[0].astype(BF16)
    kn = kn_ref[0]
    vn = vn_ref[0]
    outs = []
    for qm in (q1, q2):
        sc = _scores(qm, kc)
        sn = _scores(qm, kn)
        m = jnp.maximum(jnp.max(sc, axis=-1, keepdims=True), jnp.max(sn, axis=-1, keepdims=True))
        pc = jnp.exp(sc - m)
        pn = jnp.exp(sn - m)
        l = jnp.sum(pc, axis=-1, keepdims=True) + jnp.sum(pn, axis=-1, keepdims=True)
        acc = (jnp.dot(pc.astype(BF16), vc, preferred_element_type=F32)
               + jnp.dot(pn.astype(BF16), vn, preferred_element_type=F32))
        outs.append(acc / l)
    lam = _lambda(lq1_ref, lk1_ref, lq2_ref, lk2_ref, lam_init)
    o_ref[0] = _sub_ln(outs[0], outs[1], lam, sg_ref, lam_init).astype(BF16)


def _attn_sample(q, k_cache, v_cache, k_new, v_new, lw, lam_init):
    b, t, _ = q.shape
    past = k_cache.shape[1]
    new = pl.BlockSpec((1, t, LANES), lambda i, h: (i, 0, h))
    cache = pl.BlockSpec((1, past, LANES), lambda i, h: (i, 0, h))
    vec = _const_spec((1, HEAD_DIM))
    return pl.pallas_call(
        functools.partial(_attn_sample_kernel, lam_init=lam_init),
        grid=(b, N_HEADS),
        in_specs=[new, cache, cache, new, new, vec, vec, vec, vec, _const_spec((1, V_DIM))],
        out_specs=new,
        out_shape=jax.ShapeDtypeStruct((b, t, ATTN_DIM), BF16),
        compiler_params=pltpu.CompilerParams(dimension_semantics=("arbitrary", "arbitrary"),
                                             vmem_limit_bytes=VMEM_LIMIT),
        name="attn_sample",
    )(q, k_cache, v_cache, k_new, v_new, lw["lq1"], lw["lk1"], lw["lq2"], lw["lk2"], lw["subln_g"])


def _post_kernel(attn_ref, p_ref, g_ref, x_ref, wao_ref, wo_ref, g2_ref, wrt_ref, brt_ref,
                 xm_ref, h2_ref, comb_ref):
    ao = jnp.dot(attn_ref[...], wao_ref[...], preferred_element_type=F32)
    merged = p_ref[...].astype(F32) + g_ref[...].astype(F32) * ao
    xm = x_ref[...] + jnp.dot(merged.astype(BF16), wo_ref[...], preferred_element_type=F32)
    xm_ref[...] = xm
    ms = jnp.mean(xm * xm, axis=-1, keepdims=True)
    h2 = xm * lax.rsqrt(ms + EPS) * g2_ref[...]
    h2_ref[...] = h2.astype(BF16)

    lg = jnp.dot(h2, wrt_ref[...], preferred_element_type=F32, precision=lax.Precision.HIGHEST) + brt_ref[...]
    lane = lax.broadcasted_iota(jnp.int32, lg.shape, 1)
    neg = jnp.full_like(lg, -jnp.inf)
    big = jnp.full_like(lane, LANES)

    def first_max(z):
        zmax = jnp.max(z, axis=-1, keepdims=True)
        return zmax, jnp.min(jnp.where(z == zmax, lane, big), axis=-1, keepdims=True)

    is_group = lane < N_GROUPS
    gmax, gi = first_max(jnp.where(is_group, lg, neg))
    g_w = 1.0 / jnp.sum(jnp.where(is_group, jnp.exp(lg - gmax), 0.0), axis=-1, keepdims=True)
    lo = N_GROUPS + EXPERTS_PER_GROUP * gi
    el = jnp.where((lane >= lo) & (lane < lo + EXPERTS_PER_GROUP), lg, neg)
    v1, i1 = first_max(el)
    v2, i2 = first_max(jnp.where(lane == i1, neg, el))
    e2 = jnp.exp(v2 - v1)
    w1 = g_w / (1.0 + e2)
    w2 = g_w * e2 / (1.0 + e2)
    comb_ref[...] = (jnp.where(lane == i1 - N_GROUPS, w1, 0.0) + jnp.where(lane == i2 - N_GROUPS, w2, 0.0))


def _post(attn, p, g, x, lw, tm):
    t = x.shape[0]
    tok = lambda width: pl.BlockSpec((tm, width), lambda i: (i, 0))
    return pl.pallas_call(
        _post_kernel,
        grid=(t // tm,),
        in_specs=[tok(ATTN_DIM), tok(D_MODEL), tok(D_MODEL), tok(D_MODEL),
                  _const_spec((ATTN_DIM, D_MODEL)), _const_spec((D_MODEL, D_MODEL)), _const_spec((1, D_MODEL)),
                  _const_spec((D_MODEL, LANES)), _const_spec((1, LANES))],
        out_specs=(tok(D_MODEL), tok(D_MODEL), tok(LANES)),
        out_shape=(jax.ShapeDtypeStruct((t, D_MODEL), F32), jax.ShapeDtypeStruct((t, D_MODEL), BF16),
                   jax.ShapeDtypeStruct((t, LANES), F32)),
        compiler_params=pltpu.CompilerParams(dimension_semantics=("arbitrary",), vmem_limit_bytes=VMEM_LIMIT),
        name="post",
    )(attn, p, g, x, lw["w_attn_out"], lw["w_o"], lw["g2"], lw["w_rt"], lw["b_rt"])


def _moe_kernel(h2_ref, comb_ref, xm_ref, wg_ref, wu_ref, wd_ref, y_ref):
    e = pl.program_id(1)

    @pl.when(e == 0)
    def _():
        y_ref[...] = xm_ref[...]

    t = h2_ref[...]
    gate = jnp.dot(t, wg_ref[0], preferred_element_type=F32)
    up = jnp.dot(t, wu_ref[0], preferred_element_type=F32)
    he = (gate * _sigmoid(gate) * up).astype(BF16)
    d = jnp.dot(he, wd_ref[0], preferred_element_type=F32)
    comb = comb_ref[...]
    lane = lax.broadcasted_iota(jnp.int32, comb.shape, 1)
    c = jnp.sum(jnp.where(lane == e, comb, 0.0), axis=-1, keepdims=True)
    y_ref[...] += c * d


def _moe(h2, comb, xm, lw, tm):
    t = h2.shape[0]
    tok = lambda width: pl.BlockSpec((tm, width), lambda i, e: (i, 0))
    return pl.pallas_call(
        _moe_kernel,
        grid=(t // tm, N_EXPERTS),
        in_specs=[tok(D_MODEL), tok(LANES), tok(D_MODEL),
                  pl.BlockSpec((1, D_MODEL, EXPERT_HIDDEN), lambda i, e: (e, 0, 0)),
                  pl.BlockSpec((1, D_MODEL, EXPERT_HIDDEN), lambda i, e: (e, 0, 0)),
                  pl.BlockSpec((1, EXPERT_HIDDEN, D_MODEL), lambda i, e: (e, 0, 0))],
        out_specs=tok(D_MODEL),
        out_shape=jax.ShapeDtypeStruct((t, D_MODEL), F32),
        compiler_params=pltpu.CompilerParams(dimension_semantics=("arbitrary", "arbitrary"),
                                             vmem_limit_bytes=VMEM_LIMIT),
        name="moe",
    )(h2, comb, xm, lw["w_gate"], lw["w_up"], lw["w_down"])


def _layer_weights(l, norm1_g, w_in, conv_dw, conv_db, conv_ln_g, conv_ln_b, w_conv_out, q_norm_g, k_norm_g,
                   lambda_q1, lambda_k1, lambda_q2, lambda_k2, subln_g, w_attn_out, w_o, norm2_g, w_group, b_group,
                   w_router, b_router, w_gate, w_up, w_down):
    row = lambda a: a.reshape(1, -1).astype(F32)
    head_gain = lambda gain: jnp.tile(gain.reshape(-1), N_HEADS).reshape(1, QK_DIM).astype(F32)
    grp = jnp.arange(QK_DIM) // HEAD_DIM
    w_rt = jnp.concatenate([w_group[l], jnp.moveaxis(w_router[l], 0, 1).reshape(D_MODEL, N_EXPERTS)], axis=1)
    b_rt = jnp.concatenate([b_group[l], b_router[l].reshape(-1)])
    pad = LANES - N_GROUPS - N_EXPERTS
    return {
        "g1": row(norm1_g[l]), "w_in": w_in[l].astype(BF16), "conv_dw": conv_dw[l].astype(F32),
        "conv_db": row(conv_db[l]), "conv_ln_g": row(conv_ln_g[l]), "conv_ln_b": row(conv_ln_b[l]),
        "w_conv_out": w_conv_out[l].astype(BF16), "gq": head_gain(q_norm_g[l]), "gk": head_gain(k_norm_g[l]),
        "bd": (grp[:, None] == grp[None, :]).astype(BF16),
        "lq1": row(lambda_q1[l]), "lk1": row(lambda_k1[l]), "lq2": row(lambda_q2[l]), "lk2": row(lambda_k2[l]),
        "subln_g": row(subln_g[l]), "w_attn_out": w_attn_out[l].astype(BF16), "w_o": w_o[l].astype(BF16),
        "g2": row(norm2_g[l]),
        "w_rt": jnp.pad(w_rt.astype(F32), ((0, 0), (0, pad))), "b_rt": jnp.pad(b_rt.astype(F32), (0, pad)).reshape(1, LANES),
        "w_gate": w_gate[l].astype(BF16), "w_up": w_up[l].astype(BF16), "w_down": w_down[l].astype(BF16),
    }


def _tail(x, attn, p, g, lw, tm_post, tm_moe):
    b, s, _ = x.shape
    flat = lambda a: a.reshape(b * s, a.shape[-1])
    xm, h2, comb = _post(flat(attn), flat(p), flat(g), flat(x), lw, tm_post)
    return _moe(h2, comb, xm, lw, tm_moe).reshape(b, s, D_MODEL)


def kernel(x_prompt, x_sample, cache_k, cache_v, state_conv, norm1_g, w_in, conv_dw, conv_db, conv_ln_g, conv_ln_b, w_conv_out, q_norm_g, k_norm_g, lambda_q1, lambda_k1, lambda_q2, lambda_k2, subln_g, w_attn_out, w_o, norm2_g, w_group, b_group, w_router, b_router, w_gate, w_up, w_down):
    depth = w_in.shape[0]
    bp, sp, _ = x_prompt.shape
    bs, ss, _ = x_sample.shape
    past = cache_k.shape[2]
    xp, xs = x_prompt, x_sample
    kp_l, vp_l, cp_l, ks_l, vs_l, cs_l = [], [], [], [], [], []
    for l in range(depth):
        lw = _layer_weights(l, norm1_g, w_in, conv_dw, conv_db, conv_ln_g, conv_ln_b, w_conv_out, q_norm_g,
                            k_norm_g, lambda_q1, lambda_k1, lambda_q2, lambda_k2, subln_g, w_attn_out, w_o,
                            norm2_g, w_group, b_group, w_router, b_router, w_gate, w_up, w_down)
        lam_init = 0.8 - 0.6 * math.exp(-0.3 * l)

        hist_p = jnp.zeros((bp, HIST_ROWS, CONV_DIM), F32)
        q, kf, kb, vf, vb, p, g, cs = _in_proj(xp, hist_p, lw, 512)
        attn = _attn_prompt(q, kb, vb, lw, lam_init, 512)
        xp = _tail(xp, attn, p, g, lw, 512, 1024)
        kp_l.append(kf.reshape(bp, sp, N_HEADS, 2, HEAD_DIM))
        vp_l.append(vf.reshape(bp, sp, N_HEADS, V_DIM))
        cp_l.append(cs[:, HIST_PAD:, :])

        hist_s = jnp.pad(state_conv[l].astype(F32), ((0, 0), (HIST_PAD, 0), (0, 0)))
        q, kf, kb, vf, vb, p, g, cs = _in_proj(xs, hist_s, lw, ss)
        attn = _attn_sample(q, cache_k[l].reshape(bs, past, QK_DIM), cache_v[l].reshape(bs, past, ATTN_DIM),
                            kb, vb, lw, lam_init)
        xs = _tail(xs, attn, p, g, lw, bs * ss, bs * ss)
        ks_l.append(kf.reshape(bs, ss, N_HEADS, 2, HEAD_DIM))
        vs_l.append(vf.reshape(bs, ss, N_HEADS, V_DIM))
        cs_l.append(cs[:, HIST_PAD:, :])
    return (xp, xs, jnp.stack(kp_l), jnp.stack(vp_l), jnp.stack(cp_l),
            jnp.stack(ks_l), jnp.stack(vs_l), jnp.stack(cs_l))
```

```python
import functools
import math

import jax
import jax.numpy as jnp
from jax import lax
from jax.experimental import pallas as pl
from jax.experimental.pallas import tpu as pltpu

D_MODEL = 1024
CHUNK = 64
CONV_DIM = 512
CONV_TAPS = 31
CONV_STATE = CONV_TAPS - 1
N_HEADS = 4
HEAD_DIM = 64
V_DIM = 2 * HEAD_DIM
ATTN_DIM = N_HEADS * V_DIM
QK_DIM = N_HEADS * 2 * HEAD_DIM
N_GROUPS = 4
EXPERTS_PER_GROUP = 4
N_EXPERTS = N_GROUPS * EXPERTS_PER_GROUP
TOP_K_INNER = 2
EXPERT_HIDDEN = 512
EPS = 1e-6
SCALE = HEAD_DIM ** -0.5
LOG2E = math.log2(math.e)
Q_SCALE = SCALE * LOG2E
MAX_FIXED_SHIFT = 40.0
COL_GLU = 2 * CONV_DIM
COL_Q = COL_GLU
COL_K = COL_Q + QK_DIM
COL_V = COL_K + QK_DIM
COL_GC = COL_V + ATTN_DIM
COL_GA = COL_GC + D_MODEL
IN_COLS = COL_GA + D_MODEL

LANES = 128
HIST_ROWS = 32
HIST_PAD = HIST_ROWS - CONV_STATE
CONV_ROW_CHUNK = 64
VMEM_LIMIT = 56 * 1024 * 1024

BF16 = jnp.bfloat16
F32 = jnp.float32


def _sigmoid(x):
    return 1.0 / (1.0 + jnp.exp(-x))


def _const_spec(shape):
    n = len(shape)
    return pl.BlockSpec(shape, lambda *_: (0,) * n)


def _in_proj_kernel(x_ref, hist_ref, g1_ref, w_in_ref, dw_ref, db_ref, lng_ref, lnb_ref, wco_ref, gq_ref, gk_ref,
                    bd_ref, q_ref, kf_ref, kb_ref, vf_ref, vb_ref, p_ref, g_ref, cs_ref, cbuf, ybuf):
    t = pl.program_id(1)
    tm = x_ref.shape[1]
    x = x_ref[0]
    ms = jnp.mean(x * x, axis=-1, keepdims=True)
    h = (x * lax.rsqrt(ms + EPS) * g1_ref[...]).astype(BF16)

    def proj(lo, hi):
        return jnp.dot(h, w_in_ref[:, lo:hi], preferred_element_type=F32)

    u = proj(0, CONV_DIM) * _sigmoid(proj(CONV_DIM, COL_GLU))

    @pl.when(t == 0)
    def _():
        cbuf[0:HIST_ROWS, :] = hist_ref[0]

    @pl.when(t > 0)
    def _():
        cbuf[0:HIST_ROWS, :] = cbuf[tm:tm + HIST_ROWS, :]

    cbuf[HIST_ROWS:HIST_ROWS + tm, :] = u
    cs_ref[0] = cbuf[tm:tm + HIST_ROWS, :]

    rc = min(CONV_ROW_CHUNK, tm)
    for c in range(tm // rc):
        base = c * rc + HIST_PAD
        acc = dw_ref[0:1, :] * cbuf[base:base + rc, :]
        for j in range(1, CONV_TAPS):
            acc = acc + dw_ref[j:j + 1, :] * cbuf[base + j:base + j + rc, :]
        y = acc + db_ref[...]
        mu = jnp.mean(y, axis=-1, keepdims=True)
        yc = y - mu
        var = jnp.mean(yc * yc, axis=-1, keepdims=True)
        yn = yc * lax.rsqrt(var + EPS) * lng_ref[...] + lnb_ref[...]
        ybuf[c * rc:(c + 1) * rc, :] = (yn * _sigmoid(yn)).astype(BF16)
    conv_out = jnp.dot(ybuf[...], wco_ref[...], preferred_element_type=F32)
    p_ref[0] = (_sigmoid(proj(COL_GC, COL_GA)) * conv_out).astype(BF16)
    g_ref[0] = _sigmoid(proj(COL_GA, IN_COLS)).astype(BF16)

    def head_norm(z, gain_ref):
        ss = jnp.dot((z * z).astype(BF16), bd_ref[...], preferred_element_type=F32) * (1.0 / HEAD_DIM)
        return z * lax.rsqrt(ss + EPS) * gain_ref[...]

    q_ref[0] = (head_norm(proj(COL_Q, COL_K), gq_ref) * Q_SCALE).astype(BF16)
    kn = head_norm(proj(COL_K, COL_V), gk_ref)
    kf_ref[0] = kn
    kb_ref[0] = kn.astype(BF16)
    v = proj(COL_V, COL_GC)
    vf_ref[0] = v
    vb_ref[0] = v.astype(BF16)


def _in_proj(x, hist, lw, tm):
    b, s, _ = x.shape
    grid = (b, s // tm)
    tok = lambda width: pl.BlockSpec((1, tm, width), lambda i, j: (i, j, 0))
    out_shape = (
        jax.ShapeDtypeStruct((b, s, QK_DIM), BF16),
        jax.ShapeDtypeStruct((b, s, QK_DIM), F32),
        jax.ShapeDtypeStruct((b, s, QK_DIM), BF16),
        jax.ShapeDtypeStruct((b, s, ATTN_DIM), F32),
        jax.ShapeDtypeStruct((b, s, ATTN_DIM), BF16),
        jax.ShapeDtypeStruct((b, s, D_MODEL), BF16),
        jax.ShapeDtypeStruct((b, s, D_MODEL), BF16),
        jax.ShapeDtypeStruct((b, HIST_ROWS, CONV_DIM), F32),
    )
    return pl.pallas_call(
        _in_proj_kernel,
        grid=grid,
        in_specs=[
            tok(D_MODEL),
            pl.BlockSpec((1, HIST_ROWS, CONV_DIM), lambda i, j: (i, 0, 0)),
            _const_spec((1, D_MODEL)),
            _const_spec((D_MODEL, IN_COLS)),
            _const_spec((CONV_TAPS, CONV_DIM)),
            _const_spec((1, CONV_DIM)),
            _const_spec((1, CONV_DIM)),
            _const_spec((1, CONV_DIM)),
            _const_spec((CONV_DIM, D_MODEL)),
            _const_spec((1, QK_DIM)),
            _const_spec((1, QK_DIM)),
            _const_spec((QK_DIM, QK_DIM)),
        ],
        out_specs=(tok(QK_DIM), tok(QK_DIM), tok(QK_DIM), tok(ATTN_DIM), tok(ATTN_DIM), tok(D_MODEL), tok(D_MODEL),
                   pl.BlockSpec((1, HIST_ROWS, CONV_DIM), lambda i, j: (i, 0, 0))),
        out_shape=out_shape,
        scratch_shapes=[pltpu.VMEM((HIST_ROWS + tm, CONV_DIM), F32), pltpu.VMEM((tm, CONV_DIM), BF16)],
        compiler_params=pltpu.CompilerParams(dimension_semantics=("arbitrary", "arbitrary"),
                                             vmem_limit_bytes=VMEM_LIMIT),
        name="in_proj",
    )(x, hist, lw["g1"], lw["w_in"], lw["conv_dw"], lw["conv_db"], lw["conv_ln_g"], lw["conv_ln_b"],
      lw["w_conv_out"], lw["gq"], lw["gk"], lw["bd"])


def _lambda(lq1_ref, lk1_ref, lq2_ref, lk2_ref, lam_init):
    a = jnp.sum(lq1_ref[...] * lk1_ref[...], axis=-1, keepdims=True)
    b = jnp.sum(lq2_ref[...] * lk2_ref[...], axis=-1, keepdims=True)
    return jnp.exp(a) - jnp.exp(b) + lam_init


def _split_maps(q):
    lane = lax.broadcasted_iota(jnp.int32, q.shape, 1)
    zero = jnp.zeros_like(q)
    return jnp.where(lane < HEAD_DIM, q, zero), jnp.where(lane >= HEAD_DIM, q, zero)


def _scores(qm, k):
    return lax.dot_general(qm, k, (((1,), (1,)), ((), ())), preferred_element_type=F32)


def _sub_ln(o1, o2, lam, sg_ref, lam_init):
    o = o1 - lam * o2
    ms = jnp.mean(o * o, axis=-1, keepdims=True)
    return o * lax.rsqrt(ms + EPS) * sg_ref[...] * (1.0 - lam_init)


def _chunk_mask(s):
    row = lax.broadcasted_iota(jnp.int32, s.shape, 0) // CHUNK
    col = lax.broadcasted_iota(jnp.int32, s.shape, 1) // CHUNK
    return jnp.where(col <= row, s, -jnp.inf)


def _attn_prompt_kernel(shift_ref, q_ref, k_ref, v_ref, lq1_ref, lk1_ref, lq2_ref, lk2_ref, sg_ref, o_ref,
                        qs1, qs2, m1, l1, a1, m2, l2, a2, *, lam_init, online):
    qi = pl.program_id(2)
    tq = q_ref.shape[1]
    q1, q2 = _split_maps(q_ref[0])
    qs1[...] = q1
    qs2[...] = q2
    for m, l, a in ((m1, l1, a1), (m2, l2, a2)):
        m[...] = jnp.full(m.shape, -jnp.inf, F32)
        l[...] = jnp.zeros(l.shape, F32)
        a[...] = jnp.zeros(a.shape, F32)

    def block(kstart, masked):
        k = k_ref[0, pl.ds(kstart, tq), :]
        v = v_ref[0, pl.ds(kstart, tq), :]
        for qs, m, l, a in ((qs1, m1, l1, a1), (qs2, m2, l2, a2)):
            s = _scores(qs[...], k)
            if masked:
                s = _chunk_mask(s)
            if online:
                m_prev = m[...]
                m_new = jnp.maximum(m_prev, jnp.max(s, axis=-1, keepdims=True))
                alpha = jnp.exp2(m_prev - m_new)
                p = jnp.exp2(s - m_new)
                l[...] = alpha * l[...] + jnp.sum(p, axis=-1, keepdims=True)
                a[...] = alpha * a[...] + jnp.dot(p.astype(BF16), v, preferred_element_type=F32)
                m[...] = m_new
            else:
                p = jnp.exp2(s - shift_ref[0])
                part = p[:, 0:LANES]
                for c in range(1, tq // LANES):
                    part = part + p[:, c * LANES:(c + 1) * LANES]
                l[...] += part
                a[...] += jnp.dot(p.astype(BF16), v, preferred_element_type=F32)

    def body(i, carry):
        block(pl.multiple_of(i * tq, tq), False)
        return carry

    lax.fori_loop(0, qi, body, 0)
    block(pl.multiple_of(qi * tq, tq), True)

    lam = _lambda(lq1_ref, lk1_ref, lq2_ref, lk2_ref, lam_init)
    d1 = jnp.sum(l1[...], axis=-1, keepdims=True)
    d2 = jnp.sum(l2[...], axis=-1, keepdims=True)
    o_ref[0] = _sub_ln(a1[...] / d1, a2[...] / d2, lam, sg_ref, lam_init).astype(BF16)


def _attn_prompt(shift, q, k, v, lw, lam_init, tq, online):
    b, s, _ = q.shape
    grid = (b, N_HEADS, s // tq)
    qspec = pl.BlockSpec((1, tq, LANES), lambda i, h, j: (i, j, h))
    kvspec = pl.BlockSpec((1, s, LANES), lambda i, h, j: (i, 0, h))
    vec = _const_spec((1, HEAD_DIM))
    lw_width = 1 if online else LANES
    return pl.pallas_call(
        functools.partial(_attn_prompt_kernel, lam_init=lam_init, online=online),
        grid=grid,
        in_specs=[pl.BlockSpec(memory_space=pltpu.SMEM), qspec, kvspec, kvspec, vec, vec, vec, vec,
                  _const_spec((1, V_DIM))],
        out_specs=qspec,
        out_shape=jax.ShapeDtypeStruct((b, s, ATTN_DIM), BF16),
        scratch_shapes=[pltpu.VMEM((tq, LANES), BF16), pltpu.VMEM((tq, LANES), BF16),
                        pltpu.VMEM((tq, 1), F32), pltpu.VMEM((tq, lw_width), F32), pltpu.VMEM((tq, V_DIM), F32),
                        pltpu.VMEM((tq, 1), F32), pltpu.VMEM((tq, lw_width), F32), pltpu.VMEM((tq, V_DIM), F32)],
        compiler_params=pltpu.CompilerParams(dimension_semantics=("arbitrary", "arbitrary", "arbitrary"),
                                             vmem_limit_bytes=VMEM_LIMIT),
        name="attn_prompt_online" if online else "attn_prompt",
    )(shift, q, k, v, lw["lq1"], lw["lk1"], lw["lq2"], lw["lk2"], lw["subln_g"])


def _attn_sample_kernel(q_ref, kc_ref, vc_ref, kn_ref, vn_ref, lq1_ref, lk1_ref, lq2_ref, lk2_ref, sg_ref, o_ref,
                        *, lam_init):
    q1, q2 = _split_maps(q_ref[0])
    kc = kc_ref[0].astype(BF16)
    vc = vc_ref[0].astype(BF16)
    kn = kn_ref[0]
    vn = vn_ref[0]
    outs = []
    for qm in (q1, q2):
        sc = _scores(qm, kc)
        sn = _scores(qm, kn)
        m = jnp.maximum(jnp.max(sc, axis=-1, keepdims=True), jnp.max(sn, axis=-1, keepdims=True))
        pc = jnp.exp2(sc - m)
        pn = jnp.exp2(sn - m)
        l = jnp.sum(pc, axis=-1, keepdims=True) + jnp.sum(pn, axis=-1, keepdims=True)
        acc = (jnp.dot(pc.astype(BF16), vc, preferred_element_type=F32)
               + jnp.dot(pn.astype(BF16), vn, preferred_element_type=F32))
        outs.append(acc / l)
    lam = _lambda(lq1_ref, lk1_ref, lq2_ref, lk2_ref, lam_init)
    o_ref[0] = _sub_ln(outs[0], outs[1], lam, sg_ref, lam_init).astype(BF16)


def _attn_sample(q, k_cache, v_cache, k_new, v_new, lw, lam_init):
    b, t, _ = q.shape
    past = k_cache.shape[1]
    new = pl.BlockSpec((1, t, LANES), lambda i, h: (i, 0, h))
    cache = pl.BlockSpec((1, past, LANES), lambda i, h: (i, 0, h))
    vec = _const_spec((1, HEAD_DIM))
    return pl.pallas_call(
        functools.partial(_attn_sample_kernel, lam_init=lam_init),
        grid=(b, N_HEADS),
        in_specs=[new, cache, cache, new, new, vec, vec, vec, vec, _const_spec((1, V_DIM))],
        out_specs=new,
        out_shape=jax.ShapeDtypeStruct((b, t, ATTN_DIM), BF16),
        compiler_params=pltpu.CompilerParams(dimension_semantics=("arbitrary", "arbitrary"),
                                             vmem_limit_bytes=VMEM_LIMIT),
        name="attn_sample",
    )(q, k_cache, v_cache, k_new, v_new, lw["lq1"], lw["lk1"], lw["lq2"], lw["lk2"], lw["subln_g"])


def _post_kernel(attn_ref, p_ref, g_ref, x_ref, wao_ref, wo_ref, g2_ref, wrt_ref, brt_ref,
                 xm_ref, h2_ref, comb_ref):
    ao = jnp.dot(attn_ref[...], wao_ref[...], preferred_element_type=F32)
    merged = p_ref[...].astype(F32) + g_ref[...].astype(F32) * ao
    xm = x_ref[...] + jnp.dot(merged.astype(BF16), wo_ref[...], preferred_element_type=F32)
    xm_ref[...] = xm
    ms = jnp.mean(xm * xm, axis=-1, keepdims=True)
    h2 = xm * lax.rsqrt(ms + EPS) * g2_ref[...]
    h2_ref[...] = h2.astype(BF16)

    lg = jnp.dot(h2, wrt_ref[...], preferred_element_type=F32, precision=lax.Precision.HIGHEST) + brt_ref[...]
    lane = lax.broadcasted_iota(jnp.int32, lg.shape, 1)
    neg = jnp.full_like(lg, -jnp.inf)
    big = jnp.full_like(lane, LANES)

    def first_max(z):
        zmax = jnp.max(z, axis=-1, keepdims=True)
        return zmax, jnp.min(jnp.where(z == zmax, lane, big), axis=-1, keepdims=True)

    is_group = lane < N_GROUPS
    gmax, gi = first_max(jnp.where(is_group, lg, neg))
    g_w = 1.0 / jnp.sum(jnp.where(is_group, jnp.exp(lg - gmax), 0.0), axis=-1, keepdims=True)
    lo = N_GROUPS + EXPERTS_PER_GROUP * gi
    el = jnp.where((lane >= lo) & (lane < lo + EXPERTS_PER_GROUP), lg, neg)
    v1, i1 = first_max(el)
    v2, i2 = first_max(jnp.where(lane == i1, neg, el))
    e2 = jnp.exp(v2 - v1)
    w1 = g_w / (1.0 + e2)
    w2 = g_w * e2 / (1.0 + e2)
    comb_ref[...] = (jnp.where(lane == i1 - N_GROUPS, w1, 0.0) + jnp.where(lane == i2 - N_GROUPS, w2, 0.0))


def _post(attn, p, g, x, lw, tm):
    t = x.shape[0]
    tok = lambda width: pl.BlockSpec((tm, width), lambda i: (i, 0))
    return pl.pallas_call(
        _post_kernel,
        grid=(t // tm,),
        in_specs=[tok(ATTN_DIM), tok(D_MODEL), tok(D_MODEL), tok(D_MODEL),
                  _const_spec((ATTN_DIM, D_MODEL)), _const_spec((D_MODEL, D_MODEL)), _const_spec((1, D_MODEL)),
                  _const_spec((D_MODEL, LANES)), _const_spec((1, LANES))],
        out_specs=(tok(D_MODEL), tok(D_MODEL), tok(LANES)),
        out_shape=(jax.ShapeDtypeStruct((t, D_MODEL), F32), jax.ShapeDtypeStruct((t, D_MODEL), BF16),
                   jax.ShapeDtypeStruct((t, LANES), F32)),
        compiler_params=pltpu.CompilerParams(dimension_semantics=("arbitrary",), vmem_limit_bytes=VMEM_LIMIT),
        name="post",
    )(attn, p, g, x, lw["w_attn_out"], lw["w_o"], lw["g2"], lw["w_rt"], lw["b_rt"])


def _moe_kernel(h2_ref, comb_ref, xm_ref, wg_ref, wu_ref, wd_ref, y_ref):
    e = pl.program_id(1)

    @pl.when(e == 0)
    def _():
        y_ref[...] = xm_ref[...]

    t = h2_ref[...]
    gate = jnp.dot(t, wg_ref[0], preferred_element_type=F32)
    up = jnp.dot(t, wu_ref[0], preferred_element_type=F32)
    he = (gate * _sigmoid(gate) * up).astype(BF16)
    d = jnp.dot(he, wd_ref[0], preferred_element_type=F32)
    comb = comb_ref[...]
    lane = lax.broadcasted_iota(jnp.int32, comb.shape, 1)
    c = jnp.sum(jnp.where(lane == e, comb, 0.0), axis=-1, keepdims=True)
    y_ref[...] += c * d


def _moe(h2, comb, xm, lw, tm):
    t = h2.shape[0]
    tok = lambda width: pl.BlockSpec((tm, width), lambda i, e: (i, 0))
    return pl.pallas_call(
        _moe_kernel,
        grid=(t // tm, N_EXPERTS),
        in_specs=[tok(D_MODEL), tok(LANES), tok(D_MODEL),
                  pl.BlockSpec((1, D_MODEL, EXPERT_HIDDEN), lambda i, e: (e, 0, 0)),
                  pl.BlockSpec((1, D_MODEL, EXPERT_HIDDEN), lambda i, e: (e, 0, 0)),
                  pl.BlockSpec((1, EXPERT_HIDDEN, D_MODEL), lambda i, e: (e, 0, 0))],
        out_specs=tok(D_MODEL),
        out_shape=jax.ShapeDtypeStruct((t, D_MODEL), F32),
        compiler_params=pltpu.CompilerParams(dimension_semantics=("arbitrary", "arbitrary"),
                                             vmem_limit_bytes=VMEM_LIMIT),
        name="moe",
    )(h2, comb, xm, lw["w_gate"], lw["w_up"], lw["w_down"])


def _layer_weights(l, norm1_g, w_in, conv_dw, conv_db, conv_ln_g, conv_ln_b, w_conv_out, q_norm_g, k_norm_g,
                   lambda_q1, lambda_k1, lambda_q2, lambda_k2, subln_g, w_attn_out, w_o, norm2_g, w_group, b_group,
                   w_router, b_router, w_gate, w_up, w_down):
    row = lambda a: a.reshape(1, -1).astype(F32)
    head_gain = lambda gain: jnp.tile(gain.reshape(-1), N_HEADS).reshape(1, QK_DIM).astype(F32)
    grp = jnp.arange(QK_DIM) // HEAD_DIM
    w_rt = jnp.concatenate([w_group[l], jnp.moveaxis(w_router[l], 0, 1).reshape(D_MODEL, N_EXPERTS)], axis=1)
    b_rt = jnp.concatenate([b_group[l], b_router[l].reshape(-1)])
    pad = LANES - N_GROUPS - N_EXPERTS
    return {
        "g1": row(norm1_g[l]), "w_in": w_in[l].astype(BF16), "conv_dw": conv_dw[l].astype(F32),
        "conv_db": row(conv_db[l]), "conv_ln_g": row(conv_ln_g[l]), "conv_ln_b": row(conv_ln_b[l]),
        "w_conv_out": w_conv_out[l].astype(BF16), "gq": head_gain(q_norm_g[l]), "gk": head_gain(k_norm_g[l]),
        "bd": (grp[:, None] == grp[None, :]).astype(BF16),
        "lq1": row(lambda_q1[l]), "lk1": row(lambda_k1[l]), "lq2": row(lambda_q2[l]), "lk2": row(lambda_k2[l]),
        "subln_g": row(subln_g[l]), "w_attn_out": w_attn_out[l].astype(BF16), "w_o": w_o[l].astype(BF16),
        "g2": row(norm2_g[l]),
        "w_rt": jnp.pad(w_rt.astype(F32), ((0, 0), (0, pad))), "b_rt": jnp.pad(b_rt.astype(F32), (0, pad)).reshape(1, LANES),
        "w_gate": w_gate[l].astype(BF16), "w_up": w_up[l].astype(BF16), "w_down": w_down[l].astype(BF16),
    }


def _tail(x, attn, p, g, lw, tm_post, tm_moe):
    b, s, _ = x.shape
    flat = lambda a: a.reshape(b * s, a.shape[-1])
    xm, h2, comb = _post(flat(attn), flat(p), flat(g), flat(x), lw, tm_post)
    return _moe(h2, comb, xm, lw, tm_moe).reshape(b, s, D_MODEL)


def kernel(x_prompt, x_sample, cache_k, cache_v, state_conv, norm1_g, w_in, conv_dw, conv_db, conv_ln_g, conv_ln_b, w_conv_out, q_norm_g, k_norm_g, lambda_q1, lambda_k1, lambda_q2, lambda_k2, subln_g, w_attn_out, w_o, norm2_g, w_group, b_group, w_router, b_router, w_gate, w_up, w_down):
    depth = w_in.shape[0]
    bp, sp, _ = x_prompt.shape
    bs, ss, _ = x_sample.shape
    past = cache_k.shape[2]
    xp, xs = x_prompt, x_sample
    kp_l, vp_l, cp_l, ks_l, vs_l, cs_l = [], [], [], [], [], []
    for l in range(depth):
        lw = _layer_weights(l, norm1_g, w_in, conv_dw, conv_db, conv_ln_g, conv_ln_b, w_conv_out, q_norm_g,
                            k_norm_g, lambda_q1, lambda_k1, lambda_q2, lambda_k2, subln_g, w_attn_out, w_o,
                            norm2_g, w_group, b_group, w_router, b_router, w_gate, w_up, w_down)
        lam_init = 0.8 - 0.6 * math.exp(-0.3 * l)

        hist_p = jnp.zeros((bp, HIST_ROWS, CONV_DIM), F32)
        q, kf, kb, vf, vb, p, g, cs = _in_proj(xp, hist_p, lw, 512)
        bound = (HEAD_DIM * SCALE) * jnp.max(jnp.abs(lw["gq"])) * jnp.max(jnp.abs(lw["gk"]))
        shift = (bound * LOG2E).reshape(1)
        attn = lax.cond(bound <= MAX_FIXED_SHIFT,
                        lambda *a: _attn_prompt(*a, lw, lam_init, 512, False),
                        lambda *a: _attn_prompt(*a, lw, lam_init, 512, True),
                        shift, q, kb, vb)
        xp = _tail(xp, attn, p, g, lw, 512, 1024)
        kp_l.append(kf.reshape(bp, sp, N_HEADS, 2, HEAD_DIM))
        vp_l.append(vf.reshape(bp, sp, N_HEADS, V_DIM))
        cp_l.append(cs[:, HIST_PAD:, :])

        hist_s = jnp.pad(state_conv[l].astype(F32), ((0, 0), (HIST_PAD, 0), (0, 0)))
        q, kf, kb, vf, vb, p, g, cs = _in_proj(xs, hist_s, lw, ss)
        attn = _attn_sample(q, cache_k[l].reshape(bs, past, QK_DIM), cache_v[l].reshape(bs, past, ATTN_DIM),
                            kb, vb, lw, lam_init)
        xs = _tail(xs, attn, p, g, lw, bs * ss, bs * ss)
        ks_l.append(kf.reshape(bs, ss, N_HEADS, 2, HEAD_DIM))
        vs_l.append(vf.reshape(bs, ss, N_HEADS, V_DIM))
        cs_l.append(cs[:, HIST_PAD:, :])
    return (xp, xs, jnp.stack(kp_l), jnp.stack(vp_l), jnp.stack(cp_l),
            jnp.stack(ks_l), jnp.stack(vs_l), jnp.stack(cs_l))
```

```python
import functools
import math

import jax
import jax.numpy as jnp
from jax import lax
from jax.experimental import pallas as pl
from jax.experimental.pallas import tpu as pltpu

D_MODEL = 1024
CHUNK = 64
CONV_DIM = 512
CONV_TAPS = 31
CONV_STATE = CONV_TAPS - 1
N_HEADS = 4
HEAD_DIM = 64
V_DIM = 2 * HEAD_DIM
ATTN_DIM = N_HEADS * V_DIM
QK_DIM = N_HEADS * 2 * HEAD_DIM
N_GROUPS = 4
EXPERTS_PER_GROUP = 4
N_EXPERTS = N_GROUPS * EXPERTS_PER_GROUP
TOP_K_INNER = 2
EXPERT_HIDDEN = 512
EPS = 1e-6
SCALE = HEAD_DIM ** -0.5
LOG2E = math.log2(math.e)
Q_SCALE = SCALE * LOG2E
MAX_FIXED_SHIFT = 40.0
COL_GLU = 2 * CONV_DIM
COL_Q = COL_GLU
COL_K = COL_Q + QK_DIM
COL_V = COL_K + QK_DIM
COL_GC = COL_V + ATTN_DIM
COL_GA = COL_GC + D_MODEL
IN_COLS = COL_GA + D_MODEL

LANES = 128
HIST_ROWS = 32
HIST_PAD = HIST_ROWS - CONV_STATE
CONV_ROW_CHUNK = 64
VMEM_LIMIT = 56 * 1024 * 1024
GROUP_LANE = N_EXPERTS
ROW_WIDTH = D_MODEL + LANES
MOE_TILE = 512

BF16 = jnp.bfloat16
F32 = jnp.float32


def _sigmoid(x):
    return 1.0 / (1.0 + jnp.exp(-x))


def _const_spec(shape):
    n = len(shape)
    return pl.BlockSpec(shape, lambda *_: (0,) * n)


def _in_proj_kernel(x_ref, hist_ref, g1_ref, w_in_ref, dw_ref, db_ref, lng_ref, lnb_ref, wco_ref, gq_ref, gk_ref,
                    bd_ref, q_ref, kf_ref, kb_ref, vf_ref, vb_ref, p_ref, g_ref, cs_ref, cbuf, ybuf):
    t = pl.program_id(1)
    tm = x_ref.shape[1]
    x = x_ref[0]
    ms = jnp.mean(x * x, axis=-1, keepdims=True)
    h = (x * lax.rsqrt(ms + EPS) * g1_ref[...]).astype(BF16)

    def proj(lo, hi):
        return jnp.dot(h, w_in_ref[:, lo:hi], preferred_element_type=F32)

    u = proj(0, CONV_DIM) * _sigmoid(proj(CONV_DIM, COL_GLU))

    @pl.when(t == 0)
    def _():
        cbuf[0:HIST_ROWS, :] = hist_ref[0]

    @pl.when(t > 0)
    def _():
        cbuf[0:HIST_ROWS, :] = cbuf[tm:tm + HIST_ROWS, :]

    cbuf[HIST_ROWS:HIST_ROWS + tm, :] = u
    cs_ref[0] = cbuf[tm:tm + HIST_ROWS, :]

    rc = min(CONV_ROW_CHUNK, tm)
    for c in range(tm // rc):
        base = c * rc + HIST_PAD
        acc = dw_ref[0:1, :] * cbuf[base:base + rc, :]
        for j in range(1, CONV_TAPS):
            acc = acc + dw_ref[j:j + 1, :] * cbuf[base + j:base + j + rc, :]
        y = acc + db_ref[...]
        mu = jnp.mean(y, axis=-1, keepdims=True)
        yc = y - mu
        var = jnp.mean(yc * yc, axis=-1, keepdims=True)
        yn = yc * lax.rsqrt(var + EPS) * lng_ref[...] + lnb_ref[...]
        ybuf[c * rc:(c + 1) * rc, :] = (yn * _sigmoid(yn)).astype(BF16)
    conv_out = jnp.dot(ybuf[...], wco_ref[...], preferred_element_type=F32)
    p_ref[0] = (_sigmoid(proj(COL_GC, COL_GA)) * conv_out).astype(BF16)
    g_ref[0] = _sigmoid(proj(COL_GA, IN_COLS)).astype(BF16)

    def head_norm(z, gain_ref):
        ss = jnp.dot((z * z).astype(BF16), bd_ref[...], preferred_element_type=F32) * (1.0 / HEAD_DIM)
        return z * lax.rsqrt(ss + EPS) * gain_ref[...]

    q_ref[0] = (head_norm(proj(COL_Q, COL_K), gq_ref) * Q_SCALE).astype(BF16)
    kn = head_norm(proj(COL_K, COL_V), gk_ref)
    kf_ref[0] = kn
    kb_ref[0] = kn.astype(BF16)
    v = proj(COL_V, COL_GC)
    vf_ref[0] = v
    vb_ref[0] = v.astype(BF16)


def _in_proj(x, hist, lw, tm):
    b, s, _ = x.shape
    grid = (b, s // tm)
    tok = lambda width: pl.BlockSpec((1, tm, width), lambda i, j: (i, j, 0))
    out_shape = (
        jax.ShapeDtypeStruct((b, s, QK_DIM), BF16),
        jax.ShapeDtypeStruct((b, s, QK_DIM), F32),
        jax.ShapeDtypeStruct((b, s, QK_DIM), BF16),
        jax.ShapeDtypeStruct((b, s, ATTN_DIM), F32),
        jax.ShapeDtypeStruct((b, s, ATTN_DIM), BF16),
        jax.ShapeDtypeStruct((b, s, D_MODEL), BF16),
        jax.ShapeDtypeStruct((b, s, D_MODEL), BF16),
        jax.ShapeDtypeStruct((b, HIST_ROWS, CONV_DIM), F32),
    )
    return pl.pallas_call(
        _in_proj_kernel,
        grid=grid,
        in_specs=[
            tok(D_MODEL),
            pl.BlockSpec((1, HIST_ROWS, CONV_DIM), lambda i, j: (i, 0, 0)),
            _const_spec((1, D_MODEL)),
            _const_spec((D_MODEL, IN_COLS)),
            _const_spec((CONV_TAPS, CONV_DIM)),
            _const_spec((1, CONV_DIM)),
            _const_spec((1, CONV_DIM)),
            _const_spec((1, CONV_DIM)),
            _const_spec((CONV_DIM, D_MODEL)),
            _const_spec((1, QK_DIM)),
            _const_spec((1, QK_DIM)),
            _const_spec((QK_DIM, QK_DIM)),
        ],
        out_specs=(tok(QK_DIM), tok(QK_DIM), tok(QK_DIM), tok(ATTN_DIM), tok(ATTN_DIM), tok(D_MODEL), tok(D_MODEL),
                   pl.BlockSpec((1, HIST_ROWS, CONV_DIM), lambda i, j: (i, 0, 0))),
        out_shape=out_shape,
        scratch_shapes=[pltpu.VMEM((HIST_ROWS + tm, CONV_DIM), F32), pltpu.VMEM((tm, CONV_DIM), BF16)],
        compiler_params=pltpu.CompilerParams(dimension_semantics=("arbitrary", "arbitrary"),
                                             vmem_limit_bytes=VMEM_LIMIT),
        name="in_proj",
    )(x, hist, lw["g1"], lw["w_in"], lw["conv_dw"], lw["conv_db"], lw["conv_ln_g"], lw["conv_ln_b"],
      lw["w_conv_out"], lw["gq"], lw["gk"], lw["bd"])


def _lambda(lq1_ref, lk1_ref, lq2_ref, lk2_ref, lam_init):
    a = jnp.sum(lq1_ref[...] * lk1_ref[...], axis=-1, keepdims=True)
    b = jnp.sum(lq2_ref[...] * lk2_ref[...], axis=-1, keepdims=True)
    return jnp.exp(a) - jnp.exp(b) + lam_init


def _split_maps(q):
    lane = lax.broadcasted_iota(jnp.int32, q.shape, 1)
    zero = jnp.zeros_like(q)
    return jnp.where(lane < HEAD_DIM, q, zero), jnp.where(lane >= HEAD_DIM, q, zero)


def _scores(qm, k):
    return lax.dot_general(qm, k, (((1,), (1,)), ((), ())), preferred_element_type=F32)


def _sub_ln(o1, o2, lam, sg_ref, lam_init):
    o = o1 - lam * o2
    ms = jnp.mean(o * o, axis=-1, keepdims=True)
    return o * lax.rsqrt(ms + EPS) * sg_ref[...] * (1.0 - lam_init)


def _chunk_mask(s):
    row = lax.broadcasted_iota(jnp.int32, s.shape, 0) // CHUNK
    col = lax.broadcasted_iota(jnp.int32, s.shape, 1) // CHUNK
    return jnp.where(col <= row, s, -jnp.inf)


def _attn_prompt_kernel(shift_ref, q_ref, k_ref, v_ref, lq1_ref, lk1_ref, lq2_ref, lk2_ref, sg_ref, o_ref,
                        qs1, qs2, m1, l1, a1, m2, l2, a2, *, lam_init, online):
    qi = pl.program_id(2)
    tq = q_ref.shape[1]
    q1, q2 = _split_maps(q_ref[0])
    qs1[...] = q1
    qs2[...] = q2
    for m, l, a in ((m1, l1, a1), (m2, l2, a2)):
        m[...] = jnp.full(m.shape, -jnp.inf, F32)
        l[...] = jnp.zeros(l.shape, F32)
        a[...] = jnp.zeros(a.shape, F32)

    def block(kstart, masked):
        k = k_ref[0, pl.ds(kstart, tq), :]
        v = v_ref[0, pl.ds(kstart, tq), :]
        for qs, m, l, a in ((qs1, m1, l1, a1), (qs2, m2, l2, a2)):
            s = _scores(qs[...], k)
            if masked:
                s = _chunk_mask(s)
            if online:
                m_prev = m[...]
                m_new = jnp.maximum(m_prev, jnp.max(s, axis=-1, keepdims=True))
                alpha = jnp.exp2(m_prev - m_new)
                p = jnp.exp2(s - m_new)
                l[...] = alpha * l[...] + jnp.sum(p, axis=-1, keepdims=True)
                a[...] = alpha * a[...] + jnp.dot(p.astype(BF16), v, preferred_element_type=F32)
                m[...] = m_new
            else:
                p = jnp.exp2(s - shift_ref[0])
                part = p[:, 0:LANES]
                for c in range(1, tq // LANES):
                    part = part + p[:, c * LANES:(c + 1) * LANES]
                l[...] += part
                a[...] += jnp.dot(p.astype(BF16), v, preferred_element_type=F32)

    def body(i, carry):
        block(pl.multiple_of(i * tq, tq), False)
        return carry

    lax.fori_loop(0, qi, body, 0)
    block(pl.multiple_of(qi * tq, tq), True)

    lam = _lambda(lq1_ref, lk1_ref, lq2_ref, lk2_ref, lam_init)
    d1 = jnp.sum(l1[...], axis=-1, keepdims=True)
    d2 = jnp.sum(l2[...], axis=-1, keepdims=True)
    o_ref[0] = _sub_ln(a1[...] / d1, a2[...] / d2, lam, sg_ref, lam_init).astype(BF16)


def _attn_prompt(shift, q, k, v, lw, lam_init, tq, online):
    b, s, _ = q.shape
    grid = (b, N_HEADS, s // tq)
    qspec = pl.BlockSpec((1, tq, LANES), lambda i, h, j: (i, j, h))
    kvspec = pl.BlockSpec((1, s, LANES), lambda i, h, j: (i, 0, h))
    vec = _const_spec((1, HEAD_DIM))
    lw_width = 1 if online else LANES
    return pl.pallas_call(
        functools.partial(_attn_prompt_kernel, lam_init=lam_init, online=online),
        grid=grid,
        in_specs=[pl.BlockSpec(memory_space=pltpu.SMEM), qspec, kvspec, kvspec, vec, vec, vec, vec,
                  _const_spec((1, V_DIM))],
        out_specs=qspec,
        out_shape=jax.ShapeDtypeStruct((b, s, ATTN_DIM), BF16),
        scratch_shapes=[pltpu.VMEM((tq, LANES), BF16), pltpu.VMEM((tq, LANES), BF16),
                        pltpu.VMEM((tq, 1), F32), pltpu.VMEM((tq, lw_width), F32), pltpu.VMEM((tq, V_DIM), F32),
                        pltpu.VMEM((tq, 1), F32), pltpu.VMEM((tq, lw_width), F32), pltpu.VMEM((tq, V_DIM), F32)],
        compiler_params=pltpu.CompilerParams(dimension_semantics=("arbitrary", "arbitrary", "arbitrary"),
                                             vmem_limit_bytes=VMEM_LIMIT),
        name="attn_prompt_online" if online else "attn_prompt",
    )(shift, q, k, v, lw["lq1"], lw["lk1"], lw["lq2"], lw["lk2"], lw["subln_g"])


def _attn_sample_kernel(q_ref, kc_ref, vc_ref, kn_ref, vn_ref, lq1_ref, lk1_ref, lq2_ref, lk2_ref, sg_ref, o_ref,
                        *, lam_init):
    q1, q2 = _split_maps(q_ref[0])
    kc = kc_ref[0].astype(BF16)
    vc = vc_ref[0].astype(BF16)
    kn = kn_ref[0]
    vn = vn_ref[0]
    outs = []
    for qm in (q1, q2):
        sc = _scores(qm, kc)
        sn = _scores(qm, kn)
        m = jnp.maximum(jnp.max(sc, axis=-1, keepdims=True), jnp.max(sn, axis=-1, keepdims=True))
        pc = jnp.exp2(sc - m)
        pn = jnp.exp2(sn - m)
        l = jnp.sum(pc, axis=-1, keepdims=True) + jnp.sum(pn, axis=-1, keepdims=True)
        acc = (jnp.dot(pc.astype(BF16), vc, preferred_element_type=F32)
               + jnp.dot(pn.astype(BF16), vn, preferred_element_type=F32))
        outs.append(acc / l)
    lam = _lambda(lq1_ref, lk1_ref, lq2_ref, lk2_ref, lam_init)
    o_ref[0] = _sub_ln(outs[0], outs[1], lam, sg_ref, lam_init).astype(BF16)


def _attn_sample(q, k_cache, v_cache, k_new, v_new, lw, lam_init):
    b, t, _ = q.shape
    past = k_cache.shape[1]
    new = pl.BlockSpec((1, t, LANES), lambda i, h: (i, 0, h))
    cache = pl.BlockSpec((1, past, LANES), lambda i, h: (i, 0, h))
    vec = _const_spec((1, HEAD_DIM))
    return pl.pallas_call(
        functools.partial(_attn_sample_kernel, lam_init=lam_init),
        grid=(b, N_HEADS),
        in_specs=[new, cache, cache, new, new, vec, vec, vec, vec, _const_spec((1, V_DIM))],
        out_specs=new,
        out_shape=jax.ShapeDtypeStruct((b, t, ATTN_DIM), BF16),
        compiler_params=pltpu.CompilerParams(dimension_semantics=("arbitrary", "arbitrary"),
                                             vmem_limit_bytes=VMEM_LIMIT),
        name="attn_sample",
    )(q, k_cache, v_cache, k_new, v_new, lw["lq1"], lw["lk1"], lw["lq2"], lw["lk2"], lw["subln_g"])


def _post_kernel(attn_ref, p_ref, g_ref, x_ref, wao_ref, wo_ref, g2_ref, wrt_ref, brt_ref, xm_ref, *out_refs, routed):
    tm = x_ref.shape[0]
    ao = jnp.dot(attn_ref[...], wao_ref[...], preferred_element_type=F32)
    merged = p_ref[...].astype(F32) + g_ref[...].astype(F32) * ao
    xm = x_ref[...] + jnp.dot(merged.astype(BF16), wo_ref[...], preferred_element_type=F32)
    xm_ref[...] = xm
    ms = jnp.mean(xm * xm, axis=-1, keepdims=True)
    h2 = xm * lax.rsqrt(ms + EPS) * g2_ref[...]

    h_hi = h2.astype(BF16)
    h_lo = (h2 - h_hi.astype(F32)).astype(BF16)
    r = (jnp.dot(h_hi, wrt_ref[...], preferred_element_type=F32)
         + jnp.dot(h_lo, wrt_ref[...], preferred_element_type=F32))
    lg = r[:, :LANES] + r[:, LANES:] + brt_ref[...]
    lane = lax.broadcasted_iota(jnp.int32, lg.shape, 1)
    neg = jnp.full_like(lg, -jnp.inf)
    big = jnp.full_like(lane, LANES)

    def first_max(z):
        zmax = jnp.max(z, axis=-1, keepdims=True)
        return zmax, jnp.min(jnp.where(z == zmax, lane, big), axis=-1, keepdims=True)

    is_group = lane < N_GROUPS
    gmax, gi = first_max(jnp.where(is_group, lg, neg))
    g_w = 1.0 / jnp.sum(jnp.where(is_group, jnp.exp(lg - gmax), 0.0), axis=-1, keepdims=True)
    lo = N_GROUPS + EXPERTS_PER_GROUP * gi
    el = jnp.where((lane >= lo) & (lane < lo + EXPERTS_PER_GROUP), lg, neg)
    v1, i1 = first_max(el)
    v2, i2 = first_max(jnp.where(lane == i1, neg, el))
    e2 = jnp.exp(v2 - v1)
    w1 = g_w / (1.0 + e2)
    w2 = g_w * e2 / (1.0 + e2)
    comb = jnp.where(lane == i1 - N_GROUPS, w1, 0.0) + jnp.where(lane == i2 - N_GROUPS, w2, 0.0)
    if routed:
        rows_ref, = out_refs
        comb = jnp.where(lane == GROUP_LANE, gi.astype(F32), comb)
        rows_ref[:, :, 0:D_MODEL] = h2.reshape(tm, 1, D_MODEL)
        rows_ref[:, :, D_MODEL:ROW_WIDTH] = comb.reshape(tm, 1, LANES)
    else:
        h2_ref, comb_ref = out_refs
        h2_ref[...] = h_hi
        comb_ref[...] = comb


def _post(attn, p, g, x, lw, tm, routed):
    t = x.shape[0]
    tok = lambda width: pl.BlockSpec((tm, width), lambda i: (i, 0))
    if routed:
        out_specs = (tok(D_MODEL), pl.BlockSpec((tm, 1, ROW_WIDTH), lambda i: (i, 0, 0)))
        out_shape = (jax.ShapeDtypeStruct((t, D_MODEL), F32), jax.ShapeDtypeStruct((t, 1, ROW_WIDTH), F32))
    else:
        out_specs = (tok(D_MODEL), tok(D_MODEL), tok(LANES))
        out_shape = (jax.ShapeDtypeStruct((t, D_MODEL), F32), jax.ShapeDtypeStruct((t, D_MODEL), BF16),
                     jax.ShapeDtypeStruct((t, LANES), F32))
    return pl.pallas_call(
        functools.partial(_post_kernel, routed=routed),
        grid=(t // tm,),
        in_specs=[tok(ATTN_DIM), tok(D_MODEL), tok(D_MODEL), tok(D_MODEL),
                  _const_spec((ATTN_DIM, D_MODEL)), _const_spec((D_MODEL, D_MODEL)), _const_spec((1, D_MODEL)),
                  _const_spec((D_MODEL, 2 * LANES)), _const_spec((1, LANES))],
        out_specs=out_specs,
        out_shape=out_shape,
        compiler_params=pltpu.CompilerParams(dimension_semantics=("arbitrary",), vmem_limit_bytes=VMEM_LIMIT),
        name="post_routed" if routed else "post",
    )(attn, p, g, x, lw["w_attn_out"], lw["w_o"], lw["g2"], lw["w_rt"], lw["b_rt"])


def _row_copies_wait(src_row, dst_row, sem, n):
    def wait(_, c):
        pltpu.make_async_copy(src_row, dst_row, sem).wait()
        return c
    lax.fori_loop(0, n, wait, 0)


def _dispatch_kernel(src_ref, rows_hbm, sorted_hbm, sems):
    j = pl.program_id(0)
    n = pl.num_programs(0)
    tile = src_ref.shape[2]
    slot = j % 2

    def start(r, c):
        pltpu.make_async_copy(rows_hbm.at[src_ref[0, 0, r]], sorted_hbm.at[j * tile + r], sems.at[slot]).start()
        return c
    lax.fori_loop(0, tile, start, 0)

    @pl.when(j > 0)
    def _():
        _row_copies_wait(rows_hbm.at[0], sorted_hbm.at[0], sems.at[1 - slot], tile)

    @pl.when(j == n - 1)
    def _():
        _row_copies_wait(rows_hbm.at[0], sorted_hbm.at[0], sems.at[slot], tile)


def _dispatch(src, rows, tile):
    n_tiles = src.shape[0]
    return pl.pallas_call(
        _dispatch_kernel,
        grid=(n_tiles,),
        in_specs=[pl.BlockSpec((1, 1, tile), lambda j: (j, 0, 0), memory_space=pltpu.SMEM),
                  pl.BlockSpec(memory_space=pl.ANY)],
        out_specs=pl.BlockSpec(memory_space=pl.ANY),
        out_shape=jax.ShapeDtypeStruct((n_tiles * tile, 1, ROW_WIDTH), F32),
        scratch_shapes=[pltpu.SemaphoreType.DMA((2,))],
        compiler_params=pltpu.CompilerParams(dimension_semantics=("arbitrary",)),
        name="moe_dispatch",
    )(src, rows)


def _moe_routed_kernel(group_ref, rows_ref, wg_ref, wu_ref, wd_ref, out_ref):
    tile = rows_ref.shape[0]
    g = group_ref[pl.program_id(0)]
    rows = rows_ref[...].reshape(tile, ROW_WIDTH)
    t = rows[:, 0:D_MODEL].astype(BF16)
    comb = rows[:, D_MODEL:ROW_WIDTH]
    lane = lax.broadcasted_iota(jnp.int32, comb.shape, 1)
    out = jnp.zeros((tile, D_MODEL), F32)
    for e in range(EXPERTS_PER_GROUP):
        gate = jnp.dot(t, wg_ref[0, e], preferred_element_type=F32)
        up = jnp.dot(t, wu_ref[0, e], preferred_element_type=F32)
        he = (gate * _sigmoid(gate) * up).astype(BF16)
        d = jnp.dot(he, wd_ref[0, e], preferred_element_type=F32)
        c = jnp.sum(jnp.where(lane == g * EXPERTS_PER_GROUP + e, comb, 0.0), axis=-1, keepdims=True)
        out = out + c * d
    out_ref[...] = out.reshape(tile, 1, D_MODEL)


def _moe_routed(tile_group, sorted_rows, lw, tile):
    n_tiles = tile_group.shape[0]
    grouped = lambda w: w.reshape(N_GROUPS, EXPERTS_PER_GROUP, *w.shape[1:])
    wspec = lambda a, b: pl.BlockSpec((1, EXPERTS_PER_GROUP, a, b), lambda j, grp: (grp[j], 0, 0, 0))
    return pl.pallas_call(
        _moe_routed_kernel,
        grid_spec=pltpu.PrefetchScalarGridSpec(
            num_scalar_prefetch=1,
            grid=(n_tiles,),
            in_specs=[pl.BlockSpec((tile, 1, ROW_WIDTH), lambda j, grp: (j, 0, 0)),
                      wspec(D_MODEL, EXPERT_HIDDEN), wspec(D_MODEL, EXPERT_HIDDEN), wspec(EXPERT_HIDDEN, D_MODEL)],
            out_specs=pl.BlockSpec((tile, 1, D_MODEL), lambda j, grp: (j, 0, 0)),
        ),
        out_shape=jax.ShapeDtypeStruct((n_tiles * tile, 1, D_MODEL), F32),
        compiler_params=pltpu.CompilerParams(dimension_semantics=("arbitrary",), vmem_limit_bytes=VMEM_LIMIT),
        name="moe_routed",
    )(tile_group, sorted_rows, grouped(lw["w_gate"]), grouped(lw["w_up"]), grouped(lw["w_down"]))


def _combine_kernel(dest_ref, dest_next_ref, xm_ref, sorted_hbm, y_ref, buf0, buf1, sems):
    j = pl.program_id(0)
    n = pl.num_programs(0)
    tm = xm_ref.shape[0]

    def gather(idx_ref, buf, sem):
        def start(r, c):
            pltpu.make_async_copy(sorted_hbm.at[idx_ref[0, 0, r]], buf.at[r], sem).start()
            return c
        lax.fori_loop(0, tm, start, 0)

    def finish(buf, sem):
        _row_copies_wait(sorted_hbm.at[0], buf.at[0], sem, tm)
        y_ref[...] = xm_ref[...] + buf[...].reshape(tm, D_MODEL)

    @pl.when(j == 0)
    def _():
        gather(dest_ref, buf0, sems.at[0])

    for parity, (cur, nxt) in enumerate(((buf0, buf1), (buf1, buf0))):
        @pl.when(j % 2 == parity)
        def _():
            @pl.when(j + 1 < n)
            def _():
                gather(dest_next_ref, nxt, sems.at[1 - parity])
            finish(cur, sems.at[parity])


def _combine(dest, xm, moe_sorted, tm):
    t = xm.shape[0]
    n = t // tm
    idx = dest.reshape(n, 1, tm)
    return pl.pallas_call(
        _combine_kernel,
        grid=(n,),
        in_specs=[pl.BlockSpec((1, 1, tm), lambda j: (j, 0, 0), memory_space=pltpu.SMEM),
                  pl.BlockSpec((1, 1, tm), lambda j: (jnp.minimum(j + 1, n - 1), 0, 0), memory_space=pltpu.SMEM),
                  pl.BlockSpec((tm, D_MODEL), lambda j: (j, 0)),
                  pl.BlockSpec(memory_space=pl.ANY)],
        out_specs=pl.BlockSpec((tm, D_MODEL), lambda j: (j, 0)),
        out_shape=jax.ShapeDtypeStruct((t, D_MODEL), F32),
        scratch_shapes=[pltpu.VMEM((tm, 1, D_MODEL), F32), pltpu.VMEM((tm, 1, D_MODEL), F32),
                        pltpu.SemaphoreType.DMA((2,))],
        compiler_params=pltpu.CompilerParams(dimension_semantics=("arbitrary",), vmem_limit_bytes=VMEM_LIMIT),
        name="moe_combine",
    )(idx, idx, xm, moe_sorted)


def _routing_tables(group_id, tile):
    t = group_id.shape[0]
    n_tiles = t // tile + N_GROUPS
    onehot = (group_id[:, None] == jnp.arange(N_GROUPS, dtype=jnp.int32)[None, :]).astype(jnp.int32)
    csum = jnp.cumsum(onehot, axis=0)
    rank = jnp.sum(onehot * csum, axis=1) - 1
    tiles_per_group = (csum[-1] + tile - 1) // tile
    tile_end = jnp.cumsum(tiles_per_group)
    offset = (tile_end - tiles_per_group) * tile
    dest = jnp.sum(onehot * offset[None, :], axis=1) + rank
    src = jnp.zeros((n_tiles * tile,), jnp.int32).at[dest].set(jnp.arange(t, dtype=jnp.int32))
    tile_group = jnp.sum(jnp.arange(n_tiles, dtype=jnp.int32)[:, None] >= tile_end[None, :], axis=1)
    tile_group = jnp.minimum(tile_group, N_GROUPS - 1).astype(jnp.int32)
    return dest.astype(jnp.int32), src.reshape(n_tiles, 1, tile), tile_group


def _moe_kernel(h2_ref, comb_ref, xm_ref, wg_ref, wu_ref, wd_ref, y_ref):
    e = pl.program_id(1)

    @pl.when(e == 0)
    def _():
        y_ref[...] = xm_ref[...]

    t = h2_ref[...]
    gate = jnp.dot(t, wg_ref[0], preferred_element_type=F32)
    up = jnp.dot(t, wu_ref[0], preferred_element_type=F32)
    he = (gate * _sigmoid(gate) * up).astype(BF16)
    d = jnp.dot(he, wd_ref[0], preferred_element_type=F32)
    comb = comb_ref[...]
    lane = lax.broadcasted_iota(jnp.int32, comb.shape, 1)
    c = jnp.sum(jnp.where(lane == e, comb, 0.0), axis=-1, keepdims=True)
    y_ref[...] += c * d


def _moe(h2, comb, xm, lw, tm):
    t = h2.shape[0]
    tok = lambda width: pl.BlockSpec((tm, width), lambda i, e: (i, 0))
    return pl.pallas_call(
        _moe_kernel,
        grid=(t // tm, N_EXPERTS),
        in_specs=[tok(D_MODEL), tok(LANES), tok(D_MODEL),
                  pl.BlockSpec((1, D_MODEL, EXPERT_HIDDEN), lambda i, e: (e, 0, 0)),
                  pl.BlockSpec((1, D_MODEL, EXPERT_HIDDEN), lambda i, e: (e, 0, 0)),
                  pl.BlockSpec((1, EXPERT_HIDDEN, D_MODEL), lambda i, e: (e, 0, 0))],
        out_specs=tok(D_MODEL),
        out_shape=jax.ShapeDtypeStruct((t, D_MODEL), F32),
        compiler_params=pltpu.CompilerParams(dimension_semantics=("arbitrary", "arbitrary"),
                                             vmem_limit_bytes=VMEM_LIMIT),
        name="moe",
    )(h2, comb, xm, lw["w_gate"], lw["w_up"], lw["w_down"])


def _hi_lo_columns(w):
    hi = w.astype(BF16)
    return jnp.concatenate([hi, (w - hi.astype(F32)).astype(BF16)], axis=1)


def _layer_weights(l, norm1_g, w_in, conv_dw, conv_db, conv_ln_g, conv_ln_b, w_conv_out, q_norm_g, k_norm_g,
                   lambda_q1, lambda_k1, lambda_q2, lambda_k2, subln_g, w_attn_out, w_o, norm2_g, w_group, b_group,
                   w_router, b_router, w_gate, w_up, w_down):
    row = lambda a: a.reshape(1, -1).astype(F32)
    head_gain = lambda gain: jnp.tile(gain.reshape(-1), N_HEADS).reshape(1, QK_DIM).astype(F32)
    grp = jnp.arange(QK_DIM) // HEAD_DIM
    w_rt = jnp.concatenate([w_group[l], jnp.moveaxis(w_router[l], 0, 1).reshape(D_MODEL, N_EXPERTS)], axis=1)
    b_rt = jnp.concatenate([b_group[l], b_router[l].reshape(-1)])
    pad = LANES - N_GROUPS - N_EXPERTS
    return {
        "g1": row(norm1_g[l]), "w_in": w_in[l].astype(BF16), "conv_dw": conv_dw[l].astype(F32),
        "conv_db": row(conv_db[l]), "conv_ln_g": row(conv_ln_g[l]), "conv_ln_b": row(conv_ln_b[l]),
        "w_conv_out": w_conv_out[l].astype(BF16), "gq": head_gain(q_norm_g[l]), "gk": head_gain(k_norm_g[l]),
        "bd": (grp[:, None] == grp[None, :]).astype(BF16),
        "lq1": row(lambda_q1[l]), "lk1": row(lambda_k1[l]), "lq2": row(lambda_q2[l]), "lk2": row(lambda_k2[l]),
        "subln_g": row(subln_g[l]), "w_attn_out": w_attn_out[l].astype(BF16), "w_o": w_o[l].astype(BF16),
        "g2": row(norm2_g[l]),
        "w_rt": _hi_lo_columns(jnp.pad(w_rt.astype(F32), ((0, 0), (0, pad)))),
        "b_rt": jnp.pad(b_rt.astype(F32), (0, pad)).reshape(1, LANES),
        "w_gate": w_gate[l].astype(BF16), "w_up": w_up[l].astype(BF16), "w_down": w_down[l].astype(BF16),
    }


def _tail_dense(x, attn, p, g, lw, tm):
    b, s, _ = x.shape
    flat = lambda a: a.reshape(b * s, a.shape[-1])
    xm, h2, comb = _post(flat(attn), flat(p), flat(g), flat(x), lw, tm, False)
    return _moe(h2, comb, xm, lw, tm).reshape(b, s, D_MODEL)


def _tail_routed(x, attn, p, g, lw, tm, tile):
    b, s, _ = x.shape
    flat = lambda a: a.reshape(b * s, a.shape[-1])
    xm, rows = _post(flat(attn), flat(p), flat(g), flat(x), lw, tm, True)
    group_id = rows[:, 0, D_MODEL + GROUP_LANE].astype(jnp.int32)
    dest, src, tile_group = _routing_tables(group_id, tile)
    moe_sorted = _moe_routed(tile_group, _dispatch(src, rows, tile), lw, tile)
    return _combine(dest, xm, moe_sorted, tm).reshape(b, s, D_MODEL)


def kernel(x_prompt, x_sample, cache_k, cache_v, state_conv, norm1_g, w_in, conv_dw, conv_db, conv_ln_g, conv_ln_b, w_conv_out, q_norm_g, k_norm_g, lambda_q1, lambda_k1, lambda_q2, lambda_k2, subln_g, w_attn_out, w_o, norm2_g, w_group, b_group, w_router, b_router, w_gate, w_up, w_down):
    depth = w_in.shape[0]
    bp, sp, _ = x_prompt.shape
    bs, ss, _ = x_sample.shape
    past = cache_k.shape[2]
    xp, xs = x_prompt, x_sample
    kp_l, vp_l, cp_l, ks_l, vs_l, cs_l = [], [], [], [], [], []
    for l in range(depth):
        lw = _layer_weights(l, norm1_g, w_in, conv_dw, conv_db, conv_ln_g, conv_ln_b, w_conv_out, q_norm_g,
                            k_norm_g, lambda_q1, lambda_k1, lambda_q2, lambda_k2, subln_g, w_attn_out, w_o,
                            norm2_g, w_group, b_group, w_router, b_router, w_gate, w_up, w_down)
        lam_init = 0.8 - 0.6 * math.exp(-0.3 * l)

        hist_p = jnp.zeros((bp, HIST_ROWS, CONV_DIM), F32)
        q, kf, kb, vf, vb, p, g, cs = _in_proj(xp, hist_p, lw, 512)
        bound = (HEAD_DIM * SCALE) * jnp.max(jnp.abs(lw["gq"])) * jnp.max(jnp.abs(lw["gk"]))
        shift = (bound * LOG2E).reshape(1)
        attn = lax.cond(bound <= MAX_FIXED_SHIFT,
                        lambda *a: _attn_prompt(*a, lw, lam_init, 512, False),
                        lambda *a: _attn_prompt(*a, lw, lam_init, 512, True),
                        shift, q, kb, vb)
        xp = _tail_routed(xp, attn, p, g, lw, 512, MOE_TILE)
        kp_l.append(kf.reshape(bp, sp, N_HEADS, 2, HEAD_DIM))
        vp_l.append(vf.reshape(bp, sp, N_HEADS, V_DIM))
        cp_l.append(cs[:, HIST_PAD:, :])

        hist_s = jnp.pad(state_conv[l].astype(F32), ((0, 0), (HIST_PAD, 0), (0, 0)))
        q, kf, kb, vf, vb, p, g, cs = _in_proj(xs, hist_s, lw, ss)
        attn = _attn_sample(q, cache_k[l].reshape(bs, past, QK_DIM), cache_v[l].reshape(bs, past, ATTN_DIM),
                            kb, vb, lw, lam_init)
        xs = _tail_dense(xs, attn, p, g, lw, bs * ss)
        ks_l.append(kf.reshape(bs, ss, N_HEADS, 2, HEAD_DIM))
        vs_l.append(vf.reshape(bs, ss, N_HEADS, V_DIM))
        cs_l.append(cs[:, HIST_PAD:, :])
    return (xp, xs, jnp.stack(kp_l), jnp.stack(vp_l), jnp.stack(cp_l),
            jnp.stack(ks_l), jnp.stack(vs_l), jnp.stack(cs_l))
```

```python
import functools
import math

import jax
import jax.numpy as jnp
from jax import lax
from jax.experimental import pallas as pl
from jax.experimental.pallas import tpu as pltpu

D_MODEL = 1024
CHUNK = 64
CONV_DIM = 512
CONV_TAPS = 31
CONV_STATE = CONV_TAPS - 1
N_HEADS = 4
HEAD_DIM = 64
V_DIM = 2 * HEAD_DIM
ATTN_DIM = N_HEADS * V_DIM
QK_DIM = N_HEADS * 2 * HEAD_DIM
N_GROUPS = 4
EXPERTS_PER_GROUP = 4
N_EXPERTS = N_GROUPS * EXPERTS_PER_GROUP
TOP_K_INNER = 2
EXPERT_HIDDEN = 512
EPS = 1e-6
SCALE = HEAD_DIM ** -0.5
LOG2E = math.log2(math.e)
Q_SCALE = SCALE * LOG2E
MAX_FIXED_SHIFT = 40.0
COL_GLU = 2 * CONV_DIM
COL_Q = COL_GLU
COL_K = COL_Q + QK_DIM
COL_V = COL_K + QK_DIM
COL_GC = COL_V + ATTN_DIM
COL_GA = COL_GC + D_MODEL
IN_COLS = COL_GA + D_MODEL

LANES = 128
HIST_ROWS = 32
HIST_PAD = HIST_ROWS - CONV_STATE
CONV_ROW_CHUNK = 64
VMEM_LIMIT = 56 * 1024 * 1024
ROW_WIDTH = D_MODEL + LANES
MOE_TILE = 512

BF16 = jnp.bfloat16
F32 = jnp.float32


def _sigmoid(x):
    return 0.5 * jnp.tanh(0.5 * x) + 0.5


def _const_spec(shape):
    n = len(shape)
    return pl.BlockSpec(shape, lambda *_: (0,) * n)


def _in_proj_kernel(x_ref, hist_ref, g1_ref, w_in_ref, dw_ref, db_ref, lng_ref, lnb_ref, wco_ref, gq_ref, gk_ref,
                    bd_ref, q_ref, kf_ref, kb_ref, vf_ref, vb_ref, p_ref, g_ref, cs_ref, cbuf, ybuf):
    t = pl.program_id(1)
    tm = x_ref.shape[1]
    x = x_ref[0]
    ms = jnp.mean(x * x, axis=-1, keepdims=True)
    h = (x * lax.rsqrt(ms + EPS) * g1_ref[...]).astype(BF16)

    def proj(lo, hi):
        return jnp.dot(h, w_in_ref[:, lo:hi], preferred_element_type=F32)

    u = proj(0, CONV_DIM) * _sigmoid(proj(CONV_DIM, COL_GLU))

    @pl.when(t == 0)
    def _():
        cbuf[0:HIST_ROWS, :] = hist_ref[0]

    @pl.when(t > 0)
    def _():
        cbuf[0:HIST_ROWS, :] = cbuf[tm:tm + HIST_ROWS, :]

    cbuf[HIST_ROWS:HIST_ROWS + tm, :] = u
    cs_ref[0] = cbuf[tm:tm + HIST_ROWS, :]

    rc = min(CONV_ROW_CHUNK, tm)
    for c in range(tm // rc):
        acc = None
        for res in range(8):
            rows = rc if res == 0 else rc + 8
            z = None
            for off in range(res, CONV_TAPS + HIST_PAD, 8):
                j = off - HIST_PAD
                if j < 0:
                    continue
                lo = c * rc + off - res
                term = dw_ref[j:j + 1, :] * cbuf[lo:lo + rows, :]
                z = term if z is None else z + term
            z = z[res:res + rc, :]
            acc = z if acc is None else acc + z
        y = acc + db_ref[...]
        mu = jnp.mean(y, axis=-1, keepdims=True)
        yc = y - mu
        var = jnp.mean(yc * yc, axis=-1, keepdims=True)
        yn = yc * lax.rsqrt(var + EPS) * lng_ref[...] + lnb_ref[...]
        ybuf[c * rc:(c + 1) * rc, :] = (yn * _sigmoid(yn)).astype(BF16)
    conv_out = jnp.dot(ybuf[...], wco_ref[...], preferred_element_type=F32)
    p_ref[0] = (_sigmoid(proj(COL_GC, COL_GA)) * conv_out).astype(BF16)
    g_ref[0] = _sigmoid(proj(COL_GA, IN_COLS)).astype(BF16)

    def head_norm(z, gain_ref):
        ss = jnp.dot((z * z).astype(BF16), bd_ref[...], preferred_element_type=F32) * (1.0 / HEAD_DIM)
        return z * lax.rsqrt(ss + EPS) * gain_ref[...]

    q_ref[0] = (head_norm(proj(COL_Q, COL_K), gq_ref) * Q_SCALE).astype(BF16)
    kn = head_norm(proj(COL_K, COL_V), gk_ref)
    kf_ref[0] = kn
    kb_ref[0] = kn.astype(BF16)
    v = proj(COL_V, COL_GC)
    vf_ref[0] = v
    vb_ref[0] = v.astype(BF16)


def _in_proj(x, hist, lw, tm):
    b, s, _ = x.shape
    grid = (b, s // tm)
    tok = lambda width: pl.BlockSpec((1, tm, width), lambda i, j: (i, j, 0))
    out_shape = (
        jax.ShapeDtypeStruct((b, s, QK_DIM), BF16),
        jax.ShapeDtypeStruct((b, s, QK_DIM), F32),
        jax.ShapeDtypeStruct((b, s, QK_DIM), BF16),
        jax.ShapeDtypeStruct((b, s, ATTN_DIM), F32),
        jax.ShapeDtypeStruct((b, s, ATTN_DIM), BF16),
        jax.ShapeDtypeStruct((b, s, D_MODEL), BF16),
        jax.ShapeDtypeStruct((b, s, D_MODEL), BF16),
        jax.ShapeDtypeStruct((b, HIST_ROWS, CONV_DIM), F32),
    )
    return pl.pallas_call(
        _in_proj_kernel,
        grid=grid,
        in_specs=[
            tok(D_MODEL),
            pl.BlockSpec((1, HIST_ROWS, CONV_DIM), lambda i, j: (i, 0, 0)),
            _const_spec((1, D_MODEL)),
            _const_spec((D_MODEL, IN_COLS)),
            _const_spec((CONV_TAPS, CONV_DIM)),
            _const_spec((1, CONV_DIM)),
            _const_spec((1, CONV_DIM)),
            _const_spec((1, CONV_DIM)),
            _const_spec((CONV_DIM, D_MODEL)),
            _const_spec((1, QK_DIM)),
            _const_spec((1, QK_DIM)),
            _const_spec((QK_DIM, QK_DIM)),
        ],
        out_specs=(tok(QK_DIM), tok(QK_DIM), tok(QK_DIM), tok(ATTN_DIM), tok(ATTN_DIM), tok(D_MODEL), tok(D_MODEL),
                   pl.BlockSpec((1, HIST_ROWS, CONV_DIM), lambda i, j: (i, 0, 0))),
        out_shape=out_shape,
        scratch_shapes=[pltpu.VMEM((HIST_ROWS + tm, CONV_DIM), F32), pltpu.VMEM((tm, CONV_DIM), BF16)],
        compiler_params=pltpu.CompilerParams(dimension_semantics=("arbitrary", "arbitrary"),
                                             vmem_limit_bytes=VMEM_LIMIT),
        name="in_proj",
    )(x, hist, lw["g1"], lw["w_in"], lw["conv_dw"], lw["conv_db"], lw["conv_ln_g"], lw["conv_ln_b"],
      lw["w_conv_out"], lw["gq"], lw["gk"], lw["bd"])


def _lambda(lq1_ref, lk1_ref, lq2_ref, lk2_ref, lam_init):
    a = jnp.sum(lq1_ref[...] * lk1_ref[...], axis=-1, keepdims=True)
    b = jnp.sum(lq2_ref[...] * lk2_ref[...], axis=-1, keepdims=True)
    return jnp.exp(a) - jnp.exp(b) + lam_init


def _split_maps(q):
    lane = lax.broadcasted_iota(jnp.int32, q.shape, 1)
    zero = jnp.zeros_like(q)
    return jnp.where(lane < HEAD_DIM, q, zero), jnp.where(lane >= HEAD_DIM, q, zero)


def _scores(qm, k):
    return lax.dot_general(qm, k, (((1,), (1,)), ((), ())), preferred_element_type=F32)


def _sub_ln(o1, o2, lam, sg_ref, lam_init):
    o = o1 - lam * o2
    ms = jnp.mean(o * o, axis=-1, keepdims=True)
    return o * lax.rsqrt(ms + EPS) * sg_ref[...] * (1.0 - lam_init)


def _chunk_mask(s):
    row = lax.broadcasted_iota(jnp.int32, s.shape, 0) // CHUNK
    col = lax.broadcasted_iota(jnp.int32, s.shape, 1) // CHUNK
    return jnp.where(col <= row, s, -jnp.inf)


def _attn_prompt_kernel(shift_ref, q_ref, k_ref, v_ref, lq1_ref, lk1_ref, lq2_ref, lk2_ref, sg_ref, o_ref,
                        qs1, qs2, m1, l1, a1, m2, l2, a2, *, lam_init, online):
    qi = pl.program_id(2)
    tq = q_ref.shape[1]
    q1, q2 = _split_maps(q_ref[0])
    qs1[...] = q1
    qs2[...] = q2
    for m, l, a in ((m1, l1, a1), (m2, l2, a2)):
        m[...] = jnp.full(m.shape, -jnp.inf, F32)
        l[...] = jnp.zeros(l.shape, F32)
        a[...] = jnp.zeros(a.shape, F32)

    def block(kstart, masked):
        k = k_ref[0, pl.ds(kstart, tq), :]
        v = v_ref[0, pl.ds(kstart, tq), :]
        for qs, m, l, a in ((qs1, m1, l1, a1), (qs2, m2, l2, a2)):
            s = _scores(qs[...], k)
            if masked:
                s = _chunk_mask(s)
            if online:
                m_prev = m[...]
                m_new = jnp.maximum(m_prev, jnp.max(s, axis=-1, keepdims=True))
                alpha = jnp.exp2(m_prev - m_new)
                p = jnp.exp2(s - m_new)
                l[...] = alpha * l[...] + jnp.sum(p, axis=-1, keepdims=True)
                a[...] = alpha * a[...] + jnp.dot(p.astype(BF16), v, preferred_element_type=F32)
                m[...] = m_new
            else:
                p = jnp.exp2(s - shift_ref[0])
                part = p[:, 0:LANES]
                for c in range(1, tq // LANES):
                    part = part + p[:, c * LANES:(c + 1) * LANES]
                l[...] += part
                a[...] += jnp.dot(p.astype(BF16), v, preferred_element_type=F32)

    def body(i, carry):
        block(pl.multiple_of(i * tq, tq), False)
        return carry

    lax.fori_loop(0, qi, body, 0)
    block(pl.multiple_of(qi * tq, tq), True)

    lam = _lambda(lq1_ref, lk1_ref, lq2_ref, lk2_ref, lam_init)
    d1 = jnp.sum(l1[...], axis=-1, keepdims=True)
    d2 = jnp.sum(l2[...], axis=-1, keepdims=True)
    o_ref[0] = _sub_ln(a1[...] / d1, a2[...] / d2, lam, sg_ref, lam_init).astype(BF16)


def _attn_prompt(shift, q, k, v, lw, lam_init, tq, online):
    b, s, _ = q.shape
    grid = (b, N_HEADS, s // tq)
    qspec = pl.BlockSpec((1, tq, LANES), lambda i, h, j: (i, j, h))
    kvspec = pl.BlockSpec((1, s, LANES), lambda i, h, j: (i, 0, h))
    vec = _const_spec((1, HEAD_DIM))
    lw_width = 1 if online else LANES
    return pl.pallas_call(
        functools.partial(_attn_prompt_kernel, lam_init=lam_init, online=online),
        grid=grid,
        in_specs=[pl.BlockSpec(memory_space=pltpu.SMEM), qspec, kvspec, kvspec, vec, vec, vec, vec,
                  _const_spec((1, V_DIM))],
        out_specs=qspec,
        out_shape=jax.ShapeDtypeStruct((b, s, ATTN_DIM), BF16),
        scratch_shapes=[pltpu.VMEM((tq, LANES), BF16), pltpu.VMEM((tq, LANES), BF16),
                        pltpu.VMEM((tq, 1), F32), pltpu.VMEM((tq, lw_width), F32), pltpu.VMEM((tq, V_DIM), F32),
                        pltpu.VMEM((tq, 1), F32), pltpu.VMEM((tq, lw_width), F32), pltpu.VMEM((tq, V_DIM), F32)],
        compiler_params=pltpu.CompilerParams(dimension_semantics=("arbitrary", "arbitrary", "arbitrary"),
                                             vmem_limit_bytes=VMEM_LIMIT),
        name="attn_prompt_online" if online else "attn_prompt",
    )(shift, q, k, v, lw["lq1"], lw["lk1"], lw["lq2"], lw["lk2"], lw["subln_g"])


def _attn_sample_kernel(q_ref, kc_ref, vc_ref, kn_ref, vn_ref, lq1_ref, lk1_ref, lq2_ref, lk2_ref, sg_ref, o_ref,
                        *, lam_init):
    q1, q2 = _split_maps(q_ref[0])
    kc = kc_ref[0].astype(BF16)
    vc = vc_ref[0].astype(BF16)
    kn = kn_ref[0]
    vn = vn_ref[0]
    outs = []
    for qm in (q1, q2):
        sc = _scores(qm, kc)
        sn = _scores(qm, kn)
        m = jnp.maximum(jnp.max(sc, axis=-1, keepdims=True), jnp.max(sn, axis=-1, keepdims=True))
        pc = jnp.exp2(sc - m)
        pn = jnp.exp2(sn - m)
        l = jnp.sum(pc, axis=-1, keepdims=True) + jnp.sum(pn, axis=-1, keepdims=True)
        acc = (jnp.dot(pc.astype(BF16), vc, preferred_element_type=F32)
               + jnp.dot(pn.astype(BF16), vn, preferred_element_type=F32))
        outs.append(acc / l)
    lam = _lambda(lq1_ref, lk1_ref, lq2_ref, lk2_ref, lam_init)
    o_ref[0] = _sub_ln(outs[0], outs[1], lam, sg_ref, lam_init).astype(BF16)


def _attn_sample(q, k_cache, v_cache, k_new, v_new, lw, lam_init):
    b, t, _ = q.shape
    past = k_cache.shape[1]
    new = pl.BlockSpec((1, t, LANES), lambda i, h: (i, 0, h))
    cache = pl.BlockSpec((1, past, LANES), lambda i, h: (i, 0, h))
    vec = _const_spec((1, HEAD_DIM))
    return pl.pallas_call(
        functools.partial(_attn_sample_kernel, lam_init=lam_init),
        grid=(b, N_HEADS),
        in_specs=[new, cache, cache, new, new, vec, vec, vec, vec, _const_spec((1, V_DIM))],
        out_specs=new,
        out_shape=jax.ShapeDtypeStruct((b, t, ATTN_DIM), BF16),
        compiler_params=pltpu.CompilerParams(dimension_semantics=("arbitrary", "arbitrary"),
                                             vmem_limit_bytes=VMEM_LIMIT),
        name="attn_sample",
    )(q, k_cache, v_cache, k_new, v_new, lw["lq1"], lw["lk1"], lw["lq2"], lw["lk2"], lw["subln_g"])


def _post_kernel(attn_ref, p_ref, g_ref, x_ref, wao_ref, wo_ref, g2_ref, wrt_ref, brt_ref, xm_ref, *out_refs, routed):
    tm = x_ref.shape[0]
    ao = jnp.dot(attn_ref[...], wao_ref[...], preferred_element_type=F32)
    merged = p_ref[...].astype(F32) + g_ref[...].astype(F32) * ao
    xm = x_ref[...] + jnp.dot(merged.astype(BF16), wo_ref[...], preferred_element_type=F32)
    xm_ref[...] = xm
    ms = jnp.mean(xm * xm, axis=-1, keepdims=True)
    h2 = xm * lax.rsqrt(ms + EPS) * g2_ref[...]

    h_hi = h2.astype(BF16)
    h_lo = (h2 - h_hi.astype(F32)).astype(BF16)
    r = (jnp.dot(h_hi, wrt_ref[...], preferred_element_type=F32)
         + jnp.dot(h_lo, wrt_ref[...], preferred_element_type=F32))
    lg = r[:, :LANES] + r[:, LANES:] + brt_ref[...]
    lane = lax.broadcasted_iota(jnp.int32, lg.shape, 1)
    neg = jnp.full_like(lg, -jnp.inf)
    big = jnp.full_like(lane, LANES)

    def first_max(z):
        zmax = jnp.max(z, axis=-1, keepdims=True)
        return zmax, jnp.min(jnp.where(z == zmax, lane, big), axis=-1, keepdims=True)

    is_group = lane < N_GROUPS
    gmax, gi = first_max(jnp.where(is_group, lg, neg))
    g_w = 1.0 / jnp.sum(jnp.where(is_group, jnp.exp(lg - gmax), 0.0), axis=-1, keepdims=True)
    lo = N_GROUPS + EXPERTS_PER_GROUP * gi
    el = jnp.where((lane >= lo) & (lane < lo + EXPERTS_PER_GROUP), lg, neg)
    v1, i1 = first_max(el)
    v2, i2 = first_max(jnp.where(lane == i1, neg, el))
    e2 = jnp.exp(v2 - v1)
    w1 = g_w / (1.0 + e2)
    w2 = g_w * e2 / (1.0 + e2)
    comb = jnp.where(lane == i1 - N_GROUPS, w1, 0.0) + jnp.where(lane == i2 - N_GROUPS, w2, 0.0)
    if routed:
        rows_ref, gid_ref = out_refs
        rows_ref[:, :, 0:D_MODEL] = h2.reshape(tm, 1, D_MODEL)
        rows_ref[:, :, D_MODEL:ROW_WIDTH] = comb.reshape(tm, 1, LANES)
        gid = jnp.where(lane == 0, gi.astype(F32), 0.0).astype(BF16)
        pick = (lax.broadcasted_iota(jnp.int32, (8, LANES), 1) == 0).astype(BF16)
        gid_ref[0] = lax.dot_general(pick, gid, (((1,), (1,)), ((), ())), preferred_element_type=F32)
    else:
        h2_ref, comb_ref = out_refs
        h2_ref[...] = h_hi
        comb_ref[...] = comb


def _post(attn, p, g, x, lw, tm, routed):
    t = x.shape[0]
    tok = lambda width: pl.BlockSpec((tm, width), lambda i: (i, 0))
    if routed:
        out_specs = (tok(D_MODEL), pl.BlockSpec((tm, 1, ROW_WIDTH), lambda i: (i, 0, 0)),
                     pl.BlockSpec((1, 8, tm), lambda i: (i, 0, 0)))
        out_shape = (jax.ShapeDtypeStruct((t, D_MODEL), F32), jax.ShapeDtypeStruct((t, 1, ROW_WIDTH), F32),
                     jax.ShapeDtypeStruct((t // tm, 8, tm), F32))
    else:
        out_specs = (tok(D_MODEL), tok(D_MODEL), tok(LANES))
        out_shape = (jax.ShapeDtypeStruct((t, D_MODEL), F32), jax.ShapeDtypeStruct((t, D_MODEL), BF16),
                     jax.ShapeDtypeStruct((t, LANES), F32))
    return pl.pallas_call(
        functools.partial(_post_kernel, routed=routed),
        grid=(t // tm,),
        in_specs=[tok(ATTN_DIM), tok(D_MODEL), tok(D_MODEL), tok(D_MODEL),
                  _const_spec((ATTN_DIM, D_MODEL)), _const_spec((D_MODEL, D_MODEL)), _const_spec((1, D_MODEL)),
                  _const_spec((D_MODEL, 2 * LANES)), _const_spec((1, LANES))],
        out_specs=out_specs,
        out_shape=out_shape,
        compiler_params=pltpu.CompilerParams(dimension_semantics=("arbitrary",), vmem_limit_bytes=VMEM_LIMIT),
        name="post_routed" if routed else "post",
    )(attn, p, g, x, lw["w_attn_out"], lw["w_o"], lw["g2"], lw["w_rt"], lw["b_rt"])


def _row_gather(idx_ref, rows_hbm, buf, sem, base, n):
    def start():
        def one(r, c):
            pltpu.make_async_copy(rows_hbm.at[idx_ref[0, 0, r]], buf.at[base + r], sem).start()
            return c
        lax.fori_loop(0, n, one, 0, unroll=8)

    def wait():
        pltpu.make_async_copy(rows_hbm.at[pl.ds(0, n)], buf.at[pl.ds(base, n)], sem).wait()

    return start, wait


def _gathered_rows(idx_ref, idx_next_ref, rows_hbm, buf, sems, n):
    j = pl.program_id(0)
    slot = j % 2

    @pl.when(j == 0)
    def _():
        _row_gather(idx_ref, rows_hbm, buf, sems.at[0], 0, n)[0]()

    @pl.when(j + 1 < pl.num_programs(0))
    def _():
        _row_gather(idx_next_ref, rows_hbm, buf, sems.at[1 - slot], (1 - slot) * n, n)[0]()

    _row_gather(idx_ref, rows_hbm, buf, sems.at[slot], slot * n, n)[1]()
    return buf[pl.ds(slot * n, n)]


def _idx_specs(n_steps, n):
    cur = lambda j, *_: (j, 0, 0)
    nxt = lambda j, *_: (jnp.minimum(j + 1, n_steps - 1), 0, 0)
    return [pl.BlockSpec((1, 1, n), cur, memory_space=pltpu.SMEM),
            pl.BlockSpec((1, 1, n), nxt, memory_space=pltpu.SMEM)]


def _moe_routed_kernel(group_ref, src_ref, src_next_ref, rows_hbm, wg_ref, wu_ref, wd_ref, out_ref, buf, sems):
    tile = out_ref.shape[0]
    g = group_ref[pl.program_id(0)]
    rows = _gathered_rows(src_ref, src_next_ref, rows_hbm, buf, sems, tile).reshape(tile, ROW_WIDTH)
    t = rows[:, 0:D_MODEL].astype(BF16)
    comb = rows[:, D_MODEL:ROW_WIDTH]
    lane = lax.broadcasted_iota(jnp.int32, comb.shape, 1)
    out = jnp.zeros((tile, D_MODEL), F32)
    for e in range(EXPERTS_PER_GROUP):
        gate = jnp.dot(t, wg_ref[0, e], preferred_element_type=F32)
        up = jnp.dot(t, wu_ref[0, e], preferred_element_type=F32)
        he = (gate * _sigmoid(gate) * up).astype(BF16)
        d = jnp.dot(he, wd_ref[0, e], preferred_element_type=F32)
        c = jnp.sum(jnp.where(lane == g * EXPERTS_PER_GROUP + e, comb, 0.0), axis=-1, keepdims=True)
        out = out + c * d
    out_ref[...] = out.reshape(tile, 1, D_MODEL)


def _moe_routed(tile_group, src, rows, lw, tile):
    n_tiles = tile_group.shape[0]
    grouped = lambda w: w.reshape(N_GROUPS, EXPERTS_PER_GROUP, *w.shape[1:])
    wspec = lambda a, b: pl.BlockSpec((1, EXPERTS_PER_GROUP, a, b), lambda j, grp: (grp[j], 0, 0, 0))
    return pl.pallas_call(
        _moe_routed_kernel,
        grid_spec=pltpu.PrefetchScalarGridSpec(
            num_scalar_prefetch=1,
            grid=(n_tiles,),
            in_specs=_idx_specs(n_tiles, tile) + [
                pl.BlockSpec(memory_space=pl.ANY),
                wspec(D_MODEL, EXPERT_HIDDEN), wspec(D_MODEL, EXPERT_HIDDEN), wspec(EXPERT_HIDDEN, D_MODEL)],
            out_specs=pl.BlockSpec((tile, 1, D_MODEL), lambda j, grp: (j, 0, 0)),
            scratch_shapes=[pltpu.VMEM((2 * tile, 1, ROW_WIDTH), F32), pltpu.SemaphoreType.DMA((2,))],
        ),
        out_shape=jax.ShapeDtypeStruct((n_tiles * tile, 1, D_MODEL), F32),
        compiler_params=pltpu.CompilerParams(dimension_semantics=("arbitrary",), vmem_limit_bytes=VMEM_LIMIT),
        name="moe_routed",
    )(tile_group, src, src, rows, grouped(lw["w_gate"]), grouped(lw["w_up"]), grouped(lw["w_down"]))


def _combine_kernel(dest_ref, dest_next_ref, xm_ref, sorted_hbm, y_ref, buf, sems):
    tm = xm_ref.shape[0]
    moe = _gathered_rows(dest_ref, dest_next_ref, sorted_hbm, buf, sems, tm)
    y_ref[...] = xm_ref[...] + moe.reshape(tm, D_MODEL)


def _combine(dest, xm, moe_sorted, tm):
    t = xm.shape[0]
    n = t // tm
    idx = dest.reshape(n, 1, tm)
    return pl.pallas_call(
        _combine_kernel,
        grid=(n,),
        in_specs=_idx_specs(n, tm) + [pl.BlockSpec((tm, D_MODEL), lambda j: (j, 0)),
                                      pl.BlockSpec(memory_space=pl.ANY)],
        out_specs=pl.BlockSpec((tm, D_MODEL), lambda j: (j, 0)),
        out_shape=jax.ShapeDtypeStruct((t, D_MODEL), F32),
        scratch_shapes=[pltpu.VMEM((2 * tm, 1, D_MODEL), F32), pltpu.SemaphoreType.DMA((2,))],
        compiler_params=pltpu.CompilerParams(dimension_semantics=("arbitrary",), vmem_limit_bytes=VMEM_LIMIT),
        name="moe_combine",
    )(idx, idx, xm, moe_sorted)


def _routing_tables(group_id, tile):
    t = group_id.shape[0]
    n_tiles = t // tile + N_GROUPS
    onehot = (group_id[:, None] == jnp.arange(N_GROUPS, dtype=jnp.int32)[None, :]).astype(jnp.int32)
    csum = jnp.cumsum(onehot, axis=0)
    rank = jnp.sum(onehot * csum, axis=1) - 1
    tiles_per_group = (csum[-1] + tile - 1) // tile
    tile_end = jnp.cumsum(tiles_per_group)
    offset = (tile_end - tiles_per_group) * tile
    dest = jnp.sum(onehot * offset[None, :], axis=1) + rank
    src = jnp.zeros((n_tiles * tile,), jnp.int32).at[dest].set(jnp.arange(t, dtype=jnp.int32))
    tile_group = jnp.sum(jnp.arange(n_tiles, dtype=jnp.int32)[:, None] >= tile_end[None, :], axis=1)
    tile_group = jnp.minimum(tile_group, N_GROUPS - 1).astype(jnp.int32)
    return dest.astype(jnp.int32), src.reshape(n_tiles, 1, tile), tile_group


def _moe_kernel(h2_ref, comb_ref, xm_ref, wg_ref, wu_ref, wd_ref, y_ref):
    e = pl.program_id(1)

    @pl.when(e == 0)
    def _():
        y_ref[...] = xm_ref[...]

    t = h2_ref[...]
    gate = jnp.dot(t, wg_ref[0], preferred_element_type=F32)
    up = jnp.dot(t, wu_ref[0], preferred_element_type=F32)
    he = (gate * _sigmoid(gate) * up).astype(BF16)
    d = jnp.dot(he, wd_ref[0], preferred_element_type=F32)
    comb = comb_ref[...]
    lane = lax.broadcasted_iota(jnp.int32, comb.shape, 1)
    c = jnp.sum(jnp.where(lane == e, comb, 0.0), axis=-1, keepdims=True)
    y_ref[...] += c * d


def _moe(h2, comb, xm, lw, tm):
    t = h2.shape[0]
    tok = lambda width: pl.BlockSpec((tm, width), lambda i, e: (i, 0))
    return pl.pallas_call(
        _moe_kernel,
        grid=(t // tm, N_EXPERTS),
        in_specs=[tok(D_MODEL), tok(LANES), tok(D_MODEL),
                  pl.BlockSpec((1, D_MODEL, EXPERT_HIDDEN), lambda i, e: (e, 0, 0)),
                  pl.BlockSpec((1, D_MODEL, EXPERT_HIDDEN), lambda i, e: (e, 0, 0)),
                  pl.BlockSpec((1, EXPERT_HIDDEN, D_MODEL), lambda i, e: (e, 0, 0))],
        out_specs=tok(D_MODEL),
        out_shape=jax.ShapeDtypeStruct((t, D_MODEL), F32),
        compiler_params=pltpu.CompilerParams(dimension_semantics=("arbitrary", "arbitrary"),
                                             vmem_limit_bytes=VMEM_LIMIT),
        name="moe",
    )(h2, comb, xm, lw["w_gate"], lw["w_up"], lw["w_down"])


def _hi_lo_columns(w):
    hi = w.astype(BF16)
    return jnp.concatenate([hi, (w - hi.astype(F32)).astype(BF16)], axis=1)


def _layer_weights(l, norm1_g, w_in, conv_dw, conv_db, conv_ln_g, conv_ln_b, w_conv_out, q_norm_g, k_norm_g,
                   lambda_q1, lambda_k1, lambda_q2, lambda_k2, subln_g, w_attn_out, w_o, norm2_g, w_group, b_group,
                   w_router, b_router, w_gate, w_up, w_down):
    row = lambda a: a.reshape(1, -1).astype(F32)
    head_gain = lambda gain: jnp.tile(gain.reshape(-1), N_HEADS).reshape(1, QK_DIM).astype(F32)
    grp = jnp.arange(QK_DIM) // HEAD_DIM
    w_rt = jnp.concatenate([w_group[l], jnp.moveaxis(w_router[l], 0, 1).reshape(D_MODEL, N_EXPERTS)], axis=1)
    b_rt = jnp.concatenate([b_group[l], b_router[l].reshape(-1)])
    pad = LANES - N_GROUPS - N_EXPERTS
    return {
        "g1": row(norm1_g[l]), "w_in": w_in[l].astype(BF16), "conv_dw": conv_dw[l].astype(F32),
        "conv_db": row(conv_db[l]), "conv_ln_g": row(conv_ln_g[l]), "conv_ln_b": row(conv_ln_b[l]),
        "w_conv_out": w_conv_out[l].astype(BF16), "gq": head_gain(q_norm_g[l]), "gk": head_gain(k_norm_g[l]),
        "bd": (grp[:, None] == grp[None, :]).astype(BF16),
        "lq1": row(lambda_q1[l]), "lk1": row(lambda_k1[l]), "lq2": row(lambda_q2[l]), "lk2": row(lambda_k2[l]),
        "subln_g": row(subln_g[l]), "w_attn_out": w_attn_out[l].astype(BF16), "w_o": w_o[l].astype(BF16),
        "g2": row(norm2_g[l]),
        "w_rt": _hi_lo_columns(jnp.pad(w_rt.astype(F32), ((0, 0), (0, pad)))),
        "b_rt": jnp.pad(b_rt.astype(F32), (0, pad)).reshape(1, LANES),
        "w_gate": w_gate[l].astype(BF16), "w_up": w_up[l].astype(BF16), "w_down": w_down[l].astype(BF16),
    }


def _tail_dense(x, attn, p, g, lw, tm):
    b, s, _ = x.shape
    flat = lambda a: a.reshape(b * s, a.shape[-1])
    xm, h2, comb = _post(flat(attn), flat(p), flat(g), flat(x), lw, tm, False)
    return _moe(h2, comb, xm, lw, tm).reshape(b, s, D_MODEL)


def _tail_routed(x, attn, p, g, lw, tm, tile):
    b, s, _ = x.shape
    flat = lambda a: a.reshape(b * s, a.shape[-1])
    xm, rows, gid = _post(flat(attn), flat(p), flat(g), flat(x), lw, tm, True)
    dest, src, tile_group = _routing_tables(gid[:, 0, :].reshape(-1).astype(jnp.int32), tile)
    moe_sorted = _moe_routed(tile_group, src, rows, lw, tile)
    return _combine(dest, xm, moe_sorted, tm).reshape(b, s, D_MODEL)


def kernel(x_prompt, x_sample, cache_k, cache_v, state_conv, norm1_g, w_in, conv_dw, conv_db, conv_ln_g, conv_ln_b, w_conv_out, q_norm_g, k_norm_g, lambda_q1, lambda_k1, lambda_q2, lambda_k2, subln_g, w_attn_out, w_o, norm2_g, w_group, b_group, w_router, b_router, w_gate, w_up, w_down):
    depth = w_in.shape[0]
    bp, sp, _ = x_prompt.shape
    bs, ss, _ = x_sample.shape
    past = cache_k.shape[2]
    xp, xs = x_prompt, x_sample
    kp_l, vp_l, cp_l, ks_l, vs_l, cs_l = [], [], [], [], [], []
    for l in range(depth):
        lw = _layer_weights(l, norm1_g, w_in, conv_dw, conv_db, conv_ln_g, conv_ln_b, w_conv_out, q_norm_g,
                            k_norm_g, lambda_q1, lambda_k1, lambda_q2, lambda_k2, subln_g, w_attn_out, w_o,
                            norm2_g, w_group, b_group, w_router, b_router, w_gate, w_up, w_down)
        lam_init = 0.8 - 0.6 * math.exp(-0.3 * l)

        hist_p = jnp.zeros((bp, HIST_ROWS, CONV_DIM), F32)
        q, kf, kb, vf, vb, p, g, cs = _in_proj(xp, hist_p, lw, 512)
        bound = (HEAD_DIM * SCALE) * jnp.max(jnp.abs(lw["gq"])) * jnp.max(jnp.abs(lw["gk"]))
        shift = (bound * LOG2E).reshape(1)
        attn = lax.cond(bound <= MAX_FIXED_SHIFT,
                        lambda *a: _attn_prompt(*a, lw, lam_init, 512, False),
                        lambda *a: _attn_prompt(*a, lw, lam_init, 512, True),
                        shift, q, kb, vb)
        xp = _tail_routed(xp, attn, p, g, lw, 512, MOE_TILE)
        kp_l.append(kf.reshape(bp, sp, N_HEADS, 2, HEAD_DIM))
        vp_l.append(vf.reshape(bp, sp, N_HEADS, V_DIM))
        cp_l.append(cs[:, HIST_PAD:, :])

        hist_s = jnp.pad(state_conv[l].astype(F32), ((0, 0), (HIST_PAD, 0), (0, 0)))
        q, kf, kb, vf, vb, p, g, cs = _in_proj(xs, hist_s, lw, ss)
        attn = _attn_sample(q, cache_k[l].reshape(bs, past, QK_DIM), cache_v[l].reshape(bs, past, ATTN_DIM),
                            kb, vb, lw, lam_init)
        xs = _tail_dense(xs, attn, p, g, lw, bs * ss)
        ks_l.append(kf.reshape(bs, ss, N_HEADS, 2, HEAD_DIM))
        vs_l.append(vf.reshape(bs, ss, N_HEADS, V_DIM))
        cs_l.append(cs[:, HIST_PAD:, :])
    return (xp, xs, jnp.stack(kp_l), jnp.stack(vp_l), jnp.stack(cp_l),
            jnp.stack(ks_l), jnp.stack(vs_l), jnp.stack(cs_l))
```

```python
import functools
import math

import jax
import jax.numpy as jnp
from jax import lax
from jax.experimental import pallas as pl
from jax.experimental.pallas import tpu as pltpu

D_MODEL = 1024
CHUNK = 64
CONV_DIM = 512
CONV_TAPS = 31
CONV_STATE = CONV_TAPS - 1
N_HEADS = 4
HEAD_DIM = 64
V_DIM = 2 * HEAD_DIM
ATTN_DIM = N_HEADS * V_DIM
QK_DIM = N_HEADS * 2 * HEAD_DIM
N_GROUPS = 4
EXPERTS_PER_GROUP = 4
N_EXPERTS = N_GROUPS * EXPERTS_PER_GROUP
TOP_K_INNER = 2
EXPERT_HIDDEN = 512
EPS = 1e-6
SCALE = HEAD_DIM ** -0.5
LOG2E = math.log2(math.e)
Q_SCALE = SCALE * LOG2E
MAX_FIXED_SHIFT = 40.0
COL_GLU = 2 * CONV_DIM
COL_Q = COL_GLU
COL_K = COL_Q + QK_DIM
COL_V = COL_K + QK_DIM
COL_GC = COL_V + ATTN_DIM
COL_GA = COL_GC + D_MODEL
IN_COLS = COL_GA + D_MODEL

LANES = 128
HIST_ROWS = 32
HIST_PAD = HIST_ROWS - CONV_STATE
CONV_ROW_CHUNK = 64
VMEM_LIMIT = 56 * 1024 * 1024
OUT_SPAN = D_MODEL // LANES
ROW_TILES = OUT_SPAN + 1
ROW_SPAN = 16
MOE_TILE = 512

BF16 = jnp.bfloat16
F32 = jnp.float32


def _sigmoid(x):
    return 0.5 * jnp.tanh(0.5 * x) + 0.5


def _const_spec(shape):
    n = len(shape)
    return pl.BlockSpec(shape, lambda *_: (0,) * n)


def _in_proj_kernel(x_ref, hist_ref, g1_ref, w_in_ref, dw_ref, db_ref, lng_ref, lnb_ref, wco_ref, gq_ref, gk_ref,
                    bd_ref, q_ref, kf_ref, kb_ref, vf_ref, vb_ref, p_ref, g_ref, cs_ref, cbuf, ybuf):
    t = pl.program_id(1)
    tm = x_ref.shape[1]
    x = x_ref[0]
    ms = jnp.mean(x * x, axis=-1, keepdims=True)
    h = (x * lax.rsqrt(ms + EPS) * g1_ref[...]).astype(BF16)

    def proj(lo, hi):
        return jnp.dot(h, w_in_ref[:, lo:hi], preferred_element_type=F32)

    u = proj(0, CONV_DIM) * _sigmoid(proj(CONV_DIM, COL_GLU))

    @pl.when(t == 0)
    def _():
        cbuf[0:HIST_ROWS, :] = hist_ref[0]

    @pl.when(t > 0)
    def _():
        cbuf[0:HIST_ROWS, :] = cbuf[tm:tm + HIST_ROWS, :]

    cbuf[HIST_ROWS:HIST_ROWS + tm, :] = u
    cs_ref[0] = cbuf[tm:tm + HIST_ROWS, :]

    rc = min(CONV_ROW_CHUNK, tm)
    for c in range(tm // rc):
        acc = None
        for res in range(8):
            rows = rc if res == 0 else rc + 8
            z = None
            for off in range(res, CONV_TAPS + HIST_PAD, 8):
                j = off - HIST_PAD
                if j < 0:
                    continue
                lo = c * rc + off - res
                term = dw_ref[j:j + 1, :] * cbuf[lo:lo + rows, :]
                z = term if z is None else z + term
            z = z[res:res + rc, :]
            acc = z if acc is None else acc + z
        y = acc + db_ref[...]
        mu = jnp.mean(y, axis=-1, keepdims=True)
        yc = y - mu
        var = jnp.mean(yc * yc, axis=-1, keepdims=True)
        yn = yc * lax.rsqrt(var + EPS) * lng_ref[...] + lnb_ref[...]
        ybuf[c * rc:(c + 1) * rc, :] = (yn * _sigmoid(yn)).astype(BF16)
    conv_out = jnp.dot(ybuf[...], wco_ref[...], preferred_element_type=F32)
    p_ref[0] = (_sigmoid(proj(COL_GC, COL_GA)) * conv_out).astype(BF16)
    g_ref[0] = _sigmoid(proj(COL_GA, IN_COLS)).astype(BF16)

    def head_norm(z, gain_ref):
        ss = jnp.dot((z * z).astype(BF16), bd_ref[...], preferred_element_type=F32) * (1.0 / HEAD_DIM)
        return z * lax.rsqrt(ss + EPS) * gain_ref[...]

    q_ref[0] = (head_norm(proj(COL_Q, COL_K), gq_ref) * Q_SCALE).astype(BF16)
    kn = head_norm(proj(COL_K, COL_V), gk_ref)
    kf_ref[0] = kn
    kb_ref[0] = kn.astype(BF16)
    v = proj(COL_V, COL_GC)
    vf_ref[0] = v
    vb_ref[0] = v.astype(BF16)


def _in_proj(x, hist, lw, tm):
    b, s, _ = x.shape
    grid = (b, s // tm)
    tok = lambda width: pl.BlockSpec((1, tm, width), lambda i, j: (i, j, 0))
    out_shape = (
        jax.ShapeDtypeStruct((b, s, QK_DIM), BF16),
        jax.ShapeDtypeStruct((b, s, QK_DIM), F32),
        jax.ShapeDtypeStruct((b, s, QK_DIM), BF16),
        jax.ShapeDtypeStruct((b, s, ATTN_DIM), F32),
        jax.ShapeDtypeStruct((b, s, ATTN_DIM), BF16),
        jax.ShapeDtypeStruct((b, s, D_MODEL), BF16),
        jax.ShapeDtypeStruct((b, s, D_MODEL), BF16),
        jax.ShapeDtypeStruct((b, HIST_ROWS, CONV_DIM), F32),
    )
    return pl.pallas_call(
        _in_proj_kernel,
        grid=grid,
        in_specs=[
            tok(D_MODEL),
            pl.BlockSpec((1, HIST_ROWS, CONV_DIM), lambda i, j: (i, 0, 0)),
            _const_spec((1, D_MODEL)),
            _const_spec((D_MODEL, IN_COLS)),
            _const_spec((CONV_TAPS, CONV_DIM)),
            _const_spec((1, CONV_DIM)),
            _const_spec((1, CONV_DIM)),
            _const_spec((1, CONV_DIM)),
            _const_spec((CONV_DIM, D_MODEL)),
            _const_spec((1, QK_DIM)),
            _const_spec((1, QK_DIM)),
            _const_spec((QK_DIM, QK_DIM)),
        ],
        out_specs=(tok(QK_DIM), tok(QK_DIM), tok(QK_DIM), tok(ATTN_DIM), tok(ATTN_DIM), tok(D_MODEL), tok(D_MODEL),
                   pl.BlockSpec((1, HIST_ROWS, CONV_DIM), lambda i, j: (i, 0, 0))),
        out_shape=out_shape,
        scratch_shapes=[pltpu.VMEM((HIST_ROWS + tm, CONV_DIM), F32), pltpu.VMEM((tm, CONV_DIM), BF16)],
        compiler_params=pltpu.CompilerParams(dimension_semantics=("arbitrary", "arbitrary"),
                                             vmem_limit_bytes=VMEM_LIMIT),
        name="in_proj",
    )(x, hist, lw["g1"], lw["w_in"], lw["conv_dw"], lw["conv_db"], lw["conv_ln_g"], lw["conv_ln_b"],
      lw["w_conv_out"], lw["gq"], lw["gk"], lw["bd"])


def _lambda(lq1_ref, lk1_ref, lq2_ref, lk2_ref, lam_init):
    a = jnp.sum(lq1_ref[...] * lk1_ref[...], axis=-1, keepdims=True)
    b = jnp.sum(lq2_ref[...] * lk2_ref[...], axis=-1, keepdims=True)
    return jnp.exp(a) - jnp.exp(b) + lam_init


def _split_maps(q):
    lane = lax.broadcasted_iota(jnp.int32, q.shape, 1)
    zero = jnp.zeros_like(q)
    return jnp.where(lane < HEAD_DIM, q, zero), jnp.where(lane >= HEAD_DIM, q, zero)


def _scores(qm, k):
    return lax.dot_general(qm, k, (((1,), (1,)), ((), ())), preferred_element_type=F32)


def _sub_ln(o1, o2, lam, sg_ref, lam_init):
    o = o1 - lam * o2
    ms = jnp.mean(o * o, axis=-1, keepdims=True)
    return o * lax.rsqrt(ms + EPS) * sg_ref[...] * (1.0 - lam_init)


def _chunk_mask(s):
    row = lax.broadcasted_iota(jnp.int32, s.shape, 0) // CHUNK
    col = lax.broadcasted_iota(jnp.int32, s.shape, 1) // CHUNK
    return jnp.where(col <= row, s, -jnp.inf)


def _attn_prompt_kernel(shift_ref, q_ref, k_ref, v_ref, lq1_ref, lk1_ref, lq2_ref, lk2_ref, sg_ref, o_ref,
                        qs1, qs2, m1, l1, a1, m2, l2, a2, *, lam_init, online):
    qi = pl.program_id(2)
    tq = q_ref.shape[1]
    q1, q2 = _split_maps(q_ref[0])
    qs1[...] = q1
    qs2[...] = q2
    for m, l, a in ((m1, l1, a1), (m2, l2, a2)):
        m[...] = jnp.full(m.shape, -jnp.inf, F32)
        l[...] = jnp.zeros(l.shape, F32)
        a[...] = jnp.zeros(a.shape, F32)

    def block(kstart, nk, masked):
        k = k_ref[0, pl.ds(kstart, nk), :]
        v = v_ref[0, pl.ds(kstart, nk), :]
        for qs, m, l, a in ((qs1, m1, l1, a1), (qs2, m2, l2, a2)):
            s = _scores(qs[...], k)
            if masked:
                s = _chunk_mask(s)
            if online:
                m_prev = m[...]
                m_new = jnp.maximum(m_prev, jnp.max(s, axis=-1, keepdims=True))
                alpha = jnp.exp2(m_prev - m_new)
                p = jnp.exp2(s - m_new)
                l[...] = alpha * l[...] + jnp.sum(p, axis=-1, keepdims=True)
                a[...] = alpha * a[...] + jnp.dot(p.astype(BF16), v, preferred_element_type=F32)
                m[...] = m_new
            else:
                p = jnp.exp2(s - shift_ref[0])
                part = p[:, 0:LANES]
                for c in range(1, nk // LANES):
                    part = part + p[:, c * LANES:(c + 1) * LANES]
                l[...] += part
                a[...] += jnp.dot(p.astype(BF16), v, preferred_element_type=F32)

    def body(i, carry):
        block(pl.multiple_of(i * 2 * tq, 2 * tq), 2 * tq, False)
        return carry

    lax.fori_loop(0, qi // 2, body, 0)

    @pl.when(qi % 2 == 1)
    def _():
        block(pl.multiple_of((qi - 1) * tq, tq), tq, False)

    block(pl.multiple_of(qi * tq, tq), tq, True)

    lam = _lambda(lq1_ref, lk1_ref, lq2_ref, lk2_ref, lam_init)
    d1 = jnp.sum(l1[...], axis=-1, keepdims=True)
    d2 = jnp.sum(l2[...], axis=-1, keepdims=True)
    o_ref[0] = _sub_ln(a1[...] / d1, a2[...] / d2, lam, sg_ref, lam_init).astype(BF16)


def _attn_prompt(shift, q, k, v, lw, lam_init, tq, online):
    b, s, _ = q.shape
    grid = (b, N_HEADS, s // tq)
    qspec = pl.BlockSpec((1, tq, LANES), lambda i, h, j: (i, j, h))
    kvspec = pl.BlockSpec((1, s, LANES), lambda i, h, j: (i, 0, h))
    vec = _const_spec((1, HEAD_DIM))
    lw_width = 1 if online else LANES
    return pl.pallas_call(
        functools.partial(_attn_prompt_kernel, lam_init=lam_init, online=online),
        grid=grid,
        in_specs=[pl.BlockSpec(memory_space=pltpu.SMEM), qspec, kvspec, kvspec, vec, vec, vec, vec,
                  _const_spec((1, V_DIM))],
        out_specs=qspec,
        out_shape=jax.ShapeDtypeStruct((b, s, ATTN_DIM), BF16),
        scratch_shapes=[pltpu.VMEM((tq, LANES), BF16), pltpu.VMEM((tq, LANES), BF16),
                        pltpu.VMEM((tq, 1), F32), pltpu.VMEM((tq, lw_width), F32), pltpu.VMEM((tq, V_DIM), F32),
                        pltpu.VMEM((tq, 1), F32), pltpu.VMEM((tq, lw_width), F32), pltpu.VMEM((tq, V_DIM), F32)],
        compiler_params=pltpu.CompilerParams(dimension_semantics=("arbitrary", "arbitrary", "arbitrary"),
                                             vmem_limit_bytes=VMEM_LIMIT),
        name="attn_prompt_online" if online else "attn_prompt",
    )(shift, q, k, v, lw["lq1"], lw["lk1"], lw["lq2"], lw["lk2"], lw["subln_g"])


def _attn_sample_kernel(q_ref, kc_ref, vc_ref, kn_ref, vn_ref, lq1_ref, lk1_ref, lq2_ref, lk2_ref, sg_ref, o_ref,
                        *, lam_init):
    q1, q2 = _split_maps(q_ref[0])
    kc = kc_ref[0].astype(BF16)
    vc = vc_ref[0].astype(BF16)
    kn = kn_ref[0]
    vn = vn_ref[0]
    outs = []
    for qm in (q1, q2):
        sc = _scores(qm, kc)
        sn = _scores(qm, kn)
        m = jnp.maximum(jnp.max(sc, axis=-1, keepdims=True), jnp.max(sn, axis=-1, keepdims=True))
        pc = jnp.exp2(sc - m)
        pn = jnp.exp2(sn - m)
        l = jnp.sum(pc, axis=-1, keepdims=True) + jnp.sum(pn, axis=-1, keepdims=True)
        acc = (jnp.dot(pc.astype(BF16), vc, preferred_element_type=F32)
               + jnp.dot(pn.astype(BF16), vn, preferred_element_type=F32))
        outs.append(acc / l)
    lam = _lambda(lq1_ref, lk1_ref, lq2_ref, lk2_ref, lam_init)
    o_ref[0] = _sub_ln(outs[0], outs[1], lam, sg_ref, lam_init).astype(BF16)


def _attn_sample(q, k_cache, v_cache, k_new, v_new, lw, lam_init):
    b, t, _ = q.shape
    past = k_cache.shape[1]
    new = pl.BlockSpec((1, t, LANES), lambda i, h: (i, 0, h))
    cache = pl.BlockSpec((1, past, LANES), lambda i, h: (i, 0, h))
    vec = _const_spec((1, HEAD_DIM))
    return pl.pallas_call(
        functools.partial(_attn_sample_kernel, lam_init=lam_init),
        grid=(b, N_HEADS),
        in_specs=[new, cache, cache, new, new, vec, vec, vec, vec, _const_spec((1, V_DIM))],
        out_specs=new,
        out_shape=jax.ShapeDtypeStruct((b, t, ATTN_DIM), BF16),
        compiler_params=pltpu.CompilerParams(dimension_semantics=("arbitrary", "arbitrary"),
                                             vmem_limit_bytes=VMEM_LIMIT),
        name="attn_sample",
    )(q, k_cache, v_cache, k_new, v_new, lw["lq1"], lw["lk1"], lw["lq2"], lw["lk2"], lw["subln_g"])


def _post_kernel(attn_ref, p_ref, g_ref, x_ref, wao_ref, wo_ref, g2_ref, wrt_ref, brt_ref, xm_ref, *out_refs, routed):
    tm = x_ref.shape[0]
    ao = jnp.dot(attn_ref[...], wao_ref[...], preferred_element_type=F32)
    merged = p_ref[...].astype(F32) + g_ref[...].astype(F32) * ao
    xm = x_ref[...] + jnp.dot(merged.astype(BF16), wo_ref[...], preferred_element_type=F32)
    xm_ref[...] = xm
    ms = jnp.mean(xm * xm, axis=-1, keepdims=True)
    h2 = xm * lax.rsqrt(ms + EPS) * g2_ref[...]

    h_hi = h2.astype(BF16)
    h_lo = (h2 - h_hi.astype(F32)).astype(BF16)
    r = (jnp.dot(h_hi, wrt_ref[...], preferred_element_type=F32)
         + jnp.dot(h_lo, wrt_ref[...], preferred_element_type=F32))
    lg = r[:, :LANES] + r[:, LANES:] + brt_ref[...]
    lane = lax.broadcasted_iota(jnp.int32, lg.shape, 1)
    neg = jnp.full_like(lg, -jnp.inf)
    big = jnp.full_like(lane, LANES)

    def first_max(z):
        zmax = jnp.max(z, axis=-1, keepdims=True)
        return zmax, jnp.min(jnp.where(z == zmax, lane, big), axis=-1, keepdims=True)

    is_group = lane < N_GROUPS
    gmax, gi = first_max(jnp.where(is_group, lg, neg))
    g_w = 1.0 / jnp.sum(jnp.where(is_group, jnp.exp(lg - gmax), 0.0), axis=-1, keepdims=True)
    lo = N_GROUPS + EXPERTS_PER_GROUP * gi
    el = jnp.where((lane >= lo) & (lane < lo + EXPERTS_PER_GROUP), lg, neg)
    v1, i1 = first_max(el)
    v2, i2 = first_max(jnp.where(lane == i1, neg, el))
    e2 = jnp.exp(v2 - v1)
    w1 = g_w / (1.0 + e2)
    w2 = g_w * e2 / (1.0 + e2)
    comb = jnp.where(lane == i1 - N_GROUPS, w1, 0.0) + jnp.where(lane == i2 - N_GROUPS, w2, 0.0)
    if routed:
        rows_ref, gid_ref = out_refs
        _store_tokens(rows_ref, jnp.concatenate([h2, comb], axis=1), ROW_SPAN)
        gid = jnp.where(lane == 0, gi.astype(F32), 0.0).astype(BF16)
        pick = (lax.broadcasted_iota(jnp.int32, (8, LANES), 1) == 0).astype(BF16)
        gid_ref[0] = lax.dot_general(pick, gid, (((1,), (1,)), ((), ())), preferred_element_type=F32)
    else:
        h2_ref, comb_ref = out_refs
        h2_ref[...] = h_hi
        comb_ref[...] = comb


def _post(attn, p, g, x, lw, tm, routed):
    t = x.shape[0]
    tok = lambda width: pl.BlockSpec((tm, width), lambda i: (i, 0))
    if routed:
        out_specs = (tok(D_MODEL), pl.BlockSpec((tm * ROW_SPAN, LANES), lambda i: (i, 0)),
                     pl.BlockSpec((1, 8, tm), lambda i: (i, 0, 0)))
        out_shape = (jax.ShapeDtypeStruct((t, D_MODEL), F32), jax.ShapeDtypeStruct((t * ROW_SPAN, LANES), F32),
                     jax.ShapeDtypeStruct((t // tm, 8, tm), F32))
    else:
        out_specs = (tok(D_MODEL), tok(D_MODEL), tok(LANES))
        out_shape = (jax.ShapeDtypeStruct((t, D_MODEL), F32), jax.ShapeDtypeStruct((t, D_MODEL), BF16),
                     jax.ShapeDtypeStruct((t, LANES), F32))
    return pl.pallas_call(
        functools.partial(_post_kernel, routed=routed),
        grid=(t // tm,),
        in_specs=[tok(ATTN_DIM), tok(D_MODEL), tok(D_MODEL), tok(D_MODEL),
                  _const_spec((ATTN_DIM, D_MODEL)), _const_spec((D_MODEL, D_MODEL)), _const_spec((1, D_MODEL)),
                  _const_spec((D_MODEL, 2 * LANES)), _const_spec((1, LANES))],
        out_specs=out_specs,
        out_shape=out_shape,
        compiler_params=pltpu.CompilerParams(dimension_semantics=("arbitrary",), vmem_limit_bytes=VMEM_LIMIT),
        name="post_routed" if routed else "post",
    )(attn, p, g, x, lw["w_attn_out"], lw["w_o"], lw["g2"], lw["w_rt"], lw["b_rt"])


def _token_rows(first_token, n, c, span):
    return pl.ds(first_token * span + c, n, stride=span)


def _load_tokens(ref, first_token, n, tiles, span):
    return [ref[_token_rows(first_token, n, c, span), :] for c in range(tiles)]


def _store_tokens(ref, value, span):
    n, width = value.shape
    for c in range(span):
        tile = value[:, c * LANES:(c + 1) * LANES] if c * LANES < width else jnp.zeros((n, LANES), value.dtype)
        ref[_token_rows(0, n, c, span), :] = tile


def _token_gather(idx_ref, tokens_hbm, buf, sem, base, n, span):
    def start():
        for r in range(n):
            src = pl.ds(pl.multiple_of(idx_ref[0, 0, r] * span, span), span)
            dst = pl.ds(pl.multiple_of((base + r) * span, span), span)
            pltpu.make_async_copy(tokens_hbm.at[src, :], buf.at[dst, :], sem).start()

    def wait():
        dst = pl.ds(pl.multiple_of(base * span, span), n * span)
        pltpu.make_async_copy(tokens_hbm.at[pl.ds(0, n * span), :], buf.at[dst, :], sem).wait()

    return start, wait


def _gathered_tokens(idx_ref, idx_next_ref, tokens_hbm, buf, sems, n, tiles, span):
    j = pl.program_id(0)
    slot = j % 2
    start_cur, wait_cur = _token_gather(idx_ref, tokens_hbm, buf, sems.at[slot], slot * n, n, span)
    start_next, wait_next = _token_gather(idx_next_ref, tokens_hbm, buf, sems.at[1 - slot], (1 - slot) * n, n, span)

    @pl.when(j == 0)
    def _():
        start_cur()

    wait_cur()
    tokens = _load_tokens(buf, slot * n, n, tiles, span)
    start_next()

    def drain():
        @pl.when(j == pl.num_programs(0) - 1)
        def _():
            wait_next()

    return tokens, drain


def _idx_specs(n_steps, n):
    cur = lambda j, *_: (j, 0, 0)
    nxt = lambda j, *_: (jnp.minimum(j + 1, n_steps - 1), 0, 0)
    return [pl.BlockSpec((1, 1, n), cur, memory_space=pltpu.SMEM),
            pl.BlockSpec((1, 1, n), nxt, memory_space=pltpu.SMEM)]


def _moe_routed_kernel(group_ref, src_ref, src_next_ref, rows_hbm, wg_ref, wu_ref, wd_ref, out_ref, buf, sems):
    tile = out_ref.shape[0] // OUT_SPAN
    g = group_ref[pl.program_id(0)]
    rows, drain = _gathered_tokens(src_ref, src_next_ref, rows_hbm, buf, sems, tile, ROW_TILES, ROW_SPAN)
    t = jnp.concatenate([x.astype(BF16) for x in rows[:OUT_SPAN]], axis=1)
    comb = rows[OUT_SPAN]
    lane = lax.broadcasted_iota(jnp.int32, comb.shape, 1)
    out = jnp.zeros((tile, D_MODEL), F32)
    for e in range(EXPERTS_PER_GROUP):
        gate = jnp.dot(t, wg_ref[0, e], preferred_element_type=F32)
        up = jnp.dot(t, wu_ref[0, e], preferred_element_type=F32)
        he = (gate * _sigmoid(gate) * up).astype(BF16)
        d = jnp.dot(he, wd_ref[0, e], preferred_element_type=F32)
        c = jnp.sum(jnp.where(lane == g * EXPERTS_PER_GROUP + e, comb, 0.0), axis=-1, keepdims=True)
        out = out + c * d
    _store_tokens(out_ref, out, OUT_SPAN)
    drain()


def _moe_routed(tile_group, src, rows, lw, tile):
    n_tiles = tile_group.shape[0]
    wspec = lambda a, b: pl.BlockSpec((1, EXPERTS_PER_GROUP, a, b), lambda j, grp: (grp[j], 0, 0, 0))
    return pl.pallas_call(
        _moe_routed_kernel,
        grid_spec=pltpu.PrefetchScalarGridSpec(
            num_scalar_prefetch=1,
            grid=(n_tiles,),
            in_specs=_idx_specs(n_tiles, tile) + [
                pl.BlockSpec(memory_space=pl.ANY),
                wspec(D_MODEL, EXPERT_HIDDEN), wspec(D_MODEL, EXPERT_HIDDEN), wspec(EXPERT_HIDDEN, D_MODEL)],
            out_specs=pl.BlockSpec((tile * OUT_SPAN, LANES), lambda j, grp: (j, 0)),
            scratch_shapes=[pltpu.VMEM((2 * tile * ROW_SPAN, LANES), F32), pltpu.SemaphoreType.DMA((2,))],
        ),
        out_shape=jax.ShapeDtypeStruct((n_tiles * tile * OUT_SPAN, LANES), F32),
        compiler_params=pltpu.CompilerParams(dimension_semantics=("arbitrary",), vmem_limit_bytes=VMEM_LIMIT),
        name="moe_routed",
    )(tile_group, src, src, rows, lw["w_gate"], lw["w_up"], lw["w_down"])


def _combine_kernel(dest_ref, dest_next_ref, xm_ref, sorted_hbm, y_ref, buf, sems):
    tm = xm_ref.shape[0]
    moe, drain = _gathered_tokens(dest_ref, dest_next_ref, sorted_hbm, buf, sems, tm, OUT_SPAN, OUT_SPAN)
    y_ref[...] = xm_ref[...] + jnp.concatenate(moe, axis=1)
    drain()


def _combine(dest, xm, moe_sorted, tm):
    t = xm.shape[0]
    n = t // tm
    idx = dest.reshape(n, 1, tm)
    return pl.pallas_call(
        _combine_kernel,
        grid=(n,),
        in_specs=_idx_specs(n, tm) + [pl.BlockSpec((tm, D_MODEL), lambda j: (j, 0)),
                                      pl.BlockSpec(memory_space=pl.ANY)],
        out_specs=pl.BlockSpec((tm, D_MODEL), lambda j: (j, 0)),
        out_shape=jax.ShapeDtypeStruct((t, D_MODEL), F32),
        scratch_shapes=[pltpu.VMEM((2 * tm * OUT_SPAN, LANES), F32), pltpu.SemaphoreType.DMA((2,))],
        compiler_params=pltpu.CompilerParams(dimension_semantics=("arbitrary",), vmem_limit_bytes=VMEM_LIMIT),
        name="moe_combine",
    )(idx, idx, xm, moe_sorted)


def _routing_tables(group_id, tile):
    t = group_id.shape[0]
    n_tiles = t // tile + N_GROUPS
    onehot = (group_id[:, None] == jnp.arange(N_GROUPS, dtype=jnp.int32)[None, :]).astype(jnp.int32)
    csum = jnp.cumsum(onehot, axis=0)
    rank = jnp.sum(onehot * csum, axis=1) - 1
    tiles_per_group = (csum[-1] + tile - 1) // tile
    tile_end = jnp.cumsum(tiles_per_group)
    offset = (tile_end - tiles_per_group) * tile
    dest = jnp.sum(onehot * offset[None, :], axis=1) + rank
    src = jnp.zeros((n_tiles * tile,), jnp.int32).at[dest].set(jnp.arange(t, dtype=jnp.int32))
    tile_group = jnp.sum(jnp.arange(n_tiles, dtype=jnp.int32)[:, None] >= tile_end[None, :], axis=1)
    tile_group = jnp.minimum(tile_group, N_GROUPS - 1).astype(jnp.int32)
    return dest.astype(jnp.int32), src.reshape(n_tiles, 1, tile), tile_group


def _moe_kernel(h2_ref, comb_ref, xm_ref, wg_ref, wu_ref, wd_ref, y_ref):
    e = pl.program_id(1)

    @pl.when(e == 0)
    def _():
        y_ref[...] = xm_ref[...]

    t = h2_ref[...]
    gate = jnp.dot(t, wg_ref[0], preferred_element_type=F32)
    up = jnp.dot(t, wu_ref[0], preferred_element_type=F32)
    he = (gate * _sigmoid(gate) * up).astype(BF16)
    d = jnp.dot(he, wd_ref[0], preferred_element_type=F32)
    comb = comb_ref[...]
    lane = lax.broadcasted_iota(jnp.int32, comb.shape, 1)
    c = jnp.sum(jnp.where(lane == e, comb, 0.0), axis=-1, keepdims=True)
    y_ref[...] += c * d


def _moe(h2, comb, xm, lw, tm):
    t = h2.shape[0]
    tok = lambda width: pl.BlockSpec((tm, width), lambda i, e: (i, 0))
    return pl.pallas_call(
        _moe_kernel,
        grid=(t // tm, N_EXPERTS),
        in_specs=[tok(D_MODEL), tok(LANES), tok(D_MODEL),
                  pl.BlockSpec((1, D_MODEL, EXPERT_HIDDEN), lambda i, e: (e, 0, 0)),
                  pl.BlockSpec((1, D_MODEL, EXPERT_HIDDEN), lambda i, e: (e, 0, 0)),
                  pl.BlockSpec((1, EXPERT_HIDDEN, D_MODEL), lambda i, e: (e, 0, 0))],
        out_specs=tok(D_MODEL),
        out_shape=jax.ShapeDtypeStruct((t, D_MODEL), F32),
        compiler_params=pltpu.CompilerParams(dimension_semantics=("arbitrary", "arbitrary"),
                                             vmem_limit_bytes=VMEM_LIMIT),
        name="moe",
    )(h2, comb, xm, *(lw[n].reshape(N_EXPERTS, *lw[n].shape[2:]) for n in ("w_gate", "w_up", "w_down")))


def _hi_lo_columns(w):
    hi = w.astype(BF16)
    return jnp.concatenate([hi, (w - hi.astype(F32)).astype(BF16)], axis=1)


def _layer_weights(l, norm1_g, w_in, conv_dw, conv_db, conv_ln_g, conv_ln_b, w_conv_out, q_norm_g, k_norm_g,
                   lambda_q1, lambda_k1, lambda_q2, lambda_k2, subln_g, w_attn_out, w_o, norm2_g, w_group, b_group,
                   w_router, b_router, w_gate, w_up, w_down):
    row = lambda a: a.reshape(1, -1).astype(F32)
    grouped = lambda w: w.astype(BF16).reshape(N_GROUPS, EXPERTS_PER_GROUP, *w.shape[1:])
    head_gain = lambda gain: jnp.tile(gain.reshape(-1), N_HEADS).reshape(1, QK_DIM).astype(F32)
    grp = jnp.arange(QK_DIM) // HEAD_DIM
    w_rt = jnp.concatenate([w_group[l], jnp.moveaxis(w_router[l], 0, 1).reshape(D_MODEL, N_EXPERTS)], axis=1)
    b_rt = jnp.concatenate([b_group[l], b_router[l].reshape(-1)])
    pad = LANES - N_GROUPS - N_EXPERTS
    return {
        "g1": row(norm1_g[l]), "w_in": w_in[l].astype(BF16), "conv_dw": conv_dw[l].astype(F32),
        "conv_db": row(conv_db[l]), "conv_ln_g": row(conv_ln_g[l]), "conv_ln_b": row(conv_ln_b[l]),
        "w_conv_out": w_conv_out[l].astype(BF16), "gq": head_gain(q_norm_g[l]), "gk": head_gain(k_norm_g[l]),
        "bd": (grp[:, None] == grp[None, :]).astype(BF16),
        "lq1": row(lambda_q1[l]), "lk1": row(lambda_k1[l]), "lq2": row(lambda_q2[l]), "lk2": row(lambda_k2[l]),
        "subln_g": row(subln_g[l]), "w_attn_out": w_attn_out[l].astype(BF16), "w_o": w_o[l].astype(BF16),
        "g2": row(norm2_g[l]),
        "w_rt": _hi_lo_columns(jnp.pad(w_rt.astype(F32), ((0, 0), (0, pad)))),
        "b_rt": jnp.pad(b_rt.astype(F32), (0, pad)).reshape(1, LANES),
        "w_gate": grouped(w_gate[l]), "w_up": grouped(w_up[l]), "w_down": grouped(w_down[l]),
    }


def _tail_dense(x, attn, p, g, lw, tm):
    b, s, _ = x.shape
    flat = lambda a: a.reshape(b * s, a.shape[-1])
    xm, h2, comb = _post(flat(attn), flat(p), flat(g), flat(x), lw, tm, False)
    return _moe(h2, comb, xm, lw, tm).reshape(b, s, D_MODEL)


def _tail_routed(x, attn, p, g, lw, tm, tile):
    b, s, _ = x.shape
    flat = lambda a: a.reshape(b * s, a.shape[-1])
    xm, rows, gid = _post(flat(attn), flat(p), flat(g), flat(x), lw, tm, True)
    dest, src, tile_group = _routing_tables(gid[:, 0, :].reshape(-1).astype(jnp.int32), tile)
    moe_sorted = _moe_routed(tile_group, src, rows, lw, tile)
    return _combine(dest, xm, moe_sorted, tm).reshape(b, s, D_MODEL)


def kernel(x_prompt, x_sample, cache_k, cache_v, state_conv, norm1_g, w_in, conv_dw, conv_db, conv_ln_g, conv_ln_b, w_conv_out, q_norm_g, k_norm_g, lambda_q1, lambda_k1, lambda_q2, lambda_k2, subln_g, w_attn_out, w_o, norm2_g, w_group, b_group, w_router, b_router, w_gate, w_up, w_down):
    depth = w_in.shape[0]
    bp, sp, _ = x_prompt.shape
    bs, ss, _ = x_sample.shape
    past = cache_k.shape[2]
    xp, xs = x_prompt, x_sample
    kp_l, vp_l, cp_l, ks_l, vs_l, cs_l = [], [], [], [], [], []
    for l in range(depth):
        lw = _layer_weights(l, norm1_g, w_in, conv_dw, conv_db, conv_ln_g, conv_ln_b, w_conv_out, q_norm_g,
                            k_norm_g, lambda_q1, lambda_k1, lambda_q2, lambda_k2, subln_g, w_attn_out, w_o,
                            norm2_g, w_group, b_group, w_router, b_router, w_gate, w_up, w_down)
        lam_init = 0.8 - 0.6 * math.exp(-0.3 * l)

        hist_p = jnp.zeros((bp, HIST_ROWS, CONV_DIM), F32)
        q, kf, kb, vf, vb, p, g, cs = _in_proj(xp, hist_p, lw, 512)
        bound = (HEAD_DIM * SCALE) * jnp.max(jnp.abs(lw["gq"])) * jnp.max(jnp.abs(lw["gk"]))
        shift = (bound * LOG2E).reshape(1)
        attn = lax.cond(bound <= MAX_FIXED_SHIFT,
                        lambda *a: _attn_prompt(*a, lw, lam_init, 512, False),
                        lambda *a: _attn_prompt(*a, lw, lam_init, 512, True),
                        shift, q, kb, vb)
        xp = _tail_routed(xp, attn, p, g, lw, 512, MOE_TILE)
        kp_l.append(kf.reshape(bp, sp, N_HEADS, 2, HEAD_DIM))
        vp_l.append(vf.reshape(bp, sp, N_HEADS, V_DIM))
        cp_l.append(cs[:, HIST_PAD:, :])

        hist_s = jnp.pad(state_conv[l].astype(F32), ((0, 0), (HIST_PAD, 0), (0, 0)))
        q, kf, kb, vf, vb, p, g, cs = _in_proj(xs, hist_s, lw, ss)
        attn = _attn_sample(q, cache_k[l].astype(BF16).reshape(bs, past, QK_DIM),
                            cache_v[l].astype(BF16).reshape(bs, past, ATTN_DIM),
                            kb, vb, lw, lam_init)
        xs = _tail_dense(xs, attn, p, g, lw, bs * ss)
        ks_l.append(kf.reshape(bs, ss, N_HEADS, 2, HEAD_DIM))
        vs_l.append(vf.reshape(bs, ss, N_HEADS, V_DIM))
        cs_l.append(cs[:, HIST_PAD:, :])
    return (xp, xs, jnp.stack(kp_l), jnp.stack(vp_l), jnp.stack(cp_l),
            jnp.stack(ks_l), jnp.stack(vs_l), jnp.stack(cs_l))
```

```python
import functools
import math

import jax
import jax.numpy as jnp
from jax import lax
from jax.experimental import pallas as pl
from jax.experimental.pallas import tpu as pltpu

D_MODEL = 1024
CHUNK = 64
CONV_DIM = 512
CONV_TAPS = 31
CONV_STATE = CONV_TAPS - 1
N_HEADS = 4
HEAD_DIM = 64
V_DIM = 2 * HEAD_DIM
ATTN_DIM = N_HEADS * V_DIM
QK_DIM = N_HEADS * 2 * HEAD_DIM
N_GROUPS = 4
EXPERTS_PER_GROUP = 4
N_EXPERTS = N_GROUPS * EXPERTS_PER_GROUP
TOP_K_INNER = 2
EXPERT_HIDDEN = 512
EPS = 1e-6
SCALE = HEAD_DIM ** -0.5
LOG2E = math.log2(math.e)
Q_SCALE = SCALE * LOG2E
MAX_FIXED_SHIFT = 40.0
COL_GLU = 2 * CONV_DIM
COL_Q = COL_GLU
COL_K = COL_Q + QK_DIM
COL_V = COL_K + QK_DIM
COL_GC = COL_V + ATTN_DIM
COL_GA = COL_GC + D_MODEL
IN_COLS = COL_GA + D_MODEL

LANES = 128
HIST_ROWS = 32
HIST_PAD = HIST_ROWS - CONV_STATE
CONV_ROW_CHUNK = 64
VMEM_LIMIT = 56 * 1024 * 1024
OUT_SPAN = D_MODEL // LANES
ROW_TILES = OUT_SPAN + 1
ROW_SPAN = 16
MOE_TILE = 512
ATTN_QUERY_BLOCK = 1024
ATTN_KEY_BLOCK = 512
ATTN_KEYS_PER_TRIP = 1024

BF16 = jnp.bfloat16
F32 = jnp.float32


def _sigmoid(x):
    return 0.5 * jnp.tanh(0.5 * x) + 0.5


def _const_spec(shape):
    n = len(shape)
    return pl.BlockSpec(shape, lambda *_: (0,) * n)


def _in_proj_kernel(x_ref, hist_ref, g1_ref, w_in_ref, dw_ref, db_ref, lng_ref, lnb_ref, wco_ref, gq_ref, gk_ref,
                    bd_ref, q_ref, kf_ref, kb_ref, vf_ref, vb_ref, p_ref, g_ref, cs_ref, cbuf, ybuf):
    t = pl.program_id(1)
    tm = x_ref.shape[1]
    x = x_ref[0]
    ms = jnp.mean(x * x, axis=-1, keepdims=True)
    h = (x * lax.rsqrt(ms + EPS) * g1_ref[...]).astype(BF16)

    def proj(lo, hi):
        return jnp.dot(h, w_in_ref[:, lo:hi], preferred_element_type=F32)

    u = proj(0, CONV_DIM) * _sigmoid(proj(CONV_DIM, COL_GLU))

    @pl.when(t == 0)
    def _():
        cbuf[0:HIST_ROWS, :] = hist_ref[0]

    @pl.when(t > 0)
    def _():
        cbuf[0:HIST_ROWS, :] = cbuf[tm:tm + HIST_ROWS, :]

    cbuf[HIST_ROWS:HIST_ROWS + tm, :] = u
    cs_ref[0] = cbuf[tm:tm + HIST_ROWS, :]

    rc = min(CONV_ROW_CHUNK, tm)
    for c in range(tm // rc):
        acc = None
        for res in range(8):
            rows = rc if res == 0 else rc + 8
            z = None
            for off in range(res, CONV_TAPS + HIST_PAD, 8):
                j = off - HIST_PAD
                if j < 0:
                    continue
                lo = c * rc + off - res
                term = dw_ref[j:j + 1, :] * cbuf[lo:lo + rows, :]
                z = term if z is None else z + term
            z = z[res:res + rc, :]
            acc = z if acc is None else acc + z
        y = acc + db_ref[...]
        mu = jnp.mean(y, axis=-1, keepdims=True)
        yc = y - mu
        var = jnp.mean(yc * yc, axis=-1, keepdims=True)
        yn = yc * lax.rsqrt(var + EPS) * lng_ref[...] + lnb_ref[...]
        ybuf[c * rc:(c + 1) * rc, :] = (yn * _sigmoid(yn)).astype(BF16)
    conv_out = jnp.dot(ybuf[...], wco_ref[...], preferred_element_type=F32)
    p_ref[0] = (_sigmoid(proj(COL_GC, COL_GA)) * conv_out).astype(BF16)
    g_ref[0] = _sigmoid(proj(COL_GA, IN_COLS)).astype(BF16)

    def head_norm(z, gain_ref):
        ss = jnp.dot((z * z).astype(BF16), bd_ref[...], preferred_element_type=F32) * (1.0 / HEAD_DIM)
        return z * lax.rsqrt(ss + EPS) * gain_ref[...]

    q_ref[0] = (head_norm(proj(COL_Q, COL_K), gq_ref) * Q_SCALE).astype(BF16)
    kn = head_norm(proj(COL_K, COL_V), gk_ref)
    kf_ref[0] = kn
    kb_ref[0] = kn.astype(BF16)
    v = proj(COL_V, COL_GC)
    vf_ref[0] = v
    vb_ref[0] = v.astype(BF16)


def _in_proj(x, hist, lw, tm):
    b, s, _ = x.shape
    grid = (b, s // tm)
    tok = lambda width: pl.BlockSpec((1, tm, width), lambda i, j: (i, j, 0))
    out_shape = (
        jax.ShapeDtypeStruct((b, s, QK_DIM), BF16),
        jax.ShapeDtypeStruct((b, s, QK_DIM), F32),
        jax.ShapeDtypeStruct((b, s, QK_DIM), BF16),
        jax.ShapeDtypeStruct((b, s, ATTN_DIM), F32),
        jax.ShapeDtypeStruct((b, s, ATTN_DIM), BF16),
        jax.ShapeDtypeStruct((b, s, D_MODEL), BF16),
        jax.ShapeDtypeStruct((b, s, D_MODEL), BF16),
        jax.ShapeDtypeStruct((b, HIST_ROWS, CONV_DIM), F32),
    )
    return pl.pallas_call(
        _in_proj_kernel,
        grid=grid,
        in_specs=[
            tok(D_MODEL),
            pl.BlockSpec((1, HIST_ROWS, CONV_DIM), lambda i, j: (i, 0, 0)),
            _const_spec((1, D_MODEL)),
            _const_spec((D_MODEL, IN_COLS)),
            _const_spec((CONV_TAPS, CONV_DIM)),
            _const_spec((1, CONV_DIM)),
            _const_spec((1, CONV_DIM)),
            _const_spec((1, CONV_DIM)),
            _const_spec((CONV_DIM, D_MODEL)),
            _const_spec((1, QK_DIM)),
            _const_spec((1, QK_DIM)),
            _const_spec((QK_DIM, QK_DIM)),
        ],
        out_specs=(tok(QK_DIM), tok(QK_DIM), tok(QK_DIM), tok(ATTN_DIM), tok(ATTN_DIM), tok(D_MODEL), tok(D_MODEL),
                   pl.BlockSpec((1, HIST_ROWS, CONV_DIM), lambda i, j: (i, 0, 0))),
        out_shape=out_shape,
        scratch_shapes=[pltpu.VMEM((HIST_ROWS + tm, CONV_DIM), F32), pltpu.VMEM((tm, CONV_DIM), BF16)],
        compiler_params=pltpu.CompilerParams(dimension_semantics=("arbitrary", "arbitrary"),
                                             vmem_limit_bytes=VMEM_LIMIT),
        name="in_proj",
    )(x, hist, lw["g1"], lw["w_in"], lw["conv_dw"], lw["conv_db"], lw["conv_ln_g"], lw["conv_ln_b"],
      lw["w_conv_out"], lw["gq"], lw["gk"], lw["bd"])


def _lambda(lq1_ref, lk1_ref, lq2_ref, lk2_ref, lam_init):
    a = jnp.sum(lq1_ref[...] * lk1_ref[...], axis=-1, keepdims=True)
    b = jnp.sum(lq2_ref[...] * lk2_ref[...], axis=-1, keepdims=True)
    return jnp.exp(a) - jnp.exp(b) + lam_init


def _split_maps(q):
    lane = lax.broadcasted_iota(jnp.int32, q.shape, 1)
    zero = jnp.zeros_like(q)
    return jnp.where(lane < HEAD_DIM, q, zero), jnp.where(lane >= HEAD_DIM, q, zero)


def _scores(qm, k):
    return lax.dot_general(qm, k, (((1,), (1,)), ((), ())), preferred_element_type=F32)


def _sub_ln(o1, o2, lam, sg_ref, lam_init):
    o = o1 - lam * o2
    ms = jnp.mean(o * o, axis=-1, keepdims=True)
    return o * lax.rsqrt(ms + EPS) * sg_ref[...] * (1.0 - lam_init)


def _chunk_mask(s):
    row = lax.broadcasted_iota(jnp.int32, s.shape, 0) // CHUNK
    col = lax.broadcasted_iota(jnp.int32, s.shape, 1) // CHUNK
    return jnp.where(col <= row, s, -jnp.inf)


def _attn_prompt_kernel(shift_ref, q_ref, k_ref, v_ref, lq1_ref, lk1_ref, lq2_ref, lk2_ref, sg_ref, o_ref,
                        qs1, qs2, m1, l1, a1, m2, l2, a2, *, lam_init, online):
    qi = pl.program_id(2)
    tq = q_ref.shape[1]
    q1, q2 = _split_maps(q_ref[0])
    qs1[...] = q1
    qs2[...] = q2
    for m, l, a in ((m1, l1, a1), (m2, l2, a2)):
        m[...] = jnp.full(m.shape, -jnp.inf, F32)
        l[...] = jnp.zeros(l.shape, F32)
        a[...] = jnp.zeros(a.shape, F32)

    def block(kstart, nk, masked, r0=0):
        rows = slice(r0, tq)
        k = k_ref[0, pl.ds(kstart, nk), :]
        v = v_ref[0, pl.ds(kstart, nk), :]
        for qs, m, l, a in ((qs1, m1, l1, a1), (qs2, m2, l2, a2)):
            s = _scores(qs[rows, :], k)
            if masked:
                s = _chunk_mask(s)
            if online:
                m_prev = m[rows, :]
                m_new = jnp.maximum(m_prev, jnp.max(s, axis=-1, keepdims=True))
                alpha = jnp.exp2(m_prev - m_new)
                p = jnp.exp2(s - m_new)
                l[rows, :] = alpha * l[rows, :] + jnp.sum(p, axis=-1, keepdims=True)
                a[rows, :] = alpha * a[rows, :] + jnp.dot(p.astype(BF16), v, preferred_element_type=F32)
                m[rows, :] = m_new
            else:
                p = jnp.exp2(s - shift_ref[0])
                part = p[:, 0:LANES]
                for c in range(1, nk // LANES):
                    part = part + p[:, c * LANES:(c + 1) * LANES]
                l[rows, :] += part
                a[rows, :] += jnp.dot(p.astype(BF16), v, preferred_element_type=F32)

    def body(i, carry):
        block(pl.multiple_of(i * ATTN_KEYS_PER_TRIP, ATTN_KEYS_PER_TRIP), ATTN_KEYS_PER_TRIP, False)
        return carry

    lax.fori_loop(0, qi * (tq // ATTN_KEYS_PER_TRIP), body, 0)
    for d in range(tq // ATTN_KEY_BLOCK):
        block(pl.multiple_of(qi * tq + d * ATTN_KEY_BLOCK, ATTN_KEY_BLOCK), ATTN_KEY_BLOCK, True, d * ATTN_KEY_BLOCK)

    lam = _lambda(lq1_ref, lk1_ref, lq2_ref, lk2_ref, lam_init)
    d1 = jnp.sum(l1[...], axis=-1, keepdims=True)
    d2 = jnp.sum(l2[...], axis=-1, keepdims=True)
    o_ref[0] = _sub_ln(a1[...] / d1, a2[...] / d2, lam, sg_ref, lam_init).astype(BF16)


def _attn_prompt(shift, q, k, v, lw, lam_init, tq, online):
    b, s, _ = q.shape
    grid = (b, N_HEADS, s // tq)
    qspec = pl.BlockSpec((1, tq, LANES), lambda i, h, j: (i, j, h))
    kvspec = pl.BlockSpec((1, s, LANES), lambda i, h, j: (i, 0, h))
    vec = _const_spec((1, HEAD_DIM))
    lw_width = 1 if online else LANES
    return pl.pallas_call(
        functools.partial(_attn_prompt_kernel, lam_init=lam_init, online=online),
        grid=grid,
        in_specs=[pl.BlockSpec(memory_space=pltpu.SMEM), qspec, kvspec, kvspec, vec, vec, vec, vec,
                  _const_spec((1, V_DIM))],
        out_specs=qspec,
        out_shape=jax.ShapeDtypeStruct((b, s, ATTN_DIM), BF16),
        scratch_shapes=[pltpu.VMEM((tq, LANES), BF16), pltpu.VMEM((tq, LANES), BF16),
                        pltpu.VMEM((tq, 1), F32), pltpu.VMEM((tq, lw_width), F32), pltpu.VMEM((tq, V_DIM), F32),
                        pltpu.VMEM((tq, 1), F32), pltpu.VMEM((tq, lw_width), F32), pltpu.VMEM((tq, V_DIM), F32)],
        compiler_params=pltpu.CompilerParams(dimension_semantics=("arbitrary", "arbitrary", "arbitrary"),
                                             vmem_limit_bytes=VMEM_LIMIT),
        name="attn_prompt_online" if online else "attn_prompt",
    )(shift, q, k, v, lw["lq1"], lw["lk1"], lw["lq2"], lw["lk2"], lw["subln_g"])


def _attn_sample_kernel(q_ref, kc_ref, vc_ref, kn_ref, vn_ref, lq1_ref, lk1_ref, lq2_ref, lk2_ref, sg_ref, o_ref,
                        *, lam_init):
    q1, q2 = _split_maps(q_ref[0])
    kc = kc_ref[0].astype(BF16)
    vc = vc_ref[0].astype(BF16)
    kn = kn_ref[0]
    vn = vn_ref[0]
    outs = []
    for qm in (q1, q2):
        sc = _scores(qm, kc)
        sn = _scores(qm, kn)
        m = jnp.maximum(jnp.max(sc, axis=-1, keepdims=True), jnp.max(sn, axis=-1, keepdims=True))
        pc = jnp.exp2(sc - m)
        pn = jnp.exp2(sn - m)
        l = jnp.sum(pc, axis=-1, keepdims=True) + jnp.sum(pn, axis=-1, keepdims=True)
        acc = (jnp.dot(pc.astype(BF16), vc, preferred_element_type=F32)
               + jnp.dot(pn.astype(BF16), vn, preferred_element_type=F32))
        outs.append(acc / l)
    lam = _lambda(lq1_ref, lk1_ref, lq2_ref, lk2_ref, lam_init)
    o_ref[0] = _sub_ln(outs[0], outs[1], lam, sg_ref, lam_init).astype(BF16)


def _attn_sample(q, k_cache, v_cache, k_new, v_new, lw, lam_init):
    b, t, _ = q.shape
    past = k_cache.shape[1]
    new = pl.BlockSpec((1, t, LANES), lambda i, h: (i, 0, h))
    cache = pl.BlockSpec((1, past, LANES), lambda i, h: (i, 0, h))
    vec = _const_spec((1, HEAD_DIM))
    return pl.pallas_call(
        functools.partial(_attn_sample_kernel, lam_init=lam_init),
        grid=(b, N_HEADS),
        in_specs=[new, cache, cache, new, new, vec, vec, vec, vec, _const_spec((1, V_DIM))],
        out_specs=new,
        out_shape=jax.ShapeDtypeStruct((b, t, ATTN_DIM), BF16),
        compiler_params=pltpu.CompilerParams(dimension_semantics=("arbitrary", "arbitrary"),
                                             vmem_limit_bytes=VMEM_LIMIT),
        name="attn_sample",
    )(q, k_cache, v_cache, k_new, v_new, lw["lq1"], lw["lk1"], lw["lq2"], lw["lk2"], lw["subln_g"])


def _post_kernel(attn_ref, p_ref, g_ref, x_ref, wao_ref, wo_ref, g2_ref, wrt_ref, brt_ref, xm_ref, *out_refs, routed):
    tm = x_ref.shape[0]
    ao = jnp.dot(attn_ref[...], wao_ref[...], preferred_element_type=F32)
    merged = p_ref[...].astype(F32) + g_ref[...].astype(F32) * ao
    xm = x_ref[...] + jnp.dot(merged.astype(BF16), wo_ref[...], preferred_element_type=F32)
    xm_ref[...] = xm
    ms = jnp.mean(xm * xm, axis=-1, keepdims=True)
    h2 = xm * lax.rsqrt(ms + EPS) * g2_ref[...]

    h_hi = h2.astype(BF16)
    h_lo = (h2 - h_hi.astype(F32)).astype(BF16)
    r = (jnp.dot(h_hi, wrt_ref[...], preferred_element_type=F32)
         + jnp.dot(h_lo, wrt_ref[...], preferred_element_type=F32))
    lg = r[:, :LANES] + r[:, LANES:] + brt_ref[...]
    lane = lax.broadcasted_iota(jnp.int32, lg.shape, 1)
    neg = jnp.full_like(lg, -jnp.inf)
    big = jnp.full_like(lane, LANES)

    def first_max(z):
        zmax = jnp.max(z, axis=-1, keepdims=True)
        return zmax, jnp.min(jnp.where(z == zmax, lane, big), axis=-1, keepdims=True)

    is_group = lane < N_GROUPS
    gmax, gi = first_max(jnp.where(is_group, lg, neg))
    g_w = 1.0 / jnp.sum(jnp.where(is_group, jnp.exp(lg - gmax), 0.0), axis=-1, keepdims=True)
    lo = N_GROUPS + EXPERTS_PER_GROUP * gi
    el = jnp.where((lane >= lo) & (lane < lo + EXPERTS_PER_GROUP), lg, neg)
    v1, i1 = first_max(el)
    v2, i2 = first_max(jnp.where(lane == i1, neg, el))
    e2 = jnp.exp(v2 - v1)
    w1 = g_w / (1.0 + e2)
    w2 = g_w * e2 / (1.0 + e2)
    comb = jnp.where(lane == i1 - N_GROUPS, w1, 0.0) + jnp.where(lane == i2 - N_GROUPS, w2, 0.0)
    if routed:
        rows_ref, gid_ref = out_refs
        _store_tokens(rows_ref, jnp.concatenate([h2, comb], axis=1), ROW_SPAN)
        gid = jnp.where(lane == 0, gi.astype(F32), 0.0).astype(BF16)
        pick = (lax.broadcasted_iota(jnp.int32, (8, LANES), 1) == 0).astype(BF16)
        gid_ref[0] = lax.dot_general(pick, gid, (((1,), (1,)), ((), ())), preferred_element_type=F32)
    else:
        h2_ref, comb_ref = out_refs
        h2_ref[...] = h_hi
        comb_ref[...] = comb


def _post(attn, p, g, x, lw, tm, routed):
    t = x.shape[0]
    tok = lambda width: pl.BlockSpec((tm, width), lambda i: (i, 0))
    if routed:
        out_specs = (tok(D_MODEL), pl.BlockSpec((tm * ROW_SPAN, LANES), lambda i: (i, 0)),
                     pl.BlockSpec((1, 8, tm), lambda i: (i, 0, 0)))
        out_shape = (jax.ShapeDtypeStruct((t, D_MODEL), F32), jax.ShapeDtypeStruct((t * ROW_SPAN, LANES), F32),
                     jax.ShapeDtypeStruct((t // tm, 8, tm), F32))
    else:
        out_specs = (tok(D_MODEL), tok(D_MODEL), tok(LANES))
        out_shape = (jax.ShapeDtypeStruct((t, D_MODEL), F32), jax.ShapeDtypeStruct((t, D_MODEL), BF16),
                     jax.ShapeDtypeStruct((t, LANES), F32))
    return pl.pallas_call(
        functools.partial(_post_kernel, routed=routed),
        grid=(t // tm,),
        in_specs=[tok(ATTN_DIM), tok(D_MODEL), tok(D_MODEL), tok(D_MODEL),
                  _const_spec((ATTN_DIM, D_MODEL)), _const_spec((D_MODEL, D_MODEL)), _const_spec((1, D_MODEL)),
                  _const_spec((D_MODEL, 2 * LANES)), _const_spec((1, LANES))],
        out_specs=out_specs,
        out_shape=out_shape,
        compiler_params=pltpu.CompilerParams(dimension_semantics=("arbitrary",), vmem_limit_bytes=VMEM_LIMIT),
        name="post_routed" if routed else "post",
    )(attn, p, g, x, lw["w_attn_out"], lw["w_o"], lw["g2"], lw["w_rt"], lw["b_rt"])


def _token_rows(first_token, n, c, span):
    return pl.ds(first_token * span + c, n, stride=span)


def _load_tokens(ref, first_token, n, tiles, span):
    return [ref[_token_rows(first_token, n, c, span), :] for c in range(tiles)]


def _store_tokens(ref, value, span):
    n, width = value.shape
    for c in range(span):
        tile = value[:, c * LANES:(c + 1) * LANES] if c * LANES < width else jnp.zeros((n, LANES), value.dtype)
        ref[_token_rows(0, n, c, span), :] = tile


def _token_gather(idx_ref, tokens_hbm, buf, sem, base, n, span):
    def start():
        def one(r, carry):
            src = pl.ds(pl.multiple_of(idx_ref[0, 0, r] * span, span), span)
            dst = pl.ds(pl.multiple_of((base + r) * span, span), span)
            pltpu.make_async_copy(tokens_hbm.at[src, :], buf.at[dst, :], sem).start()
            return carry
        lax.fori_loop(0, n, one, 0, unroll=8)

    def wait():
        dst = pl.ds(pl.multiple_of(base * span, span), n * span)
        pltpu.make_async_copy(tokens_hbm.at[pl.ds(0, n * span), :], buf.at[dst, :], sem).wait()

    return start, wait


def _gathered_tokens(idx_ref, idx_next_ref, tokens_hbm, buf, sems, n, tiles, span):
    j = pl.program_id(0)
    slot = j % 2
    start_cur, wait_cur = _token_gather(idx_ref, tokens_hbm, buf, sems.at[slot], slot * n, n, span)
    start_next, _ = _token_gather(idx_next_ref, tokens_hbm, buf, sems.at[1 - slot], (1 - slot) * n, n, span)

    @pl.when(j == 0)
    def _():
        start_cur()

    @pl.when(j + 1 < pl.num_programs(0))
    def _():
        start_next()

    wait_cur()
    return _load_tokens(buf, slot * n, n, tiles, span)


def _idx_specs(n_steps, n):
    cur = lambda j, *_: (j, 0, 0)
    nxt = lambda j, *_: (jnp.minimum(j + 1, n_steps - 1), 0, 0)
    return [pl.BlockSpec((1, 1, n), cur, memory_space=pltpu.SMEM),
            pl.BlockSpec((1, 1, n), nxt, memory_space=pltpu.SMEM)]


def _moe_routed_kernel(group_ref, src_ref, src_next_ref, rows_hbm, wg_ref, wu_ref, wd_ref, out_ref, buf, sems):
    tile = out_ref.shape[0] // OUT_SPAN
    g = group_ref[pl.program_id(0)]
    rows = _gathered_tokens(src_ref, src_next_ref, rows_hbm, buf, sems, tile, ROW_TILES, ROW_SPAN)
    t = jnp.concatenate([x.astype(BF16) for x in rows[:OUT_SPAN]], axis=1)
    comb = rows[OUT_SPAN]
    lane = lax.broadcasted_iota(jnp.int32, comb.shape, 1)
    out = jnp.zeros((tile, D_MODEL), F32)
    for e in range(EXPERTS_PER_GROUP):
        gate = jnp.dot(t, wg_ref[0, e], preferred_element_type=F32)
        up = jnp.dot(t, wu_ref[0, e], preferred_element_type=F32)
        he = (gate * _sigmoid(gate) * up).astype(BF16)
        d = jnp.dot(he, wd_ref[0, e], preferred_element_type=F32)
        c = jnp.sum(jnp.where(lane == g * EXPERTS_PER_GROUP + e, comb, 0.0), axis=-1, keepdims=True)
        out = out + c * d
    _store_tokens(out_ref, out, OUT_SPAN)


def _moe_routed(tile_group, src, rows, lw, tile):
    n_tiles = tile_group.shape[0]
    wspec = lambda a, b: pl.BlockSpec((1, EXPERTS_PER_GROUP, a, b), lambda j, grp: (grp[j], 0, 0, 0))
    return pl.pallas_call(
        _moe_routed_kernel,
        grid_spec=pltpu.PrefetchScalarGridSpec(
            num_scalar_prefetch=1,
            grid=(n_tiles,),
            in_specs=_idx_specs(n_tiles, tile) + [
                pl.BlockSpec(memory_space=pl.ANY),
                wspec(D_MODEL, EXPERT_HIDDEN), wspec(D_MODEL, EXPERT_HIDDEN), wspec(EXPERT_HIDDEN, D_MODEL)],
            out_specs=pl.BlockSpec((tile * OUT_SPAN, LANES), lambda j, grp: (j, 0)),
            scratch_shapes=[pltpu.VMEM((2 * tile * ROW_SPAN, LANES), F32), pltpu.SemaphoreType.DMA((2,))],
        ),
        out_shape=jax.ShapeDtypeStruct((n_tiles * tile * OUT_SPAN, LANES), F32),
        compiler_params=pltpu.CompilerParams(dimension_semantics=("arbitrary",), vmem_limit_bytes=VMEM_LIMIT),
        name="moe_routed",
    )(tile_group, src, src, rows, lw["w_gate"], lw["w_up"], lw["w_down"])


def _combine_kernel(dest_ref, dest_next_ref, xm_ref, sorted_hbm, y_ref, buf, sems):
    tm = xm_ref.shape[0]
    moe = _gathered_tokens(dest_ref, dest_next_ref, sorted_hbm, buf, sems, tm, OUT_SPAN, OUT_SPAN)
    y_ref[...] = xm_ref[...] + jnp.concatenate(moe, axis=1)


def _combine(dest, xm, moe_sorted, tm):
    t = xm.shape[0]
    n = t // tm
    idx = dest.reshape(n, 1, tm)
    return pl.pallas_call(
        _combine_kernel,
        grid=(n,),
        in_specs=_idx_specs(n, tm) + [pl.BlockSpec((tm, D_MODEL), lambda j: (j, 0)),
                                      pl.BlockSpec(memory_space=pl.ANY)],
        out_specs=pl.BlockSpec((tm, D_MODEL), lambda j: (j, 0)),
        out_shape=jax.ShapeDtypeStruct((t, D_MODEL), F32),
        scratch_shapes=[pltpu.VMEM((2 * tm * OUT_SPAN, LANES), F32), pltpu.SemaphoreType.DMA((2,))],
        compiler_params=pltpu.CompilerParams(dimension_semantics=("arbitrary",), vmem_limit_bytes=VMEM_LIMIT),
        name="moe_combine",
    )(idx, idx, xm, moe_sorted)


def _routing_tables(group_id, tile):
    t = group_id.shape[0]
    n_tiles = t // tile + N_GROUPS
    onehot = (group_id[:, None] == jnp.arange(N_GROUPS, dtype=jnp.int32)[None, :]).astype(jnp.int32)
    csum = jnp.cumsum(onehot, axis=0)
    rank = jnp.sum(onehot * csum, axis=1) - 1
    tiles_per_group = (csum[-1] + tile - 1) // tile
    tile_end = jnp.cumsum(tiles_per_group)
    offset = (tile_end - tiles_per_group) * tile
    dest = jnp.sum(onehot * offset[None, :], axis=1) + rank
    src = jnp.zeros((n_tiles * tile,), jnp.int32).at[dest].set(jnp.arange(t, dtype=jnp.int32))
    tile_group = jnp.sum(jnp.arange(n_tiles, dtype=jnp.int32)[:, None] >= tile_end[None, :], axis=1)
    tile_group = jnp.minimum(tile_group, N_GROUPS - 1).astype(jnp.int32)
    return dest.astype(jnp.int32), src.reshape(n_tiles, 1, tile), tile_group


def _moe_kernel(h2_ref, comb_ref, xm_ref, wg_ref, wu_ref, wd_ref, y_ref):
    e = pl.program_id(1)

    @pl.when(e == 0)
    def _():
        y_ref[...] = xm_ref[...]

    t = h2_ref[...]
    gate = jnp.dot(t, wg_ref[0], preferred_element_type=F32)
    up = jnp.dot(t, wu_ref[0], preferred_element_type=F32)
    he = (gate * _sigmoid(gate) * up).astype(BF16)
    d = jnp.dot(he, wd_ref[0], preferred_element_type=F32)
    comb = comb_ref[...]
    lane = lax.broadcasted_iota(jnp.int32, comb.shape, 1)
    c = jnp.sum(jnp.where(lane == e, comb, 0.0), axis=-1, keepdims=True)
    y_ref[...] += c * d


def _moe(h2, comb, xm, lw, tm):
    t = h2.shape[0]
    tok = lambda width: pl.BlockSpec((tm, width), lambda i, e: (i, 0))
    return pl.pallas_call(
        _moe_kernel,
        grid=(t // tm, N_EXPERTS),
        in_specs=[tok(D_MODEL), tok(LANES), tok(D_MODEL),
                  pl.BlockSpec((1, D_MODEL, EXPERT_HIDDEN), lambda i, e: (e, 0, 0)),
                  pl.BlockSpec((1, D_MODEL, EXPERT_HIDDEN), lambda i, e: (e, 0, 0)),
                  pl.BlockSpec((1, EXPERT_HIDDEN, D_MODEL), lambda i, e: (e, 0, 0))],
        out_specs=tok(D_MODEL),
        out_shape=jax.ShapeDtypeStruct((t, D_MODEL), F32),
        compiler_params=pltpu.CompilerParams(dimension_semantics=("arbitrary", "arbitrary"),
                                             vmem_limit_bytes=VMEM_LIMIT),
        name="moe",
    )(h2, comb, xm, *(lw[n].reshape(N_EXPERTS, *lw[n].shape[2:]) for n in ("w_gate", "w_up", "w_down")))


def _hi_lo_columns(w):
    hi = w.astype(BF16)
    return jnp.concatenate([hi, (w - hi.astype(F32)).astype(BF16)], axis=1)


def _layer_weights(l, norm1_g, w_in, conv_dw, conv_db, conv_ln_g, conv_ln_b, w_conv_out, q_norm_g, k_norm_g,
                   lambda_q1, lambda_k1, lambda_q2, lambda_k2, subln_g, w_attn_out, w_o, norm2_g, w_group, b_group,
                   w_router, b_router, w_gate, w_up, w_down):
    row = lambda a: a.reshape(1, -1).astype(F32)
    grouped = lambda w: w.astype(BF16).reshape(N_GROUPS, EXPERTS_PER_GROUP, *w.shape[1:])
    head_gain = lambda gain: jnp.tile(gain.reshape(-1), N_HEADS).reshape(1, QK_DIM).astype(F32)
    grp = jnp.arange(QK_DIM) // HEAD_DIM
    w_rt = jnp.concatenate([w_group[l], jnp.moveaxis(w_router[l], 0, 1).reshape(D_MODEL, N_EXPERTS)], axis=1)
    b_rt = jnp.concatenate([b_group[l], b_router[l].reshape(-1)])
    pad = LANES - N_GROUPS - N_EXPERTS
    return {
        "g1": row(norm1_g[l]), "w_in": w_in[l].astype(BF16), "conv_dw": conv_dw[l].astype(F32),
        "conv_db": row(conv_db[l]), "conv_ln_g": row(conv_ln_g[l]), "conv_ln_b": row(conv_ln_b[l]),
        "w_conv_out": w_conv_out[l].astype(BF16), "gq": head_gain(q_norm_g[l]), "gk": head_gain(k_norm_g[l]),
        "bd": (grp[:, None] == grp[None, :]).astype(BF16),
        "lq1": row(lambda_q1[l]), "lk1": row(lambda_k1[l]), "lq2": row(lambda_q2[l]), "lk2": row(lambda_k2[l]),
        "subln_g": row(subln_g[l]), "w_attn_out": w_attn_out[l].astype(BF16), "w_o": w_o[l].astype(BF16),
        "g2": row(norm2_g[l]),
        "w_rt": _hi_lo_columns(jnp.pad(w_rt.astype(F32), ((0, 0), (0, pad)))),
        "b_rt": jnp.pad(b_rt.astype(F32), (0, pad)).reshape(1, LANES),
        "w_gate": grouped(w_gate[l]), "w_up": grouped(w_up[l]), "w_down": grouped(w_down[l]),
    }


def _tail_dense(x, attn, p, g, lw, tm):
    b, s, _ = x.shape
    flat = lambda a: a.reshape(b * s, a.shape[-1])
    xm, h2, comb = _post(flat(attn), flat(p), flat(g), flat(x), lw, tm, False)
    return _moe(h2, comb, xm, lw, tm).reshape(b, s, D_MODEL)


def _tail_routed(x, attn, p, g, lw, tm, tile):
    b, s, _ = x.shape
    flat = lambda a: a.reshape(b * s, a.shape[-1])
    xm, rows, gid = _post(flat(attn), flat(p), flat(g), flat(x), lw, tm, True)
    dest, src, tile_group = _routing_tables(gid[:, 0, :].reshape(-1).astype(jnp.int32), tile)
    moe_sorted = _moe_routed(tile_group, src, rows, lw, tile)
    return _combine(dest, xm, moe_sorted, tm).reshape(b, s, D_MODEL)


def kernel(x_prompt, x_sample, cache_k, cache_v, state_conv, norm1_g, w_in, conv_dw, conv_db, conv_ln_g, conv_ln_b, w_conv_out, q_norm_g, k_norm_g, lambda_q1, lambda_k1, lambda_q2, lambda_k2, subln_g, w_attn_out, w_o, norm2_g, w_group, b_group, w_router, b_router, w_gate, w_up, w_down):
    depth = w_in.shape[0]
    bp, sp, _ = x_prompt.shape
    bs, ss, _ = x_sample.shape
    past = cache_k.shape[2]
    xp, xs = x_prompt, x_sample
    kp_l, vp_l, cp_l, ks_l, vs_l, cs_l = [], [], [], [], [], []
    for l in range(depth):
        lw = _layer_weights(l, norm1_g, w_in, conv_dw, conv_db, conv_ln_g, conv_ln_b, w_conv_out, q_norm_g,
                            k_norm_g, lambda_q1, lambda_k1, lambda_q2, lambda_k2, subln_g, w_attn_out, w_o,
                            norm2_g, w_group, b_group, w_router, b_router, w_gate, w_up, w_down)
        lam_init = 0.8 - 0.6 * math.exp(-0.3 * l)

        hist_p = jnp.zeros((bp, HIST_ROWS, CONV_DIM), F32)
        q, kf, kb, vf, vb, p, g, cs = _in_proj(xp, hist_p, lw, 512)
        bound = (HEAD_DIM * SCALE) * jnp.max(jnp.abs(lw["gq"])) * jnp.max(jnp.abs(lw["gk"]))
        shift = (bound * LOG2E).reshape(1)
        attn = lax.cond(bound <= MAX_FIXED_SHIFT,
                        lambda *a: _attn_prompt(*a, lw, lam_init, ATTN_QUERY_BLOCK, False),
                        lambda *a: _attn_prompt(*a, lw, lam_init, ATTN_QUERY_BLOCK, True),
                        shift, q, kb, vb)
        xp = _tail_routed(xp, attn, p, g, lw, 512, MOE_TILE)
        kp_l.append(kf.reshape(bp, sp, N_HEADS, 2, HEAD_DIM))
        vp_l.append(vf.reshape(bp, sp, N_HEADS, V_DIM))
        cp_l.append(cs[:, HIST_PAD:, :])

        hist_s = jnp.pad(state_conv[l].astype(F32), ((0, 0), (HIST_PAD, 0), (0, 0)))
        q, kf, kb, vf, vb, p, g, cs = _in_proj(xs, hist_s, lw, ss)
        attn = _attn_sample(q, cache_k[l].astype(BF16).reshape(bs, past, QK_DIM),
                            cache_v[l].astype(BF16).reshape(bs, past, ATTN_DIM),
                            kb, vb, lw, lam_init)
        xs = _tail_dense(xs, attn, p, g, lw, bs * ss)
        ks_l.append(kf.reshape(bs, ss, N_HEADS, 2, HEAD_DIM))
        vs_l.append(vf.reshape(bs, ss, N_HEADS, V_DIM))
        cs_l.append(cs[:, HIST_PAD:, :])
    return (xp, xs, jnp.stack(kp_l), jnp.stack(vp_l), jnp.stack(cp_l),
            jnp.stack(ks_l), jnp.stack(vs_l), jnp.stack(cs_l))
```

```python
import functools
import math

import jax
import jax.numpy as jnp
from jax import lax
from jax.experimental import pallas as pl
from jax.experimental.pallas import tpu as pltpu

D_MODEL = 1024
CHUNK = 64
CONV_DIM = 512
CONV_TAPS = 31
CONV_STATE = CONV_TAPS - 1
N_HEADS = 4
HEAD_DIM = 64
V_DIM = 2 * HEAD_DIM
ATTN_DIM = N_HEADS * V_DIM
QK_DIM = N_HEADS * 2 * HEAD_DIM
N_GROUPS = 4
EXPERTS_PER_GROUP = 4
N_EXPERTS = N_GROUPS * EXPERTS_PER_GROUP
TOP_K_INNER = 2
EXPERT_HIDDEN = 512
EPS = 1e-6
SCALE = HEAD_DIM ** -0.5
LOG2E = math.log2(math.e)
Q_SCALE = SCALE * LOG2E
MAX_FIXED_SHIFT = 40.0
COL_GLU = 2 * CONV_DIM
COL_Q = COL_GLU
COL_K = COL_Q + QK_DIM
COL_V = COL_K + QK_DIM
COL_GC = COL_V + ATTN_DIM
COL_GA = COL_GC + D_MODEL
IN_COLS = COL_GA + D_MODEL

LANES = 128
HIST_ROWS = 32
HIST_PAD = HIST_ROWS - CONV_STATE
CONV_ROW_CHUNK = 64
VMEM_LIMIT = 56 * 1024 * 1024
OUT_SPAN = D_MODEL // LANES
ROW_TILES = OUT_SPAN + 1
ROW_SPAN = 16
MOE_TILE = 512
ATTN_QUERY_BLOCK = 1024
ATTN_KEY_BLOCK = 512
ATTN_KEYS_PER_TRIP = 1024

BF16 = jnp.bfloat16
F32 = jnp.float32


def _sigmoid(x):
    return 0.5 * jnp.tanh(0.5 * x) + 0.5


def _const_spec(shape):
    n = len(shape)
    return pl.BlockSpec(shape, lambda *_: (0,) * n)


def _in_proj_kernel(x_ref, hist_ref, g1_ref, w_in_ref, dw_ref, db_ref, lng_ref, lnb_ref, wco_ref, gq_ref, gk_ref,
                    bd_ref, q_ref, kf_ref, kb_ref, vf_ref, vb_ref, p_ref, g_ref, cs_ref, cbuf, ybuf):
    t = pl.program_id(1)
    tm = x_ref.shape[1]
    x = x_ref[0]
    ms = jnp.mean(x * x, axis=-1, keepdims=True)
    h = (x * lax.rsqrt(ms + EPS) * g1_ref[...]).astype(BF16)

    def proj(lo, hi):
        return jnp.dot(h, w_in_ref[:, lo:hi], preferred_element_type=F32)

    u = proj(0, CONV_DIM) * _sigmoid(proj(CONV_DIM, COL_GLU))

    @pl.when(t == 0)
    def _():
        cbuf[0:HIST_ROWS, :] = hist_ref[0]

    @pl.when(t > 0)
    def _():
        cbuf[0:HIST_ROWS, :] = cbuf[tm:tm + HIST_ROWS, :]

    cbuf[HIST_ROWS:HIST_ROWS + tm, :] = u
    cs_ref[0] = cbuf[tm:tm + HIST_ROWS, :]

    rc = min(CONV_ROW_CHUNK, tm)
    for c in range(tm // rc):
        acc = None
        for res in range(8):
            rows = rc if res == 0 else rc + 8
            z = None
            for off in range(res, CONV_TAPS + HIST_PAD, 8):
                j = off - HIST_PAD
                if j < 0:
                    continue
                lo = c * rc + off - res
                term = dw_ref[j:j + 1, :] * cbuf[lo:lo + rows, :]
                z = term if z is None else z + term
            z = z[res:res + rc, :]
            acc = z if acc is None else acc + z
        y = acc + db_ref[...]
        mu = jnp.mean(y, axis=-1, keepdims=True)
        yc = y - mu
        var = jnp.mean(yc * yc, axis=-1, keepdims=True)
        yn = yc * lax.rsqrt(var + EPS) * lng_ref[...] + lnb_ref[...]
        ybuf[c * rc:(c + 1) * rc, :] = (yn * _sigmoid(yn)).astype(BF16)
    conv_out = jnp.dot(ybuf[...], wco_ref[...], preferred_element_type=F32)
    p_ref[0] = (_sigmoid(proj(COL_GC, COL_GA)) * conv_out).astype(BF16)
    g_ref[0] = _sigmoid(proj(COL_GA, IN_COLS)).astype(BF16)

    def head_norm(z, gain_ref):
        ss = jnp.dot((z * z).astype(BF16), bd_ref[...], preferred_element_type=F32) * (1.0 / HEAD_DIM)
        return z * lax.rsqrt(ss + EPS) * gain_ref[...]

    q_ref[0] = (head_norm(proj(COL_Q, COL_K), gq_ref) * Q_SCALE).astype(BF16)
    kn = head_norm(proj(COL_K, COL_V), gk_ref)
    kb_ref[0] = kn.astype(BF16)
    v = proj(COL_V, COL_GC)
    vb_ref[0] = v.astype(BF16)
    n_maps = QK_DIM // HEAD_DIM
    for c in range(n_maps):
        kf_ref[0, pl.ds(c, tm, stride=n_maps), :] = kn[:, c * HEAD_DIM:(c + 1) * HEAD_DIM]
    for hd in range(N_HEADS):
        vf_ref[0, pl.ds(hd, tm, stride=N_HEADS), :] = v[:, hd * V_DIM:(hd + 1) * V_DIM]


def _in_proj(x, hist, lw, tm):
    b, s, _ = x.shape
    grid = (b, s // tm)
    tok = lambda width: pl.BlockSpec((1, tm, width), lambda i, j: (i, j, 0))
    out_shape = (
        jax.ShapeDtypeStruct((b, s, QK_DIM), BF16),
        jax.ShapeDtypeStruct((b, s * QK_DIM // HEAD_DIM, HEAD_DIM), F32),
        jax.ShapeDtypeStruct((b, s, QK_DIM), BF16),
        jax.ShapeDtypeStruct((b, s * N_HEADS, V_DIM), F32),
        jax.ShapeDtypeStruct((b, s, ATTN_DIM), BF16),
        jax.ShapeDtypeStruct((b, s, D_MODEL), BF16),
        jax.ShapeDtypeStruct((b, s, D_MODEL), BF16),
        jax.ShapeDtypeStruct((b, HIST_ROWS, CONV_DIM), F32),
    )
    return pl.pallas_call(
        _in_proj_kernel,
        grid=grid,
        in_specs=[
            tok(D_MODEL),
            pl.BlockSpec((1, HIST_ROWS, CONV_DIM), lambda i, j: (i, 0, 0)),
            _const_spec((1, D_MODEL)),
            _const_spec((D_MODEL, IN_COLS)),
            _const_spec((CONV_TAPS, CONV_DIM)),
            _const_spec((1, CONV_DIM)),
            _const_spec((1, CONV_DIM)),
            _const_spec((1, CONV_DIM)),
            _const_spec((CONV_DIM, D_MODEL)),
            _const_spec((1, QK_DIM)),
            _const_spec((1, QK_DIM)),
            _const_spec((QK_DIM, QK_DIM)),
        ],
        out_specs=(tok(QK_DIM), pl.BlockSpec((1, tm * QK_DIM // HEAD_DIM, HEAD_DIM), lambda i, j: (i, j, 0)),
                   tok(QK_DIM), pl.BlockSpec((1, tm * N_HEADS, V_DIM), lambda i, j: (i, j, 0)),
                   tok(ATTN_DIM), tok(D_MODEL), tok(D_MODEL),
                   pl.BlockSpec((1, HIST_ROWS, CONV_DIM), lambda i, j: (i, 0, 0))),
        out_shape=out_shape,
        scratch_shapes=[pltpu.VMEM((HIST_ROWS + tm, CONV_DIM), F32), pltpu.VMEM((tm, CONV_DIM), BF16)],
        compiler_params=pltpu.CompilerParams(dimension_semantics=("arbitrary", "arbitrary"),
                                             vmem_limit_bytes=VMEM_LIMIT),
        name="in_proj",
    )(x, hist, lw["g1"], lw["w_in"], lw["conv_dw"], lw["conv_db"], lw["conv_ln_g"], lw["conv_ln_b"],
      lw["w_conv_out"], lw["gq"], lw["gk"], lw["bd"])


def _lambda(lq1_ref, lk1_ref, lq2_ref, lk2_ref, lam_init):
    a = jnp.sum(lq1_ref[...] * lk1_ref[...], axis=-1, keepdims=True)
    b = jnp.sum(lq2_ref[...] * lk2_ref[...], axis=-1, keepdims=True)
    return jnp.exp(a) - jnp.exp(b) + lam_init


def _split_maps(q):
    lane = lax.broadcasted_iota(jnp.int32, q.shape, 1)
    zero = jnp.zeros_like(q)
    return jnp.where(lane < HEAD_DIM, q, zero), jnp.where(lane >= HEAD_DIM, q, zero)


def _scores(qm, k):
    return lax.dot_general(qm, k, (((1,), (1,)), ((), ())), preferred_element_type=F32)


def _sub_ln(o1, o2, lam, sg_ref, lam_init):
    o = o1 - lam * o2
    ms = jnp.mean(o * o, axis=-1, keepdims=True)
    return o * lax.rsqrt(ms + EPS) * sg_ref[...] * (1.0 - lam_init)


def _chunk_mask(s):
    row = lax.broadcasted_iota(jnp.int32, s.shape, 0) // CHUNK
    col = lax.broadcasted_iota(jnp.int32, s.shape, 1) // CHUNK
    return jnp.where(col <= row, s, -jnp.inf)


def _attn_prompt_kernel(shift_ref, q_ref, k_ref, v_ref, lq1_ref, lk1_ref, lq2_ref, lk2_ref, sg_ref, o_ref,
                        qs1, qs2, m1, l1, a1, m2, l2, a2, *, lam_init, online):
    qi = pl.program_id(2)
    tq = q_ref.shape[1]
    q1, q2 = _split_maps(q_ref[0])
    qs1[...] = q1
    qs2[...] = q2
    for m, l, a in ((m1, l1, a1), (m2, l2, a2)):
        m[...] = jnp.full(m.shape, -jnp.inf, F32)
        l[...] = jnp.zeros(l.shape, F32)
        a[...] = jnp.zeros(a.shape, F32)

    def block(kstart, nk, masked, r0=0):
        rows = slice(r0, tq)
        k = k_ref[0, pl.ds(kstart, nk), :]
        v = v_ref[0, pl.ds(kstart, nk), :]
        for qs, m, l, a in ((qs1, m1, l1, a1), (qs2, m2, l2, a2)):
            s = _scores(qs[rows, :], k)
            if masked:
                s = _chunk_mask(s)
            if online:
                m_prev = m[rows, :]
                m_new = jnp.maximum(m_prev, jnp.max(s, axis=-1, keepdims=True))
                alpha = jnp.exp2(m_prev - m_new)
                p = jnp.exp2(s - m_new)
                l[rows, :] = alpha * l[rows, :] + jnp.sum(p, axis=-1, keepdims=True)
                a[rows, :] = alpha * a[rows, :] + jnp.dot(p.astype(BF16), v, preferred_element_type=F32)
                m[rows, :] = m_new
            else:
                p = jnp.exp2(s - shift_ref[0])
                part = p[:, 0:LANES]
                for c in range(1, nk // LANES):
                    part = part + p[:, c * LANES:(c + 1) * LANES]
                l[rows, :] += part
                a[rows, :] += jnp.dot(p.astype(BF16), v, preferred_element_type=F32)

    def body(i, carry):
        block(pl.multiple_of(i * ATTN_KEYS_PER_TRIP, ATTN_KEYS_PER_TRIP), ATTN_KEYS_PER_TRIP, False)
        return carry

    lax.fori_loop(0, qi * (tq // ATTN_KEYS_PER_TRIP), body, 0)
    for d in range(tq // ATTN_KEY_BLOCK):
        block(pl.multiple_of(qi * tq + d * ATTN_KEY_BLOCK, ATTN_KEY_BLOCK), ATTN_KEY_BLOCK, True, d * ATTN_KEY_BLOCK)

    lam = _lambda(lq1_ref, lk1_ref, lq2_ref, lk2_ref, lam_init)
    d1 = jnp.sum(l1[...], axis=-1, keepdims=True)
    d2 = jnp.sum(l2[...], axis=-1, keepdims=True)
    o_ref[0] = _sub_ln(a1[...] / d1, a2[...] / d2, lam, sg_ref, lam_init).astype(BF16)


def _attn_prompt(shift, q, k, v, lw, lam_init, tq, online):
    b, s, _ = q.shape
    grid = (b, N_HEADS, s // tq)
    qspec = pl.BlockSpec((1, tq, LANES), lambda i, h, j: (i, j, h))
    kvspec = pl.BlockSpec((1, s, LANES), lambda i, h, j: (i, 0, h))
    vec = _const_spec((1, HEAD_DIM))
    lw_width = 1 if online else LANES
    return pl.pallas_call(
        functools.partial(_attn_prompt_kernel, lam_init=lam_init, online=online),
        grid=grid,
        in_specs=[pl.BlockSpec(memory_space=pltpu.SMEM), qspec, kvspec, kvspec, vec, vec, vec, vec,
                  _const_spec((1, V_DIM))],
        out_specs=qspec,
        out_shape=jax.ShapeDtypeStruct((b, s, ATTN_DIM), BF16),
        scratch_shapes=[pltpu.VMEM((tq, LANES), BF16), pltpu.VMEM((tq, LANES), BF16),
                        pltpu.VMEM((tq, 1), F32), pltpu.VMEM((tq, lw_width), F32), pltpu.VMEM((tq, V_DIM), F32),
                        pltpu.VMEM((tq, 1), F32), pltpu.VMEM((tq, lw_width), F32), pltpu.VMEM((tq, V_DIM), F32)],
        compiler_params=pltpu.CompilerParams(dimension_semantics=("arbitrary", "arbitrary", "arbitrary"),
                                             vmem_limit_bytes=VMEM_LIMIT),
        name="attn_prompt_online" if online else "attn_prompt",
    )(shift, q, k, v, lw["lq1"], lw["lk1"], lw["lq2"], lw["lk2"], lw["subln_g"])


def _attn_sample_kernel(q_ref, kc_ref, vc_ref, kn_ref, vn_ref, lq1_ref, lk1_ref, lq2_ref, lk2_ref, sg_ref, o_ref,
                        *, lam_init):
    q1, q2 = _split_maps(q_ref[0])
    kc = kc_ref[0].astype(BF16)
    past = vc_ref.shape[1] // N_HEADS
    vc = vc_ref[0, pl.ds(pl.program_id(1), past, stride=N_HEADS), :].astype(BF16)
    kn = kn_ref[0]
    vn = vn_ref[0]
    outs = []
    for qm in (q1, q2):
        sc = _scores(qm, kc)
        sn = _scores(qm, kn)
        m = jnp.maximum(jnp.max(sc, axis=-1, keepdims=True), jnp.max(sn, axis=-1, keepdims=True))
        pc = jnp.exp2(sc - m)
        pn = jnp.exp2(sn - m)
        l = jnp.sum(pc, axis=-1, keepdims=True) + jnp.sum(pn, axis=-1, keepdims=True)
        acc = (jnp.dot(pc.astype(BF16), vc, preferred_element_type=F32)
               + jnp.dot(pn.astype(BF16), vn, preferred_element_type=F32))
        outs.append(acc / l)
    lam = _lambda(lq1_ref, lk1_ref, lq2_ref, lk2_ref, lam_init)
    o_ref[0] = _sub_ln(outs[0], outs[1], lam, sg_ref, lam_init).astype(BF16)


def _attn_sample(q, k_cache, v_cache, k_new, v_new, lw, lam_init):
    b, t, _ = q.shape
    past = k_cache.shape[1]
    new = pl.BlockSpec((1, t, LANES), lambda i, h: (i, 0, h))
    cache = pl.BlockSpec((1, past, LANES), lambda i, h: (i, 0, h))
    cache_v = pl.BlockSpec((1, past * N_HEADS, V_DIM), lambda i, h: (i, 0, 0))
    vec = _const_spec((1, HEAD_DIM))
    return pl.pallas_call(
        functools.partial(_attn_sample_kernel, lam_init=lam_init),
        grid=(b, N_HEADS),
        in_specs=[new, cache, cache_v, new, new, vec, vec, vec, vec, _const_spec((1, V_DIM))],
        out_specs=new,
        out_shape=jax.ShapeDtypeStruct((b, t, ATTN_DIM), BF16),
        compiler_params=pltpu.CompilerParams(dimension_semantics=("arbitrary", "arbitrary"),
                                             vmem_limit_bytes=VMEM_LIMIT),
        name="attn_sample",
    )(q, k_cache, v_cache, k_new, v_new, lw["lq1"], lw["lk1"], lw["lq2"], lw["lk2"], lw["subln_g"])


def _post_kernel(attn_ref, p_ref, g_ref, x_ref, wao_ref, wo_ref, g2_ref, wrt_ref, brt_ref, xm_ref, *out_refs, routed):
    tm = x_ref.shape[0]
    ao = jnp.dot(attn_ref[...], wao_ref[...], preferred_element_type=F32)
    merged = p_ref[...].astype(F32) + g_ref[...].astype(F32) * ao
    xm = x_ref[...] + jnp.dot(merged.astype(BF16), wo_ref[...], preferred_element_type=F32)
    xm_ref[...] = xm
    ms = jnp.mean(xm * xm, axis=-1, keepdims=True)
    h2 = xm * lax.rsqrt(ms + EPS) * g2_ref[...]

    h_hi = h2.astype(BF16)
    h_lo = (h2 - h_hi.astype(F32)).astype(BF16)
    r = (jnp.dot(h_hi, wrt_ref[...], preferred_element_type=F32)
         + jnp.dot(h_lo, wrt_ref[...], preferred_element_type=F32))
    lg = r[:, :LANES] + r[:, LANES:] + brt_ref[...]
    lane = lax.broadcasted_iota(jnp.int32, lg.shape, 1)
    neg = jnp.full_like(lg, -jnp.inf)
    big = jnp.full_like(lane, LANES)

    def first_max(z):
        zmax = jnp.max(z, axis=-1, keepdims=True)
        return zmax, jnp.min(jnp.where(z == zmax, lane, big), axis=-1, keepdims=True)

    is_group = lane < N_GROUPS
    gmax, gi = first_max(jnp.where(is_group, lg, neg))
    g_w = 1.0 / jnp.sum(jnp.where(is_group, jnp.exp(lg - gmax), 0.0), axis=-1, keepdims=True)
    lo = N_GROUPS + EXPERTS_PER_GROUP * gi
    el = jnp.where((lane >= lo) & (lane < lo + EXPERTS_PER_GROUP), lg, neg)
    v1, i1 = first_max(el)
    v2, i2 = first_max(jnp.where(lane == i1, neg, el))
    e2 = jnp.exp(v2 - v1)
    w1 = g_w / (1.0 + e2)
    w2 = g_w * e2 / (1.0 + e2)
    comb = jnp.where(lane == i1 - N_GROUPS, w1, 0.0) + jnp.where(lane == i2 - N_GROUPS, w2, 0.0)
    if routed:
        rows_ref, gid_ref = out_refs
        _store_tokens(rows_ref, jnp.concatenate([h2, comb], axis=1), ROW_SPAN)
        gid = jnp.where(lane == 0, gi.astype(F32), 0.0).astype(BF16)
        pick = (lax.broadcasted_iota(jnp.int32, (8, LANES), 1) == 0).astype(BF16)
        gid_ref[0] = lax.dot_general(pick, gid, (((1,), (1,)), ((), ())), preferred_element_type=F32)
    else:
        h2_ref, comb_ref = out_refs
        h2_ref[...] = h_hi
        comb_ref[...] = comb


def _post(attn, p, g, x, lw, tm, routed):
    t = x.shape[0]
    tok = lambda width: pl.BlockSpec((tm, width), lambda i: (i, 0))
    if routed:
        out_specs = (tok(D_MODEL), pl.BlockSpec((tm * ROW_SPAN, LANES), lambda i: (i, 0)),
                     pl.BlockSpec((1, 8, tm), lambda i: (i, 0, 0)))
        out_shape = (jax.ShapeDtypeStruct((t, D_MODEL), F32), jax.ShapeDtypeStruct((t * ROW_SPAN, LANES), F32),
                     jax.ShapeDtypeStruct((t // tm, 8, tm), F32))
    else:
        out_specs = (tok(D_MODEL), tok(D_MODEL), tok(LANES))
        out_shape = (jax.ShapeDtypeStruct((t, D_MODEL), F32), jax.ShapeDtypeStruct((t, D_MODEL), BF16),
                     jax.ShapeDtypeStruct((t, LANES), F32))
    return pl.pallas_call(
        functools.partial(_post_kernel, routed=routed),
        grid=(t // tm,),
        in_specs=[tok(ATTN_DIM), tok(D_MODEL), tok(D_MODEL), tok(D_MODEL),
                  _const_spec((ATTN_DIM, D_MODEL)), _const_spec((D_MODEL, D_MODEL)), _const_spec((1, D_MODEL)),
                  _const_spec((D_MODEL, 2 * LANES)), _const_spec((1, LANES))],
        out_specs=out_specs,
        out_shape=out_shape,
        compiler_params=pltpu.CompilerParams(dimension_semantics=("arbitrary",), vmem_limit_bytes=VMEM_LIMIT),
        name="post_routed" if routed else "post",
    )(attn, p, g, x, lw["w_attn_out"], lw["w_o"], lw["g2"], lw["w_rt"], lw["b_rt"])


def _token_rows(first_token, n, c, span):
    return pl.ds(first_token * span + c, n, stride=span)


def _load_tokens(ref, first_token, n, tiles, span):
    return [ref[_token_rows(first_token, n, c, span), :] for c in range(tiles)]


def _store_tokens(ref, value, span):
    n, width = value.shape
    for c in range(span):
        tile = value[:, c * LANES:(c + 1) * LANES] if c * LANES < width else jnp.zeros((n, LANES), value.dtype)
        ref[_token_rows(0, n, c, span), :] = tile


def _token_gather(idx_ref, tokens_hbm, buf, sem, base, n, span):
    def start():
        def one(r, carry):
            src = pl.ds(pl.multiple_of(idx_ref[0, 0, r] * span, span), span)
            dst = pl.ds(pl.multiple_of((base + r) * span, span), span)
            pltpu.make_async_copy(tokens_hbm.at[src, :], buf.at[dst, :], sem).start()
            return carry
        lax.fori_loop(0, n, one, 0, unroll=8)

    def wait():
        dst = pl.ds(pl.multiple_of(base * span, span), n * span)
        pltpu.make_async_copy(tokens_hbm.at[pl.ds(0, n * span), :], buf.at[dst, :], sem).wait()

    return start, wait


def _gathered_tokens(idx_ref, idx_next_ref, tokens_hbm, buf, sems, n, tiles, span):
    j = pl.program_id(0)
    slot = j % 2
    start_cur, wait_cur = _token_gather(idx_ref, tokens_hbm, buf, sems.at[slot], slot * n, n, span)
    start_next, _ = _token_gather(idx_next_ref, tokens_hbm, buf, sems.at[1 - slot], (1 - slot) * n, n, span)

    @pl.when(j == 0)
    def _():
        start_cur()

    @pl.when(j + 1 < pl.num_programs(0))
    def _():
        start_next()

    wait_cur()
    return _load_tokens(buf, slot * n, n, tiles, span)


def _idx_specs(n_steps, n):
    cur = lambda j, *_: (j, 0, 0)
    nxt = lambda j, *_: (jnp.minimum(j + 1, n_steps - 1), 0, 0)
    return [pl.BlockSpec((1, 1, n), cur, memory_space=pltpu.SMEM),
            pl.BlockSpec((1, 1, n), nxt, memory_space=pltpu.SMEM)]


def _moe_routed_kernel(group_ref, src_ref, src_next_ref, rows_hbm, wg_ref, wu_ref, wd_ref, out_ref, buf, sems):
    tile = out_ref.shape[0] // OUT_SPAN
    g = group_ref[pl.program_id(0)]
    rows = _gathered_tokens(src_ref, src_next_ref, rows_hbm, buf, sems, tile, ROW_TILES, ROW_SPAN)
    t = jnp.concatenate([x.astype(BF16) for x in rows[:OUT_SPAN]], axis=1)
    comb = rows[OUT_SPAN]
    lane = lax.broadcasted_iota(jnp.int32, comb.shape, 1)
    out = jnp.zeros((tile, D_MODEL), F32)
    for e in range(EXPERTS_PER_GROUP):
        gate = jnp.dot(t, wg_ref[0, e], preferred_element_type=F32)
        up = jnp.dot(t, wu_ref[0, e], preferred_element_type=F32)
        he = (gate * _sigmoid(gate) * up).astype(BF16)
        d = jnp.dot(he, wd_ref[0, e], preferred_element_type=F32)
        c = jnp.sum(jnp.where(lane == g * EXPERTS_PER_GROUP + e, comb, 0.0), axis=-1, keepdims=True)
        out = out + c * d
    _store_tokens(out_ref, out, OUT_SPAN)


def _moe_routed(tile_group, src, rows, lw, tile):
    n_tiles = tile_group.shape[0]
    wspec = lambda a, b: pl.BlockSpec((1, EXPERTS_PER_GROUP, a, b), lambda j, grp: (grp[j], 0, 0, 0))
    return pl.pallas_call(
        _moe_routed_kernel,
        grid_spec=pltpu.PrefetchScalarGridSpec(
            num_scalar_prefetch=1,
            grid=(n_tiles,),
            in_specs=_idx_specs(n_tiles, tile) + [
                pl.BlockSpec(memory_space=pl.ANY),
                wspec(D_MODEL, EXPERT_HIDDEN), wspec(D_MODEL, EXPERT_HIDDEN), wspec(EXPERT_HIDDEN, D_MODEL)],
            out_specs=pl.BlockSpec((tile * OUT_SPAN, LANES), lambda j, grp: (j, 0)),
            scratch_shapes=[pltpu.VMEM((2 * tile * ROW_SPAN, LANES), F32), pltpu.SemaphoreType.DMA((2,))],
        ),
        out_shape=jax.ShapeDtypeStruct((n_tiles * tile * OUT_SPAN, LANES), F32),
        compiler_params=pltpu.CompilerParams(dimension_semantics=("arbitrary",), vmem_limit_bytes=VMEM_LIMIT),
        name="moe_routed",
    )(tile_group, src, src, rows, lw["w_gate"], lw["w_up"], lw["w_down"])


def _combine_kernel(dest_ref, dest_next_ref, xm_ref, sorted_hbm, y_ref, buf, sems):
    tm = xm_ref.shape[0]
    moe = _gathered_tokens(dest_ref, dest_next_ref, sorted_hbm, buf, sems, tm, OUT_SPAN, OUT_SPAN)
    y_ref[...] = xm_ref[...] + jnp.concatenate(moe, axis=1)


def _combine(dest, xm, moe_sorted, tm):
    t = xm.shape[0]
    n = t // tm
    idx = dest.reshape(n, 1, tm)
    return pl.pallas_call(
        _combine_kernel,
        grid=(n,),
        in_specs=_idx_specs(n, tm) + [pl.BlockSpec((tm, D_MODEL), lambda j: (j, 0)),
                                      pl.BlockSpec(memory_space=pl.ANY)],
        out_specs=pl.BlockSpec((tm, D_MODEL), lambda j: (j, 0)),
        out_shape=jax.ShapeDtypeStruct((t, D_MODEL), F32),
        scratch_shapes=[pltpu.VMEM((2 * tm * OUT_SPAN, LANES), F32), pltpu.SemaphoreType.DMA((2,))],
        compiler_params=pltpu.CompilerParams(dimension_semantics=("arbitrary",), vmem_limit_bytes=VMEM_LIMIT),
        name="moe_combine",
    )(idx, idx, xm, moe_sorted)


def _routing_tables(group_id, tile):
    t = group_id.shape[0]
    n_tiles = t // tile + N_GROUPS
    onehot = (group_id[:, None] == jnp.arange(N_GROUPS, dtype=jnp.int32)[None, :]).astype(jnp.int32)
    csum = jnp.cumsum(onehot, axis=0)
    rank = jnp.sum(onehot * csum, axis=1) - 1
    tiles_per_group = (csum[-1] + tile - 1) // tile
    tile_end = jnp.cumsum(tiles_per_group)
    offset = (tile_end - tiles_per_group) * tile
    dest = jnp.sum(onehot * offset[None, :], axis=1) + rank
    src = jnp.zeros((n_tiles * tile,), jnp.int32).at[dest].set(jnp.arange(t, dtype=jnp.int32))
    tile_group = jnp.sum(jnp.arange(n_tiles, dtype=jnp.int32)[:, None] >= tile_end[None, :], axis=1)
    tile_group = jnp.minimum(tile_group, N_GROUPS - 1).astype(jnp.int32)
    return dest.astype(jnp.int32), src.reshape(n_tiles, 1, tile), tile_group


def _moe_kernel(h2_ref, comb_ref, xm_ref, wg_ref, wu_ref, wd_ref, y_ref):
    e = pl.program_id(1)

    @pl.when(e == 0)
    def _():
        y_ref[...] = xm_ref[...]

    t = h2_ref[...]
    gate = jnp.dot(t, wg_ref[0], preferred_element_type=F32)
    up = jnp.dot(t, wu_ref[0], preferred_element_type=F32)
    he = (gate * _sigmoid(gate) * up).astype(BF16)
    d = jnp.dot(he, wd_ref[0], preferred_element_type=F32)
    comb = comb_ref[...]
    lane = lax.broadcasted_iota(jnp.int32, comb.shape, 1)
    c = jnp.sum(jnp.where(lane == e, comb, 0.0), axis=-1, keepdims=True)
    y_ref[...] += c * d


def _moe(h2, comb, xm, lw, tm):
    t = h2.shape[0]
    tok = lambda width: pl.BlockSpec((tm, width), lambda i, e: (i, 0))
    return pl.pallas_call(
        _moe_kernel,
        grid=(t // tm, N_EXPERTS),
        in_specs=[tok(D_MODEL), tok(LANES), tok(D_MODEL),
                  pl.BlockSpec((1, D_MODEL, EXPERT_HIDDEN), lambda i, e: (e, 0, 0)),
                  pl.BlockSpec((1, D_MODEL, EXPERT_HIDDEN), lambda i, e: (e, 0, 0)),
                  pl.BlockSpec((1, EXPERT_HIDDEN, D_MODEL), lambda i, e: (e, 0, 0))],
        out_specs=tok(D_MODEL),
        out_shape=jax.ShapeDtypeStruct((t, D_MODEL), F32),
        compiler_params=pltpu.CompilerParams(dimension_semantics=("arbitrary", "arbitrary"),
                                             vmem_limit_bytes=VMEM_LIMIT),
        name="moe",
    )(h2, comb, xm, *(lw[n].reshape(N_EXPERTS, *lw[n].shape[2:]) for n in ("w_gate", "w_up", "w_down")))


def _hi_lo_columns(w):
    hi = w.astype(BF16)
    return jnp.concatenate([hi, (w - hi.astype(F32)).astype(BF16)], axis=1)


def _layer_weights(l, norm1_g, w_in, conv_dw, conv_db, conv_ln_g, conv_ln_b, w_conv_out, q_norm_g, k_norm_g,
                   lambda_q1, lambda_k1, lambda_q2, lambda_k2, subln_g, w_attn_out, w_o, norm2_g, w_group, b_group,
                   w_router, b_router, w_gate, w_up, w_down):
    row = lambda a: a.reshape(1, -1).astype(F32)
    grouped = lambda w: w.astype(BF16).reshape(N_GROUPS, EXPERTS_PER_GROUP, *w.shape[1:])
    head_gain = lambda gain: jnp.tile(gain.reshape(-1), N_HEADS).reshape(1, QK_DIM).astype(F32)
    grp = jnp.arange(QK_DIM) // HEAD_DIM
    w_rt = jnp.concatenate([w_group[l], jnp.moveaxis(w_router[l], 0, 1).reshape(D_MODEL, N_EXPERTS)], axis=1)
    b_rt = jnp.concatenate([b_group[l], b_router[l].reshape(-1)])
    pad = LANES - N_GROUPS - N_EXPERTS
    return {
        "g1": row(norm1_g[l]), "w_in": w_in[l].astype(BF16), "conv_dw": conv_dw[l].astype(F32),
        "conv_db": row(conv_db[l]), "conv_ln_g": row(conv_ln_g[l]), "conv_ln_b": row(conv_ln_b[l]),
        "w_conv_out": w_conv_out[l].astype(BF16), "gq": head_gain(q_norm_g[l]), "gk": head_gain(k_norm_g[l]),
        "bd": (grp[:, None] == grp[None, :]).astype(BF16),
        "lq1": row(lambda_q1[l]), "lk1": row(lambda_k1[l]), "lq2": row(lambda_q2[l]), "lk2": row(lambda_k2[l]),
        "subln_g": row(subln_g[l]), "w_attn_out": w_attn_out[l].astype(BF16), "w_o": w_o[l].astype(BF16),
        "g2": row(norm2_g[l]),
        "w_rt": _hi_lo_columns(jnp.pad(w_rt.astype(F32), ((0, 0), (0, pad)))),
        "b_rt": jnp.pad(b_rt.astype(F32), (0, pad)).reshape(1, LANES),
        "w_gate": grouped(w_gate[l]), "w_up": grouped(w_up[l]), "w_down": grouped(w_down[l]),
    }


def _tail_dense(x, attn, p, g, lw, tm):
    b, s, _ = x.shape
    flat = lambda a: a.reshape(b * s, a.shape[-1])
    xm, h2, comb = _post(flat(attn), flat(p), flat(g), flat(x), lw, tm, False)
    return _moe(h2, comb, xm, lw, tm).reshape(b, s, D_MODEL)


def _tail_routed(x, attn, p, g, lw, tm, tile):
    b, s, _ = x.shape
    flat = lambda a: a.reshape(b * s, a.shape[-1])
    xm, rows, gid = _post(flat(attn), flat(p), flat(g), flat(x), lw, tm, True)
    dest, src, tile_group = _routing_tables(gid[:, 0, :].reshape(-1).astype(jnp.int32), tile)
    moe_sorted = _moe_routed(tile_group, src, rows, lw, tile)
    return _combine(dest, xm, moe_sorted, tm).reshape(b, s, D_MODEL)


def kernel(x_prompt, x_sample, cache_k, cache_v, state_conv, norm1_g, w_in, conv_dw, conv_db, conv_ln_g, conv_ln_b, w_conv_out, q_norm_g, k_norm_g, lambda_q1, lambda_k1, lambda_q2, lambda_k2, subln_g, w_attn_out, w_o, norm2_g, w_group, b_group, w_router, b_router, w_gate, w_up, w_down):
    depth = w_in.shape[0]
    bp, sp, _ = x_prompt.shape
    bs, ss, _ = x_sample.shape
    past = cache_k.shape[2]
    xp, xs = x_prompt, x_sample
    kp_l, vp_l, cp_l, ks_l, vs_l, cs_l = [], [], [], [], [], []
    for l in range(depth):
        lw = _layer_weights(l, norm1_g, w_in, conv_dw, conv_db, conv_ln_g, conv_ln_b, w_conv_out, q_norm_g,
                            k_norm_g, lambda_q1, lambda_k1, lambda_q2, lambda_k2, subln_g, w_attn_out, w_o,
                            norm2_g, w_group, b_group, w_router, b_router, w_gate, w_up, w_down)
        lam_init = 0.8 - 0.6 * math.exp(-0.3 * l)

        hist_p = jnp.zeros((bp, HIST_ROWS, CONV_DIM), F32)
        q, kf, kb, vf, vb, p, g, cs = _in_proj(xp, hist_p, lw, 512)
        bound = (HEAD_DIM * SCALE) * jnp.max(jnp.abs(lw["gq"])) * jnp.max(jnp.abs(lw["gk"]))
        shift = (bound * LOG2E).reshape(1)
        attn = lax.cond(bound <= MAX_FIXED_SHIFT,
                        lambda *a: _attn_prompt(*a, lw, lam_init, ATTN_QUERY_BLOCK, False),
                        lambda *a: _attn_prompt(*a, lw, lam_init, ATTN_QUERY_BLOCK, True),
                        shift, q, kb, vb)
        xp = _tail_routed(xp, attn, p, g, lw, 512, MOE_TILE)
        kp_l.append(kf.reshape(bp, sp, N_HEADS, 2, HEAD_DIM))
        vp_l.append(vf.reshape(bp, sp, N_HEADS, V_DIM))
        cp_l.append(cs[:, HIST_PAD:, :])

        hist_s = jnp.pad(state_conv[l].astype(F32), ((0, 0), (HIST_PAD, 0), (0, 0)))
        q, kf, kb, vf, vb, p, g, cs = _in_proj(xs, hist_s, lw, ss)
        attn = _attn_sample(q, cache_k[l].astype(BF16).reshape(bs, past, QK_DIM),
                            cache_v[l].reshape(bs, past * N_HEADS, V_DIM),
                            kb, vb, lw, lam_init)
        xs = _tail_dense(xs, attn, p, g, lw, bs * ss)
        ks_l.append(kf.reshape(bs, ss, N_HEADS, 2, HEAD_DIM))
        vs_l.append(vf.reshape(bs, ss, N_HEADS, V_DIM))
        cs_l.append(cs[:, HIST_PAD:, :])
    return (xp, xs, jnp.stack(kp_l), jnp.stack(vp_l), jnp.stack(cp_l),
            jnp.stack(ks_l), jnp.stack(vs_l), jnp.stack(cs_l))
```

```python
import functools
import math

import jax
import jax.numpy as jnp
from jax import lax
from jax.experimental import pallas as pl
from jax.experimental.pallas import tpu as pltpu

D_MODEL = 1024
CHUNK = 64
CONV_DIM = 512
CONV_TAPS = 31
CONV_STATE = CONV_TAPS - 1
N_HEADS = 4
HEAD_DIM = 64
V_DIM = 2 * HEAD_DIM
ATTN_DIM = N_HEADS * V_DIM
QK_DIM = N_HEADS * 2 * HEAD_DIM
N_GROUPS = 4
EXPERTS_PER_GROUP = 4
N_EXPERTS = N_GROUPS * EXPERTS_PER_GROUP
TOP_K_INNER = 2
EXPERT_HIDDEN = 512
EPS = 1e-6
SCALE = HEAD_DIM ** -0.5
LOG2E = math.log2(math.e)
Q_SCALE = SCALE * LOG2E
MAX_FIXED_SHIFT = 40.0
COL_GLU = 2 * CONV_DIM
COL_Q = COL_GLU
COL_K = COL_Q + QK_DIM
COL_V = COL_K + QK_DIM
COL_GC = COL_V + ATTN_DIM
COL_GA = COL_GC + D_MODEL
IN_COLS = COL_GA + D_MODEL

LANES = 128
HIST_ROWS = 32
HIST_PAD = HIST_ROWS - CONV_STATE
CONV_ROW_CHUNK = 64
VMEM_LIMIT = 56 * 1024 * 1024
OUT_SPAN = D_MODEL // LANES
ROW_TILES = OUT_SPAN + 1
ROW_SPAN = 16
MOE_TILE = 512
ATTN_QUERY_BLOCK = 2048
ATTN_KEY_BLOCK = 512
ATTN_KEYS_PER_TRIP = 1024

BF16 = jnp.bfloat16
F32 = jnp.float32


def _sigmoid(x):
    return 0.5 * jnp.tanh(0.5 * x) + 0.5


def _const_spec(shape):
    n = len(shape)
    return pl.BlockSpec(shape, lambda *_: (0,) * n)


def _in_proj_kernel(x_ref, hist_ref, g1_ref, w_in_ref, dw_ref, db_ref, lng_ref, lnb_ref, wco_ref, gq_ref, gk_ref,
                    bd_ref, q_ref, kf_ref, kb_ref, vf_ref, vb_ref, p_ref, g_ref, cs_ref, cbuf, ybuf):
    t = pl.program_id(1)
    tm = x_ref.shape[1]
    x = x_ref[0]
    ms = jnp.mean(x * x, axis=-1, keepdims=True)
    h = (x * lax.rsqrt(ms + EPS) * g1_ref[...]).astype(BF16)

    def proj(lo, hi):
        return jnp.dot(h, w_in_ref[:, lo:hi], preferred_element_type=F32)

    u = proj(0, CONV_DIM) * _sigmoid(proj(CONV_DIM, COL_GLU))

    @pl.when(t == 0)
    def _():
        cbuf[0:HIST_ROWS, :] = hist_ref[0]

    @pl.when(t > 0)
    def _():
        cbuf[0:HIST_ROWS, :] = cbuf[tm:tm + HIST_ROWS, :]

    cbuf[HIST_ROWS:HIST_ROWS + tm, :] = u
    cs_ref[0] = cbuf[tm:tm + HIST_ROWS, :]

    rc = min(CONV_ROW_CHUNK, tm)
    for c in range(tm // rc):
        acc = None
        for res in range(8):
            rows = rc if res == 0 else rc + 8
            z = None
            for off in range(res, CONV_TAPS + HIST_PAD, 8):
                j = off - HIST_PAD
                if j < 0:
                    continue
                lo = c * rc + off - res
                term = dw_ref[j:j + 1, :] * cbuf[lo:lo + rows, :]
                z = term if z is None else z + term
            z = z[res:res + rc, :]
            acc = z if acc is None else acc + z
        y = acc + db_ref[...]
        mu = jnp.mean(y, axis=-1, keepdims=True)
        yc = y - mu
        var = jnp.mean(yc * yc, axis=-1, keepdims=True)
        yn = yc * lax.rsqrt(var + EPS) * lng_ref[...] + lnb_ref[...]
        ybuf[c * rc:(c + 1) * rc, :] = (yn * _sigmoid(yn)).astype(BF16)
    conv_out = jnp.dot(ybuf[...], wco_ref[...], preferred_element_type=F32)
    p_ref[0] = (_sigmoid(proj(COL_GC, COL_GA)) * conv_out).astype(BF16)
    g_ref[0] = _sigmoid(proj(COL_GA, IN_COLS)).astype(BF16)

    def head_norm(z, gain_ref):
        ss = jnp.dot((z * z).astype(BF16), bd_ref[...], preferred_element_type=F32) * (1.0 / HEAD_DIM)
        return z * lax.rsqrt(ss + EPS) * gain_ref[...]

    q_ref[0] = (head_norm(proj(COL_Q, COL_K), gq_ref) * Q_SCALE).astype(BF16)
    kn = head_norm(proj(COL_K, COL_V), gk_ref)
    kb_ref[0] = kn.astype(BF16)
    v = proj(COL_V, COL_GC)
    vb_ref[0] = v.astype(BF16)
    n_maps = QK_DIM // HEAD_DIM
    for c in range(n_maps):
        kf_ref[0, pl.ds(c, tm, stride=n_maps), :] = kn[:, c * HEAD_DIM:(c + 1) * HEAD_DIM]
    for hd in range(N_HEADS):
        vf_ref[0, pl.ds(hd, tm, stride=N_HEADS), :] = v[:, hd * V_DIM:(hd + 1) * V_DIM]


def _in_proj(x, hist, lw, tm):
    b, s, _ = x.shape
    grid = (b, s // tm)
    tok = lambda width: pl.BlockSpec((1, tm, width), lambda i, j: (i, j, 0))
    out_shape = (
        jax.ShapeDtypeStruct((b, s, QK_DIM), BF16),
        jax.ShapeDtypeStruct((b, s * QK_DIM // HEAD_DIM, HEAD_DIM), F32),
        jax.ShapeDtypeStruct((b, s, QK_DIM), BF16),
        jax.ShapeDtypeStruct((b, s * N_HEADS, V_DIM), F32),
        jax.ShapeDtypeStruct((b, s, ATTN_DIM), BF16),
        jax.ShapeDtypeStruct((b, s, D_MODEL), BF16),
        jax.ShapeDtypeStruct((b, s, D_MODEL), BF16),
        jax.ShapeDtypeStruct((b, HIST_ROWS, CONV_DIM), F32),
    )
    return pl.pallas_call(
        _in_proj_kernel,
        grid=grid,
        in_specs=[
            tok(D_MODEL),
            pl.BlockSpec((1, HIST_ROWS, CONV_DIM), lambda i, j: (i, 0, 0)),
            _const_spec((1, D_MODEL)),
            _const_spec((D_MODEL, IN_COLS)),
            _const_spec((CONV_TAPS, CONV_DIM)),
            _const_spec((1, CONV_DIM)),
            _const_spec((1, CONV_DIM)),
            _const_spec((1, CONV_DIM)),
            _const_spec((CONV_DIM, D_MODEL)),
            _const_spec((1, QK_DIM)),
            _const_spec((1, QK_DIM)),
            _const_spec((QK_DIM, QK_DIM)),
        ],
        out_specs=(tok(QK_DIM), pl.BlockSpec((1, tm * QK_DIM // HEAD_DIM, HEAD_DIM), lambda i, j: (i, j, 0)),
                   tok(QK_DIM), pl.BlockSpec((1, tm * N_HEADS, V_DIM), lambda i, j: (i, j, 0)),
                   tok(ATTN_DIM), tok(D_MODEL), tok(D_MODEL),
                   pl.BlockSpec((1, HIST_ROWS, CONV_DIM), lambda i, j: (i, 0, 0))),
        out_shape=out_shape,
        scratch_shapes=[pltpu.VMEM((HIST_ROWS + tm, CONV_DIM), F32), pltpu.VMEM((tm, CONV_DIM), BF16)],
        compiler_params=pltpu.CompilerParams(dimension_semantics=("arbitrary", "arbitrary"),
                                             vmem_limit_bytes=VMEM_LIMIT),
        name="in_proj",
    )(x, hist, lw["g1"], lw["w_in"], lw["conv_dw"], lw["conv_db"], lw["conv_ln_g"], lw["conv_ln_b"],
      lw["w_conv_out"], lw["gq"], lw["gk"], lw["bd"])


def _lambda(lq1_ref, lk1_ref, lq2_ref, lk2_ref, lam_init):
    a = jnp.sum(lq1_ref[...] * lk1_ref[...], axis=-1, keepdims=True)
    b = jnp.sum(lq2_ref[...] * lk2_ref[...], axis=-1, keepdims=True)
    return jnp.exp(a) - jnp.exp(b) + lam_init


def _split_maps(q):
    lane = lax.broadcasted_iota(jnp.int32, q.shape, 1)
    zero = jnp.zeros_like(q)
    return jnp.where(lane < HEAD_DIM, q, zero), jnp.where(lane >= HEAD_DIM, q, zero)


def _scores(qm, k):
    return lax.dot_general(qm, k, (((1,), (1,)), ((), ())), preferred_element_type=F32)


def _sub_ln(o1, o2, lam, sg_ref, lam_init):
    o = o1 - lam * o2
    ms = jnp.mean(o * o, axis=-1, keepdims=True)
    return o * lax.rsqrt(ms + EPS) * sg_ref[...] * (1.0 - lam_init)


def _chunk_mask(s):
    row = lax.broadcasted_iota(jnp.int32, s.shape, 0) // CHUNK
    col = lax.broadcasted_iota(jnp.int32, s.shape, 1) // CHUNK
    return jnp.where(col <= row, s, -jnp.inf)


def _attn_prompt_kernel(shift_ref, q_ref, k_ref, v_ref, lq1_ref, lk1_ref, lq2_ref, lk2_ref, sg_ref, o_ref,
                        qs1, qs2, m1, l1, a1, m2, l2, a2, *, lam_init, online):
    qi = pl.program_id(2)
    tq = q_ref.shape[1]
    q1, q2 = _split_maps(q_ref[0])
    qs1[...] = q1
    qs2[...] = q2
    for m, l, a in ((m1, l1, a1), (m2, l2, a2)):
        m[...] = jnp.full(m.shape, -jnp.inf, F32)
        l[...] = jnp.zeros(l.shape, F32)
        a[...] = jnp.zeros(a.shape, F32)

    def block(kstart, nk, masked, r0=0):
        rows = slice(r0, tq)
        k = k_ref[0, pl.ds(kstart, nk), :]
        v = v_ref[0, pl.ds(kstart, nk), :]
        for qs, m, l, a in ((qs1, m1, l1, a1), (qs2, m2, l2, a2)):
            s = _scores(qs[rows, :], k)
            if masked:
                s = _chunk_mask(s)
            if online:
                m_prev = m[rows, :]
                m_new = jnp.maximum(m_prev, jnp.max(s, axis=-1, keepdims=True))
                alpha = jnp.exp2(m_prev - m_new)
                p = jnp.exp2(s - m_new)
                l[rows, :] = alpha * l[rows, :] + jnp.sum(p, axis=-1, keepdims=True)
                a[rows, :] = alpha * a[rows, :] + jnp.dot(p.astype(BF16), v, preferred_element_type=F32)
                m[rows, :] = m_new
            else:
                p = jnp.exp2(s - shift_ref[0])
                part = p[:, 0:LANES]
                for c in range(1, nk // LANES):
                    part = part + p[:, c * LANES:(c + 1) * LANES]
                l[rows, :] += part
                a[rows, :] += jnp.dot(p.astype(BF16), v, preferred_element_type=F32)

    def body(i, carry):
        block(pl.multiple_of(i * ATTN_KEYS_PER_TRIP, ATTN_KEYS_PER_TRIP), ATTN_KEYS_PER_TRIP, False)
        return carry

    lax.fori_loop(0, qi * (tq // ATTN_KEYS_PER_TRIP), body, 0)
    for d in range(tq // ATTN_KEY_BLOCK):
        block(pl.multiple_of(qi * tq + d * ATTN_KEY_BLOCK, ATTN_KEY_BLOCK), ATTN_KEY_BLOCK, True, d * ATTN_KEY_BLOCK)

    lam = _lambda(lq1_ref, lk1_ref, lq2_ref, lk2_ref, lam_init)
    d1 = jnp.sum(l1[...], axis=-1, keepdims=True)
    d2 = jnp.sum(l2[...], axis=-1, keepdims=True)
    o_ref[0] = _sub_ln(a1[...] / d1, a2[...] / d2, lam, sg_ref, lam_init).astype(BF16)


def _attn_prompt(shift, q, k, v, lw, lam_init, tq, online):
    b, s, _ = q.shape
    grid = (b, N_HEADS, s // tq)
    qspec = pl.BlockSpec((1, tq, LANES), lambda i, h, j: (i, j, h))
    kvspec = pl.BlockSpec((1, s, LANES), lambda i, h, j: (i, 0, h))
    vec = _const_spec((1, HEAD_DIM))
    lw_width = 1 if online else LANES
    return pl.pallas_call(
        functools.partial(_attn_prompt_kernel, lam_init=lam_init, online=online),
        grid=grid,
        in_specs=[pl.BlockSpec(memory_space=pltpu.SMEM), qspec, kvspec, kvspec, vec, vec, vec, vec,
                  _const_spec((1, V_DIM))],
        out_specs=qspec,
        out_shape=jax.ShapeDtypeStruct((b, s, ATTN_DIM), BF16),
        scratch_shapes=[pltpu.VMEM((tq, LANES), BF16), pltpu.VMEM((tq, LANES), BF16),
                        pltpu.VMEM((tq, 1), F32), pltpu.VMEM((tq, lw_width), F32), pltpu.VMEM((tq, V_DIM), F32),
                        pltpu.VMEM((tq, 1), F32), pltpu.VMEM((tq, lw_width), F32), pltpu.VMEM((tq, V_DIM), F32)],
        compiler_params=pltpu.CompilerParams(dimension_semantics=("arbitrary", "arbitrary", "arbitrary"),
                                             vmem_limit_bytes=VMEM_LIMIT),
        name="attn_prompt_online" if online else "attn_prompt",
    )(shift, q, k, v, lw["lq1"], lw["lk1"], lw["lq2"], lw["lk2"], lw["subln_g"])


def _attn_sample_kernel(q_ref, kc_ref, vc_ref, kn_ref, vn_ref, lq1_ref, lk1_ref, lq2_ref, lk2_ref, sg_ref, o_ref,
                        *, lam_init):
    q1, q2 = _split_maps(q_ref[0])
    kc = kc_ref[0].astype(BF16)
    past = vc_ref.shape[1] // N_HEADS
    vc = vc_ref[0, pl.ds(pl.program_id(1), past, stride=N_HEADS), :].astype(BF16)
    kn = kn_ref[0]
    vn = vn_ref[0]
    outs = []
    for qm in (q1, q2):
        sc = _scores(qm, kc)
        sn = _scores(qm, kn)
        m = jnp.maximum(jnp.max(sc, axis=-1, keepdims=True), jnp.max(sn, axis=-1, keepdims=True))
        pc = jnp.exp2(sc - m)
        pn = jnp.exp2(sn - m)
        l = jnp.sum(pc, axis=-1, keepdims=True) + jnp.sum(pn, axis=-1, keepdims=True)
        acc = (jnp.dot(pc.astype(BF16), vc, preferred_element_type=F32)
               + jnp.dot(pn.astype(BF16), vn, preferred_element_type=F32))
        outs.append(acc / l)
    lam = _lambda(lq1_ref, lk1_ref, lq2_ref, lk2_ref, lam_init)
    o_ref[0] = _sub_ln(outs[0], outs[1], lam, sg_ref, lam_init).astype(BF16)


def _attn_sample(q, k_cache, v_cache, k_new, v_new, lw, lam_init):
    b, t, _ = q.shape
    past = k_cache.shape[1]
    new = pl.BlockSpec((1, t, LANES), lambda i, h: (i, 0, h))
    cache = pl.BlockSpec((1, past, LANES), lambda i, h: (i, 0, h))
    cache_v = pl.BlockSpec((1, past * N_HEADS, V_DIM), lambda i, h: (i, 0, 0))
    vec = _const_spec((1, HEAD_DIM))
    return pl.pallas_call(
        functools.partial(_attn_sample_kernel, lam_init=lam_init),
        grid=(b, N_HEADS),
        in_specs=[new, cache, cache_v, new, new, vec, vec, vec, vec, _const_spec((1, V_DIM))],
        out_specs=new,
        out_shape=jax.ShapeDtypeStruct((b, t, ATTN_DIM), BF16),
        compiler_params=pltpu.CompilerParams(dimension_semantics=("arbitrary", "arbitrary"),
                                             vmem_limit_bytes=VMEM_LIMIT),
        name="attn_sample",
    )(q, k_cache, v_cache, k_new, v_new, lw["lq1"], lw["lk1"], lw["lq2"], lw["lk2"], lw["subln_g"])


def _post_kernel(attn_ref, p_ref, g_ref, x_ref, wao_ref, wo_ref, g2_ref, wrt_ref, brt_ref, xm_ref, *out_refs, routed):
    tm = x_ref.shape[0]
    ao = jnp.dot(attn_ref[...], wao_ref[...], preferred_element_type=F32)
    merged = p_ref[...].astype(F32) + g_ref[...].astype(F32) * ao
    xm = x_ref[...] + jnp.dot(merged.astype(BF16), wo_ref[...], preferred_element_type=F32)
    xm_ref[...] = xm
    ms = jnp.mean(xm * xm, axis=-1, keepdims=True)
    h2 = xm * lax.rsqrt(ms + EPS) * g2_ref[...]

    h_hi = h2.astype(BF16)
    h_lo = (h2 - h_hi.astype(F32)).astype(BF16)
    r = (jnp.dot(h_hi, wrt_ref[...], preferred_element_type=F32)
         + jnp.dot(h_lo, wrt_ref[...], preferred_element_type=F32))
    lg = r[:, :LANES] + r[:, LANES:] + brt_ref[...]
    lane = lax.broadcasted_iota(jnp.int32, lg.shape, 1)
    neg = jnp.full_like(lg, -jnp.inf)
    big = jnp.full_like(lane, LANES)

    def first_max(z):
        zmax = jnp.max(z, axis=-1, keepdims=True)
        return zmax, jnp.min(jnp.where(z == zmax, lane, big), axis=-1, keepdims=True)

    is_group = lane < N_GROUPS
    gmax, gi = first_max(jnp.where(is_group, lg, neg))
    g_w = 1.0 / jnp.sum(jnp.where(is_group, jnp.exp(lg - gmax), 0.0), axis=-1, keepdims=True)
    lo = N_GROUPS + EXPERTS_PER_GROUP * gi
    el = jnp.where((lane >= lo) & (lane < lo + EXPERTS_PER_GROUP), lg, neg)
    v1, i1 = first_max(el)
    v2, i2 = first_max(jnp.where(lane == i1, neg, el))
    e2 = jnp.exp(v2 - v1)
    w1 = g_w / (1.0 + e2)
    w2 = g_w * e2 / (1.0 + e2)
    comb = jnp.where(lane == i1 - N_GROUPS, w1, 0.0) + jnp.where(lane == i2 - N_GROUPS, w2, 0.0)
    if routed:
        rows_ref, gid_ref = out_refs
        _store_tokens(rows_ref, jnp.concatenate([h2, comb], axis=1), ROW_SPAN)
        gid = jnp.where(lane == 0, gi.astype(F32), 0.0).astype(BF16)
        pick = (lax.broadcasted_iota(jnp.int32, (8, LANES), 1) == 0).astype(BF16)
        gid_ref[0] = lax.dot_general(pick, gid, (((1,), (1,)), ((), ())), preferred_element_type=F32)
    else:
        h2_ref, comb_ref = out_refs
        h2_ref[...] = h_hi
        comb_ref[...] = comb


def _post(attn, p, g, x, lw, tm, routed):
    t = x.shape[0]
    tok = lambda width: pl.BlockSpec((tm, width), lambda i: (i, 0))
    if routed:
        out_specs = (tok(D_MODEL), pl.BlockSpec((tm * ROW_SPAN, LANES), lambda i: (i, 0)),
                     pl.BlockSpec((1, 8, tm), lambda i: (i, 0, 0)))
        out_shape = (jax.ShapeDtypeStruct((t, D_MODEL), F32), jax.ShapeDtypeStruct((t * ROW_SPAN, LANES), F32),
                     jax.ShapeDtypeStruct((t // tm, 8, tm), F32))
    else:
        out_specs = (tok(D_MODEL), tok(D_MODEL), tok(LANES))
        out_shape = (jax.ShapeDtypeStruct((t, D_MODEL), F32), jax.ShapeDtypeStruct((t, D_MODEL), BF16),
                     jax.ShapeDtypeStruct((t, LANES), F32))
    return pl.pallas_call(
        functools.partial(_post_kernel, routed=routed),
        grid=(t // tm,),
        in_specs=[tok(ATTN_DIM), tok(D_MODEL), tok(D_MODEL), tok(D_MODEL),
                  _const_spec((ATTN_DIM, D_MODEL)), _const_spec((D_MODEL, D_MODEL)), _const_spec((1, D_MODEL)),
                  _const_spec((D_MODEL, 2 * LANES)), _const_spec((1, LANES))],
        out_specs=out_specs,
        out_shape=out_shape,
        compiler_params=pltpu.CompilerParams(dimension_semantics=("arbitrary",), vmem_limit_bytes=VMEM_LIMIT),
        name="post_routed" if routed else "post",
    )(attn, p, g, x, lw["w_attn_out"], lw["w_o"], lw["g2"], lw["w_rt"], lw["b_rt"])


def _token_rows(first_token, n, c, span):
    return pl.ds(first_token * span + c, n, stride=span)


def _load_tokens(ref, first_token, n, tiles, span):
    return [ref[_token_rows(first_token, n, c, span), :] for c in range(tiles)]


def _store_tokens(ref, value, span):
    n, width = value.shape
    for c in range(span):
        tile = value[:, c * LANES:(c + 1) * LANES] if c * LANES < width else jnp.zeros((n, LANES), value.dtype)
        ref[_token_rows(0, n, c, span), :] = tile


def _token_gather(idx_ref, tokens_hbm, buf, sem, base, n, span):
    def start():
        def one(r, carry):
            src = pl.ds(pl.multiple_of(idx_ref[0, 0, r] * span, span), span)
            dst = pl.ds(pl.multiple_of((base + r) * span, span), span)
            pltpu.make_async_copy(tokens_hbm.at[src, :], buf.at[dst, :], sem).start()
            return carry
        lax.fori_loop(0, n, one, 0, unroll=8)

    def wait():
        dst = pl.ds(pl.multiple_of(base * span, span), n * span)
        pltpu.make_async_copy(tokens_hbm.at[pl.ds(0, n * span), :], buf.at[dst, :], sem).wait()

    return start, wait


def _gathered_tokens(idx_ref, idx_next_ref, tokens_hbm, buf, sems, n, tiles, span):
    j = pl.program_id(0)
    slot = j % 2
    start_cur, wait_cur = _token_gather(idx_ref, tokens_hbm, buf, sems.at[slot], slot * n, n, span)
    start_next, _ = _token_gather(idx_next_ref, tokens_hbm, buf, sems.at[1 - slot], (1 - slot) * n, n, span)

    @pl.when(j == 0)
    def _():
        start_cur()

    @pl.when(j + 1 < pl.num_programs(0))
    def _():
        start_next()

    wait_cur()
    return _load_tokens(buf, slot * n, n, tiles, span)


def _idx_specs(n_steps, n):
    cur = lambda j, *_: (j, 0, 0)
    nxt = lambda j, *_: (jnp.minimum(j + 1, n_steps - 1), 0, 0)
    return [pl.BlockSpec((1, 1, n), cur, memory_space=pltpu.SMEM),
            pl.BlockSpec((1, 1, n), nxt, memory_space=pltpu.SMEM)]


def _moe_routed_kernel(group_ref, src_ref, src_next_ref, rows_hbm, wg_ref, wu_ref, wd_ref, out_ref, buf, sems):
    tile = out_ref.shape[0] // OUT_SPAN
    g = group_ref[pl.program_id(0)]
    rows = _gathered_tokens(src_ref, src_next_ref, rows_hbm, buf, sems, tile, ROW_TILES, ROW_SPAN)
    t = jnp.concatenate([x.astype(BF16) for x in rows[:OUT_SPAN]], axis=1)
    comb = rows[OUT_SPAN]
    lane = lax.broadcasted_iota(jnp.int32, comb.shape, 1)
    out = jnp.zeros((tile, D_MODEL), F32)
    for e in range(EXPERTS_PER_GROUP):
        gate = jnp.dot(t, wg_ref[0, e], preferred_element_type=F32)
        up = jnp.dot(t, wu_ref[0, e], preferred_element_type=F32)
        he = (gate * _sigmoid(gate) * up).astype(BF16)
        d = jnp.dot(he, wd_ref[0, e], preferred_element_type=F32)
        c = jnp.sum(jnp.where(lane == g * EXPERTS_PER_GROUP + e, comb, 0.0), axis=-1, keepdims=True)
        out = out + c * d
    _store_tokens(out_ref, out, OUT_SPAN)


def _moe_routed(tile_group, src, rows, lw, tile):
    n_tiles = tile_group.shape[0]
    wspec = lambda a, b: pl.BlockSpec((1, EXPERTS_PER_GROUP, a, b), lambda j, grp: (grp[j], 0, 0, 0))
    return pl.pallas_call(
        _moe_routed_kernel,
        grid_spec=pltpu.PrefetchScalarGridSpec(
            num_scalar_prefetch=1,
            grid=(n_tiles,),
            in_specs=_idx_specs(n_tiles, tile) + [
                pl.BlockSpec(memory_space=pl.ANY),
                wspec(D_MODEL, EXPERT_HIDDEN), wspec(D_MODEL, EXPERT_HIDDEN), wspec(EXPERT_HIDDEN, D_MODEL)],
            out_specs=pl.BlockSpec((tile * OUT_SPAN, LANES), lambda j, grp: (j, 0)),
            scratch_shapes=[pltpu.VMEM((2 * tile * ROW_SPAN, LANES), F32), pltpu.SemaphoreType.DMA((2,))],
        ),
        out_shape=jax.ShapeDtypeStruct((n_tiles * tile * OUT_SPAN, LANES), F32),
        compiler_params=pltpu.CompilerParams(dimension_semantics=("arbitrary",), vmem_limit_bytes=VMEM_LIMIT),
        name="moe_routed",
    )(tile_group, src, src, rows, lw["w_gate"], lw["w_up"], lw["w_down"])


def _combine_kernel(dest_ref, dest_next_ref, xm_ref, sorted_hbm, y_ref, buf, sems):
    tm = xm_ref.shape[0]
    moe = _gathered_tokens(dest_ref, dest_next_ref, sorted_hbm, buf, sems, tm, OUT_SPAN, OUT_SPAN)
    y_ref[...] = xm_ref[...] + jnp.concatenate(moe, axis=1)


def _combine(dest, xm, moe_sorted, tm):
    t = xm.shape[0]
    n = t // tm
    idx = dest.reshape(n, 1, tm)
    return pl.pallas_call(
        _combine_kernel,
        grid=(n,),
        in_specs=_idx_specs(n, tm) + [pl.BlockSpec((tm, D_MODEL), lambda j: (j, 0)),
                                      pl.BlockSpec(memory_space=pl.ANY)],
        out_specs=pl.BlockSpec((tm, D_MODEL), lambda j: (j, 0)),
        out_shape=jax.ShapeDtypeStruct((t, D_MODEL), F32),
        scratch_shapes=[pltpu.VMEM((2 * tm * OUT_SPAN, LANES), F32), pltpu.SemaphoreType.DMA((2,))],
        compiler_params=pltpu.CompilerParams(dimension_semantics=("arbitrary",), vmem_limit_bytes=VMEM_LIMIT),
        name="moe_combine",
    )(idx, idx, xm, moe_sorted)


def _routing_tables(group_id, tile):
    t = group_id.shape[0]
    n_tiles = t // tile + N_GROUPS
    onehot = (group_id[:, None] == jnp.arange(N_GROUPS, dtype=jnp.int32)[None, :]).astype(jnp.int32)
    csum = jnp.cumsum(onehot, axis=0)
    rank = jnp.sum(onehot * csum, axis=1) - 1
    tiles_per_group = (csum[-1] + tile - 1) // tile
    tile_end = jnp.cumsum(tiles_per_group)
    offset = (tile_end - tiles_per_group) * tile
    dest = jnp.sum(onehot * offset[None, :], axis=1) + rank
    src = jnp.zeros((n_tiles * tile,), jnp.int32).at[dest].set(jnp.arange(t, dtype=jnp.int32))
    tile_group = jnp.sum(jnp.arange(n_tiles, dtype=jnp.int32)[:, None] >= tile_end[None, :], axis=1)
    tile_group = jnp.minimum(tile_group, N_GROUPS - 1).astype(jnp.int32)
    return dest.astype(jnp.int32), src.reshape(n_tiles, 1, tile), tile_group


def _moe_kernel(h2_ref, comb_ref, xm_ref, wg_ref, wu_ref, wd_ref, y_ref):
    e = pl.program_id(1)

    @pl.when(e == 0)
    def _():
        y_ref[...] = xm_ref[...]

    t = h2_ref[...]
    gate = jnp.dot(t, wg_ref[0], preferred_element_type=F32)
    up = jnp.dot(t, wu_ref[0], preferred_element_type=F32)
    he = (gate * _sigmoid(gate) * up).astype(BF16)
    d = jnp.dot(he, wd_ref[0], preferred_element_type=F32)
    comb = comb_ref[...]
    lane = lax.broadcasted_iota(jnp.int32, comb.shape, 1)
    c = jnp.sum(jnp.where(lane == e, comb, 0.0), axis=-1, keepdims=True)
    y_ref[...] += c * d


def _moe(h2, comb, xm, lw, tm):
    t = h2.shape[0]
    tok = lambda width: pl.BlockSpec((tm, width), lambda i, e: (i, 0))
    return pl.pallas_call(
        _moe_kernel,
        grid=(t // tm, N_EXPERTS),
        in_specs=[tok(D_MODEL), tok(LANES), tok(D_MODEL),
                  pl.BlockSpec((1, D_MODEL, EXPERT_HIDDEN), lambda i, e: (e, 0, 0)),
                  pl.BlockSpec((1, D_MODEL, EXPERT_HIDDEN), lambda i, e: (e, 0, 0)),
                  pl.BlockSpec((1, EXPERT_HIDDEN, D_MODEL), lambda i, e: (e, 0, 0))],
        out_specs=tok(D_MODEL),
        out_shape=jax.ShapeDtypeStruct((t, D_MODEL), F32),
        compiler_params=pltpu.CompilerParams(dimension_semantics=("arbitrary", "arbitrary"),
                                             vmem_limit_bytes=VMEM_LIMIT),
        name="moe",
    )(h2, comb, xm, *(lw[n].reshape(N_EXPERTS, *lw[n].shape[2:]) for n in ("w_gate", "w_up", "w_down")))


def _hi_lo_columns(w):
    hi = w.astype(BF16)
    return jnp.concatenate([hi, (w - hi.astype(F32)).astype(BF16)], axis=1)


def _layer_weights(l, norm1_g, w_in, conv_dw, conv_db, conv_ln_g, conv_ln_b, w_conv_out, q_norm_g, k_norm_g,
                   lambda_q1, lambda_k1, lambda_q2, lambda_k2, subln_g, w_attn_out, w_o, norm2_g, w_group, b_group,
                   w_router, b_router, w_gate, w_up, w_down):
    row = lambda a: a.reshape(1, -1).astype(F32)
    grouped = lambda w: w.astype(BF16).reshape(N_GROUPS, EXPERTS_PER_GROUP, *w.shape[1:])
    head_gain = lambda gain: jnp.tile(gain.reshape(-1), N_HEADS).reshape(1, QK_DIM).astype(F32)
    grp = jnp.arange(QK_DIM) // HEAD_DIM
    w_rt = jnp.concatenate([w_group[l], jnp.moveaxis(w_router[l], 0, 1).reshape(D_MODEL, N_EXPERTS)], axis=1)
    b_rt = jnp.concatenate([b_group[l], b_router[l].reshape(-1)])
    pad = LANES - N_GROUPS - N_EXPERTS
    return {
        "g1": row(norm1_g[l]), "w_in": w_in[l].astype(BF16), "conv_dw": conv_dw[l].astype(F32),
        "conv_db": row(conv_db[l]), "conv_ln_g": row(conv_ln_g[l]), "conv_ln_b": row(conv_ln_b[l]),
        "w_conv_out": w_conv_out[l].astype(BF16), "gq": head_gain(q_norm_g[l]), "gk": head_gain(k_norm_g[l]),
        "bd": (grp[:, None] == grp[None, :]).astype(BF16),
        "lq1": row(lambda_q1[l]), "lk1": row(lambda_k1[l]), "lq2": row(lambda_q2[l]), "lk2": row(lambda_k2[l]),
        "subln_g": row(subln_g[l]), "w_attn_out": w_attn_out[l].astype(BF16), "w_o": w_o[l].astype(BF16),
        "g2": row(norm2_g[l]),
        "w_rt": _hi_lo_columns(jnp.pad(w_rt.astype(F32), ((0, 0), (0, pad)))),
        "b_rt": jnp.pad(b_rt.astype(F32), (0, pad)).reshape(1, LANES),
        "w_gate": grouped(w_gate[l]), "w_up": grouped(w_up[l]), "w_down": grouped(w_down[l]),
    }


def _tail_dense(x, attn, p, g, lw, tm):
    b, s, _ = x.shape
    flat = lambda a: a.reshape(b * s, a.shape[-1])
    xm, h2, comb = _post(flat(attn), flat(p), flat(g), flat(x), lw, tm, False)
    return _moe(h2, comb, xm, lw, tm).reshape(b, s, D_MODEL)


def _tail_routed(x, attn, p, g, lw, tm, tile):
    b, s, _ = x.shape
    flat = lambda a: a.reshape(b * s, a.shape[-1])
    xm, rows, gid = _post(flat(attn), flat(p), flat(g), flat(x), lw, tm, True)
    dest, src, tile_group = _routing_tables(gid[:, 0, :].reshape(-1).astype(jnp.int32), tile)
    moe_sorted = _moe_routed(tile_group, src, rows, lw, tile)
    return _combine(dest, xm, moe_sorted, tm).reshape(b, s, D_MODEL)


def kernel(x_prompt, x_sample, cache_k, cache_v, state_conv, norm1_g, w_in, conv_dw, conv_db, conv_ln_g, conv_ln_b, w_conv_out, q_norm_g, k_norm_g, lambda_q1, lambda_k1, lambda_q2, lambda_k2, subln_g, w_attn_out, w_o, norm2_g, w_group, b_group, w_router, b_router, w_gate, w_up, w_down):
    depth = w_in.shape[0]
    bp, sp, _ = x_prompt.shape
    bs, ss, _ = x_sample.shape
    past = cache_k.shape[2]
    xp, xs = x_prompt, x_sample
    kp_l, vp_l, cp_l, ks_l, vs_l, cs_l = [], [], [], [], [], []
    for l in range(depth):
        lw = _layer_weights(l, norm1_g, w_in, conv_dw, conv_db, conv_ln_g, conv_ln_b, w_conv_out, q_norm_g,
                            k_norm_g, lambda_q1, lambda_k1, lambda_q2, lambda_k2, subln_g, w_attn_out, w_o,
                            norm2_g, w_group, b_group, w_router, b_router, w_gate, w_up, w_down)
        lam_init = 0.8 - 0.6 * math.exp(-0.3 * l)

        hist_p = jnp.zeros((bp, HIST_ROWS, CONV_DIM), F32)
        q, kf, kb, vf, vb, p, g, cs = _in_proj(xp, hist_p, lw, 512)
        bound = (HEAD_DIM * SCALE) * jnp.max(jnp.abs(lw["gq"])) * jnp.max(jnp.abs(lw["gk"]))
        shift = (bound * LOG2E).reshape(1)
        attn = lax.cond(bound <= MAX_FIXED_SHIFT,
                        lambda *a: _attn_prompt(*a, lw, lam_init, ATTN_QUERY_BLOCK, False),
                        lambda *a: _attn_prompt(*a, lw, lam_init, ATTN_QUERY_BLOCK, True),
                        shift, q, kb, vb)
        xp = _tail_routed(xp, attn, p, g, lw, 512, MOE_TILE)
        kp_l.append(kf.reshape(bp, sp, N_HEADS, 2, HEAD_DIM))
        vp_l.append(vf.reshape(bp, sp, N_HEADS, V_DIM))
        cp_l.append(cs[:, HIST_PAD:, :])

        hist_s = jnp.pad(state_conv[l].astype(F32), ((0, 0), (HIST_PAD, 0), (0, 0)))
        q, kf, kb, vf, vb, p, g, cs = _in_proj(xs, hist_s, lw, ss)
        attn = _attn_sample(q, cache_k[l].astype(BF16).reshape(bs, past, QK_DIM),
                            cache_v[l].reshape(bs, past * N_HEADS, V_DIM),
                            kb, vb, lw, lam_init)
        xs = _tail_dense(xs, attn, p, g, lw, bs * ss)
        ks_l.append(kf.reshape(bs, ss, N_HEADS, 2, HEAD_DIM))
        vs_l.append(vf.reshape(bs, ss, N_HEADS, V_DIM))
        cs_l.append(cs[:, HIST_PAD:, :])
    return (xp, xs, jnp.stack(kp_l), jnp.stack(vp_l), jnp.stack(cp_l),
            jnp.stack(ks_l), jnp.stack(vs_l), jnp.stack(cs_l))
```

```python
import functools
import math

import jax
import jax.numpy as jnp
from jax import lax
from jax.experimental import pallas as pl
from jax.experimental.pallas import tpu as pltpu

D_MODEL = 1024
CHUNK = 64
CONV_DIM = 512
CONV_TAPS = 31
CONV_STATE = CONV_TAPS - 1
N_HEADS = 4
HEAD_DIM = 64
V_DIM = 2 * HEAD_DIM
ATTN_DIM = N_HEADS * V_DIM
QK_DIM = N_HEADS * 2 * HEAD_DIM
N_GROUPS = 4
EXPERTS_PER_GROUP = 4
N_EXPERTS = N_GROUPS * EXPERTS_PER_GROUP
TOP_K_INNER = 2
EXPERT_HIDDEN = 512
EPS = 1e-6
SCALE = HEAD_DIM ** -0.5
LOG2E = math.log2(math.e)
Q_SCALE = SCALE * LOG2E
MAX_FIXED_SHIFT = 40.0
COL_GLU = 2 * CONV_DIM
COL_Q = COL_GLU
COL_K = COL_Q + QK_DIM
COL_V = COL_K + QK_DIM
COL_GC = COL_V + ATTN_DIM
COL_GA = COL_GC + D_MODEL
IN_COLS = COL_GA + D_MODEL

LANES = 128
HIST_ROWS = 32
HIST_PAD = HIST_ROWS - CONV_STATE
CONV_ROW_CHUNK = 64
VMEM_LIMIT = 56 * 1024 * 1024
OUT_SPAN = D_MODEL // LANES
ROW_TILES = OUT_SPAN + 1
ROW_SPAN = 16
MOE_TILE = 512
ATTN_QUERY_BLOCK = 2048
ATTN_KEY_BLOCK = 256
ATTN_KEYS_PER_TRIP = 1024

BF16 = jnp.bfloat16
F32 = jnp.float32


def _sigmoid(x):
    return 0.5 * jnp.tanh(0.5 * x) + 0.5


def _const_spec(shape):
    n = len(shape)
    return pl.BlockSpec(shape, lambda *_: (0,) * n)


def _in_proj_kernel(x_ref, hist_ref, g1_ref, w_in_ref, dw_ref, db_ref, lng_ref, lnb_ref, wco_ref, gq_ref, gk_ref,
                    bd_ref, q_ref, kf_ref, kb_ref, vf_ref, vb_ref, p_ref, g_ref, cs_ref, cbuf, ybuf):
    t = pl.program_id(1)
    tm = x_ref.shape[1]
    x = x_ref[0]
    ms = jnp.mean(x * x, axis=-1, keepdims=True)
    h = (x * lax.rsqrt(ms + EPS) * g1_ref[...]).astype(BF16)

    def proj(lo, hi):
        return jnp.dot(h, w_in_ref[:, lo:hi], preferred_element_type=F32)

    u = proj(0, CONV_DIM) * _sigmoid(proj(CONV_DIM, COL_GLU))

    @pl.when(t == 0)
    def _():
        cbuf[0:HIST_ROWS, :] = hist_ref[0]

    @pl.when(t > 0)
    def _():
        cbuf[0:HIST_ROWS, :] = cbuf[tm:tm + HIST_ROWS, :]

    cbuf[HIST_ROWS:HIST_ROWS + tm, :] = u
    cs_ref[0] = cbuf[tm:tm + HIST_ROWS, :]

    rc = min(CONV_ROW_CHUNK, tm)
    for c in range(tm // rc):
        acc = None
        for res in range(8):
            rows = rc if res == 0 else rc + 8
            z = None
            for off in range(res, CONV_TAPS + HIST_PAD, 8):
                j = off - HIST_PAD
                if j < 0:
                    continue
                lo = c * rc + off - res
                term = dw_ref[j:j + 1, :] * cbuf[lo:lo + rows, :]
                z = term if z is None else z + term
            z = z[res:res + rc, :]
            acc = z if acc is None else acc + z
        y = acc + db_ref[...]
        mu = jnp.mean(y, axis=-1, keepdims=True)
        yc = y - mu
        var = jnp.mean(yc * yc, axis=-1, keepdims=True)
        yn = yc * lax.rsqrt(var + EPS) * lng_ref[...] + lnb_ref[...]
        ybuf[c * rc:(c + 1) * rc, :] = (yn * _sigmoid(yn)).astype(BF16)
    conv_out = jnp.dot(ybuf[...], wco_ref[...], preferred_element_type=F32)
    p_ref[0] = (_sigmoid(proj(COL_GC, COL_GA)) * conv_out).astype(BF16)
    g_ref[0] = _sigmoid(proj(COL_GA, IN_COLS)).astype(BF16)

    def head_norm(z, gain_ref):
        ss = jnp.dot((z * z).astype(BF16), bd_ref[...], preferred_element_type=F32) * (1.0 / HEAD_DIM)
        return z * lax.rsqrt(ss + EPS) * gain_ref[...]

    q_ref[0] = (head_norm(proj(COL_Q, COL_K), gq_ref) * Q_SCALE).astype(BF16)
    kn = head_norm(proj(COL_K, COL_V), gk_ref)
    kb_ref[0] = kn.astype(BF16)
    v = proj(COL_V, COL_GC)
    vb_ref[0] = v.astype(BF16)
    n_maps = QK_DIM // HEAD_DIM
    for c in range(n_maps):
        kf_ref[0, pl.ds(c, tm, stride=n_maps), :] = kn[:, c * HEAD_DIM:(c + 1) * HEAD_DIM]
    for hd in range(N_HEADS):
        vf_ref[0, pl.ds(hd, tm, stride=N_HEADS), :] = v[:, hd * V_DIM:(hd + 1) * V_DIM]


def _in_proj(x, hist, lw, tm):
    b, s, _ = x.shape
    grid = (b, s // tm)
    tok = lambda width: pl.BlockSpec((1, tm, width), lambda i, j: (i, j, 0))
    out_shape = (
        jax.ShapeDtypeStruct((b, s, QK_DIM), BF16),
        jax.ShapeDtypeStruct((b, s * QK_DIM // HEAD_DIM, HEAD_DIM), F32),
        jax.ShapeDtypeStruct((b, s, QK_DIM), BF16),
        jax.ShapeDtypeStruct((b, s * N_HEADS, V_DIM), F32),
        jax.ShapeDtypeStruct((b, s, ATTN_DIM), BF16),
        jax.ShapeDtypeStruct((b, s, D_MODEL), BF16),
        jax.ShapeDtypeStruct((b, s, D_MODEL), BF16),
        jax.ShapeDtypeStruct((b, HIST_ROWS, CONV_DIM), F32),
    )
    return pl.pallas_call(
        _in_proj_kernel,
        grid=grid,
        in_specs=[
            tok(D_MODEL),
            pl.BlockSpec((1, HIST_ROWS, CONV_DIM), lambda i, j: (i, 0, 0)),
            _const_spec((1, D_MODEL)),
            _const_spec((D_MODEL, IN_COLS)),
            _const_spec((CONV_TAPS, CONV_DIM)),
            _const_spec((1, CONV_DIM)),
            _const_spec((1, CONV_DIM)),
            _const_spec((1, CONV_DIM)),
            _const_spec((CONV_DIM, D_MODEL)),
            _const_spec((1, QK_DIM)),
            _const_spec((1, QK_DIM)),
            _const_spec((QK_DIM, QK_DIM)),
        ],
        out_specs=(tok(QK_DIM), pl.BlockSpec((1, tm * QK_DIM // HEAD_DIM, HEAD_DIM), lambda i, j: (i, j, 0)),
                   tok(QK_DIM), pl.BlockSpec((1, tm * N_HEADS, V_DIM), lambda i, j: (i, j, 0)),
                   tok(ATTN_DIM), tok(D_MODEL), tok(D_MODEL),
                   pl.BlockSpec((1, HIST_ROWS, CONV_DIM), lambda i, j: (i, 0, 0))),
        out_shape=out_shape,
        scratch_shapes=[pltpu.VMEM((HIST_ROWS + tm, CONV_DIM), F32), pltpu.VMEM((tm, CONV_DIM), BF16)],
        compiler_params=pltpu.CompilerParams(dimension_semantics=("arbitrary", "arbitrary"),
                                             vmem_limit_bytes=VMEM_LIMIT),
        name="in_proj",
    )(x, hist, lw["g1"], lw["w_in"], lw["conv_dw"], lw["conv_db"], lw["conv_ln_g"], lw["conv_ln_b"],
      lw["w_conv_out"], lw["gq"], lw["gk"], lw["bd"])


def _lambda(lq1_ref, lk1_ref, lq2_ref, lk2_ref, lam_init):
    a = jnp.sum(lq1_ref[...] * lk1_ref[...], axis=-1, keepdims=True)
    b = jnp.sum(lq2_ref[...] * lk2_ref[...], axis=-1, keepdims=True)
    return jnp.exp(a) - jnp.exp(b) + lam_init


def _split_maps(q):
    lane = lax.broadcasted_iota(jnp.int32, q.shape, 1)
    zero = jnp.zeros_like(q)
    return jnp.where(lane < HEAD_DIM, q, zero), jnp.where(lane >= HEAD_DIM, q, zero)


def _scores(qm, k):
    return lax.dot_general(qm, k, (((1,), (1,)), ((), ())), preferred_element_type=F32)


def _sub_ln(o1, o2, lam, sg_ref, lam_init):
    o = o1 - lam * o2
    ms = jnp.mean(o * o, axis=-1, keepdims=True)
    return o * lax.rsqrt(ms + EPS) * sg_ref[...] * (1.0 - lam_init)


def _chunk_mask(s):
    row = lax.broadcasted_iota(jnp.int32, s.shape, 0) // CHUNK
    col = lax.broadcasted_iota(jnp.int32, s.shape, 1) // CHUNK
    return jnp.where(col <= row, s, -jnp.inf)


def _attn_prompt_kernel(shift_ref, q_ref, k_ref, v_ref, lq1_ref, lk1_ref, lq2_ref, lk2_ref, sg_ref, o_ref,
                        qs1, qs2, m1, l1, a1, m2, l2, a2, *, lam_init, online):
    qi = pl.program_id(2)
    tq = q_ref.shape[1]
    q1, q2 = _split_maps(q_ref[0])
    qs1[...] = q1
    qs2[...] = q2
    for m, l, a in ((m1, l1, a1), (m2, l2, a2)):
        m[...] = jnp.full(m.shape, -jnp.inf, F32)
        l[...] = jnp.zeros(l.shape, F32)
        a[...] = jnp.zeros(a.shape, F32)

    def block(kstart, nk, masked, r0=0):
        rows = slice(r0, tq)
        k = k_ref[0, pl.ds(kstart, nk), :]
        v = v_ref[0, pl.ds(kstart, nk), :]
        for qs, m, l, a in ((qs1, m1, l1, a1), (qs2, m2, l2, a2)):
            s = _scores(qs[rows, :], k)
            if masked:
                s = _chunk_mask(s)
            if online:
                m_prev = m[rows, :]
                m_new = jnp.maximum(m_prev, jnp.max(s, axis=-1, keepdims=True))
                alpha = jnp.exp2(m_prev - m_new)
                p = jnp.exp2(s - m_new)
                l[rows, :] = alpha * l[rows, :] + jnp.sum(p, axis=-1, keepdims=True)
                a[rows, :] = alpha * a[rows, :] + jnp.dot(p.astype(BF16), v, preferred_element_type=F32)
                m[rows, :] = m_new
            else:
                p = jnp.exp2(s - shift_ref[0])
                part = p[:, 0:LANES]
                for c in range(1, nk // LANES):
                    part = part + p[:, c * LANES:(c + 1) * LANES]
                l[rows, :] += part
                a[rows, :] += jnp.dot(p.astype(BF16), v, preferred_element_type=F32)

    def body(i, carry):
        block(pl.multiple_of(i * ATTN_KEYS_PER_TRIP, ATTN_KEYS_PER_TRIP), ATTN_KEYS_PER_TRIP, False)
        return carry

    lax.fori_loop(0, qi * (tq // ATTN_KEYS_PER_TRIP), body, 0)
    for d in range(tq // ATTN_KEY_BLOCK):
        block(pl.multiple_of(qi * tq + d * ATTN_KEY_BLOCK, ATTN_KEY_BLOCK), ATTN_KEY_BLOCK, True, d * ATTN_KEY_BLOCK)

    lam = _lambda(lq1_ref, lk1_ref, lq2_ref, lk2_ref, lam_init)
    d1 = jnp.sum(l1[...], axis=-1, keepdims=True)
    d2 = jnp.sum(l2[...], axis=-1, keepdims=True)
    o_ref[0] = _sub_ln(a1[...] / d1, a2[...] / d2, lam, sg_ref, lam_init).astype(BF16)


def _attn_prompt(shift, q, k, v, lw, lam_init, tq, online):
    b, s, _ = q.shape
    grid = (b, N_HEADS, s // tq)
    qspec = pl.BlockSpec((1, tq, LANES), lambda i, h, j: (i, j, h))
    kvspec = pl.BlockSpec((1, s, LANES), lambda i, h, j: (i, 0, h))
    vec = _const_spec((1, HEAD_DIM))
    lw_width = 1 if online else LANES
    return pl.pallas_call(
        functools.partial(_attn_prompt_kernel, lam_init=lam_init, online=online),
        grid=grid,
        in_specs=[pl.BlockSpec(memory_space=pltpu.SMEM), qspec, kvspec, kvspec, vec, vec, vec, vec,
                  _const_spec((1, V_DIM))],
        out_specs=qspec,
        out_shape=jax.ShapeDtypeStruct((b, s, ATTN_DIM), BF16),
        scratch_shapes=[pltpu.VMEM((tq, LANES), BF16), pltpu.VMEM((tq, LANES), BF16),
                        pltpu.VMEM((tq, 1), F32), pltpu.VMEM((tq, lw_width), F32), pltpu.VMEM((tq, V_DIM), F32),
                        pltpu.VMEM((tq, 1), F32), pltpu.VMEM((tq, lw_width), F32), pltpu.VMEM((tq, V_DIM), F32)],
        compiler_params=pltpu.CompilerParams(dimension_semantics=("arbitrary", "arbitrary", "arbitrary"),
                                             vmem_limit_bytes=VMEM_LIMIT),
        name="attn_prompt_online" if online else "attn_prompt",
    )(shift, q, k, v, lw["lq1"], lw["lk1"], lw["lq2"], lw["lk2"], lw["subln_g"])


def _attn_sample_kernel(q_ref, kc_ref, vc_ref, kn_ref, vn_ref, lq1_ref, lk1_ref, lq2_ref, lk2_ref, sg_ref, o_ref,
                        *, lam_init):
    q1, q2 = _split_maps(q_ref[0])
    kc = kc_ref[0].astype(BF16)
    past = vc_ref.shape[1] // N_HEADS
    vc = vc_ref[0, pl.ds(pl.program_id(1), past, stride=N_HEADS), :].astype(BF16)
    kn = kn_ref[0]
    vn = vn_ref[0]
    outs = []
    for qm in (q1, q2):
        sc = _scores(qm, kc)
        sn = _scores(qm, kn)
        m = jnp.maximum(jnp.max(sc, axis=-1, keepdims=True), jnp.max(sn, axis=-1, keepdims=True))
        pc = jnp.exp2(sc - m)
        pn = jnp.exp2(sn - m)
        l = jnp.sum(pc, axis=-1, keepdims=True) + jnp.sum(pn, axis=-1, keepdims=True)
        acc = (jnp.dot(pc.astype(BF16), vc, preferred_element_type=F32)
               + jnp.dot(pn.astype(BF16), vn, preferred_element_type=F32))
        outs.append(acc / l)
    lam = _lambda(lq1_ref, lk1_ref, lq2_ref, lk2_ref, lam_init)
    o_ref[0] = _sub_ln(outs[0], outs[1], lam, sg_ref, lam_init).astype(BF16)


def _attn_sample(q, k_cache, v_cache, k_new, v_new, lw, lam_init):
    b, t, _ = q.shape
    past = k_cache.shape[1]
    new = pl.BlockSpec((1, t, LANES), lambda i, h: (i, 0, h))
    cache = pl.BlockSpec((1, past, LANES), lambda i, h: (i, 0, h))
    cache_v = pl.BlockSpec((1, past * N_HEADS, V_DIM), lambda i, h: (i, 0, 0))
    vec = _const_spec((1, HEAD_DIM))
    return pl.pallas_call(
        functools.partial(_attn_sample_kernel, lam_init=lam_init),
        grid=(b, N_HEADS),
        in_specs=[new, cache, cache_v, new, new, vec, vec, vec, vec, _const_spec((1, V_DIM))],
        out_specs=new,
        out_shape=jax.ShapeDtypeStruct((b, t, ATTN_DIM), BF16),
        compiler_params=pltpu.CompilerParams(dimension_semantics=("arbitrary", "arbitrary"),
                                             vmem_limit_bytes=VMEM_LIMIT),
        name="attn_sample",
    )(q, k_cache, v_cache, k_new, v_new, lw["lq1"], lw["lk1"], lw["lq2"], lw["lk2"], lw["subln_g"])


def _post_kernel(attn_ref, p_ref, g_ref, x_ref, wao_ref, wo_ref, g2_ref, wrt_ref, brt_ref, xm_ref, *out_refs, routed):
    tm = x_ref.shape[0]
    ao = jnp.dot(attn_ref[...], wao_ref[...], preferred_element_type=F32)
    merged = p_ref[...].astype(F32) + g_ref[...].astype(F32) * ao
    xm = x_ref[...] + jnp.dot(merged.astype(BF16), wo_ref[...], preferred_element_type=F32)
    xm_ref[...] = xm
    ms = jnp.mean(xm * xm, axis=-1, keepdims=True)
    h2 = xm * lax.rsqrt(ms + EPS) * g2_ref[...]

    h_hi = h2.astype(BF16)
    h_lo = (h2 - h_hi.astype(F32)).astype(BF16)
    r = (jnp.dot(h_hi, wrt_ref[...], preferred_element_type=F32)
         + jnp.dot(h_lo, wrt_ref[...], preferred_element_type=F32))
    lg = r[:, :LANES] + r[:, LANES:] + brt_ref[...]
    lane = lax.broadcasted_iota(jnp.int32, lg.shape, 1)
    neg = jnp.full_like(lg, -jnp.inf)
    big = jnp.full_like(lane, LANES)

    def first_max(z):
        zmax = jnp.max(z, axis=-1, keepdims=True)
        return zmax, jnp.min(jnp.where(z == zmax, lane, big), axis=-1, keepdims=True)

    is_group = lane < N_GROUPS
    gmax, gi = first_max(jnp.where(is_group, lg, neg))
    g_w = 1.0 / jnp.sum(jnp.where(is_group, jnp.exp(lg - gmax), 0.0), axis=-1, keepdims=True)
    lo = N_GROUPS + EXPERTS_PER_GROUP * gi
    el = jnp.where((lane >= lo) & (lane < lo + EXPERTS_PER_GROUP), lg, neg)
    v1, i1 = first_max(el)
    v2, i2 = first_max(jnp.where(lane == i1, neg, el))
    e2 = jnp.exp(v2 - v1)
    w1 = g_w / (1.0 + e2)
    w2 = g_w * e2 / (1.0 + e2)
    comb = jnp.where(lane == i1 - N_GROUPS, w1, 0.0) + jnp.where(lane == i2 - N_GROUPS, w2, 0.0)
    if routed:
        rows_ref, gid_ref = out_refs
        _store_tokens(rows_ref, jnp.concatenate([h2, comb], axis=1), ROW_SPAN)
        gid = jnp.where(lane == 0, gi.astype(F32), 0.0).astype(BF16)
        pick = (lax.broadcasted_iota(jnp.int32, (8, LANES), 1) == 0).astype(BF16)
        gid_ref[0] = lax.dot_general(pick, gid, (((1,), (1,)), ((), ())), preferred_element_type=F32)
    else:
        h2_ref, comb_ref = out_refs
        h2_ref[...] = h_hi
        comb_ref[...] = comb


def _post(attn, p, g, x, lw, tm, routed):
    t = x.shape[0]
    tok = lambda width: pl.BlockSpec((tm, width), lambda i: (i, 0))
    if routed:
        out_specs = (tok(D_MODEL), pl.BlockSpec((tm * ROW_SPAN, LANES), lambda i: (i, 0)),
                     pl.BlockSpec((1, 8, tm), lambda i: (i, 0, 0)))
        out_shape = (jax.ShapeDtypeStruct((t, D_MODEL), F32), jax.ShapeDtypeStruct((t * ROW_SPAN, LANES), F32),
                     jax.ShapeDtypeStruct((t // tm, 8, tm), F32))
    else:
        out_specs = (tok(D_MODEL), tok(D_MODEL), tok(LANES))
        out_shape = (jax.ShapeDtypeStruct((t, D_MODEL), F32), jax.ShapeDtypeStruct((t, D_MODEL), BF16),
                     jax.ShapeDtypeStruct((t, LANES), F32))
    return pl.pallas_call(
        functools.partial(_post_kernel, routed=routed),
        grid=(t // tm,),
        in_specs=[tok(ATTN_DIM), tok(D_MODEL), tok(D_MODEL), tok(D_MODEL),
                  _const_spec((ATTN_DIM, D_MODEL)), _const_spec((D_MODEL, D_MODEL)), _const_spec((1, D_MODEL)),
                  _const_spec((D_MODEL, 2 * LANES)), _const_spec((1, LANES))],
        out_specs=out_specs,
        out_shape=out_shape,
        compiler_params=pltpu.CompilerParams(dimension_semantics=("arbitrary",), vmem_limit_bytes=VMEM_LIMIT),
        name="post_routed" if routed else "post",
    )(attn, p, g, x, lw["w_attn_out"], lw["w_o"], lw["g2"], lw["w_rt"], lw["b_rt"])


def _token_rows(first_token, n, c, span):
    return pl.ds(first_token * span + c, n, stride=span)


def _load_tokens(ref, first_token, n, tiles, span):
    return [ref[_token_rows(first_token, n, c, span), :] for c in range(tiles)]


def _store_tokens(ref, value, span):
    n, width = value.shape
    for c in range(span):
        tile = value[:, c * LANES:(c + 1) * LANES] if c * LANES < width else jnp.zeros((n, LANES), value.dtype)
        ref[_token_rows(0, n, c, span), :] = tile


def _token_gather(idx_ref, tokens_hbm, buf, sem, base, n, span):
    def start():
        def one(r, carry):
            src = pl.ds(pl.multiple_of(idx_ref[0, 0, r] * span, span), span)
            dst = pl.ds(pl.multiple_of((base + r) * span, span), span)
            pltpu.make_async_copy(tokens_hbm.at[src, :], buf.at[dst, :], sem).start()
            return carry
        lax.fori_loop(0, n, one, 0, unroll=8)

    def wait():
        dst = pl.ds(pl.multiple_of(base * span, span), n * span)
        pltpu.make_async_copy(tokens_hbm.at[pl.ds(0, n * span), :], buf.at[dst, :], sem).wait()

    return start, wait


def _gathered_tokens(idx_ref, idx_next_ref, tokens_hbm, buf, sems, n, tiles, span):
    j = pl.program_id(0)
    slot = j % 2
    start_cur, wait_cur = _token_gather(idx_ref, tokens_hbm, buf, sems.at[slot], slot * n, n, span)
    start_next, _ = _token_gather(idx_next_ref, tokens_hbm, buf, sems.at[1 - slot], (1 - slot) * n, n, span)

    @pl.when(j == 0)
    def _():
        start_cur()

    @pl.when(j + 1 < pl.num_programs(0))
    def _():
        start_next()

    wait_cur()
    return _load_tokens(buf, slot * n, n, tiles, span)


def _idx_specs(n_steps, n):
    cur = lambda j, *_: (j, 0, 0)
    nxt = lambda j, *_: (jnp.minimum(j + 1, n_steps - 1), 0, 0)
    return [pl.BlockSpec((1, 1, n), cur, memory_space=pltpu.SMEM),
            pl.BlockSpec((1, 1, n), nxt, memory_space=pltpu.SMEM)]


def _moe_routed_kernel(group_ref, src_ref, src_next_ref, rows_hbm, wg_ref, wu_ref, wd_ref, out_ref, buf, sems):
    tile = out_ref.shape[0] // OUT_SPAN
    g = group_ref[pl.program_id(0)]
    rows = _gathered_tokens(src_ref, src_next_ref, rows_hbm, buf, sems, tile, ROW_TILES, ROW_SPAN)
    t = jnp.concatenate([x.astype(BF16) for x in rows[:OUT_SPAN]], axis=1)
    comb = rows[OUT_SPAN]
    lane = lax.broadcasted_iota(jnp.int32, comb.shape, 1)
    out = jnp.zeros((tile, D_MODEL), F32)
    for e in range(EXPERTS_PER_GROUP):
        gate = jnp.dot(t, wg_ref[0, e], preferred_element_type=F32)
        up = jnp.dot(t, wu_ref[0, e], preferred_element_type=F32)
        he = (gate * _sigmoid(gate) * up).astype(BF16)
        d = jnp.dot(he, wd_ref[0, e], preferred_element_type=F32)
        c = jnp.sum(jnp.where(lane == g * EXPERTS_PER_GROUP + e, comb, 0.0), axis=-1, keepdims=True)
        out = out + c * d
    _store_tokens(out_ref, out, OUT_SPAN)


def _moe_routed(tile_group, src, rows, lw, tile):
    n_tiles = tile_group.shape[0]
    wspec = lambda a, b: pl.BlockSpec((1, EXPERTS_PER_GROUP, a, b), lambda j, grp: (grp[j], 0, 0, 0))
    return pl.pallas_call(
        _moe_routed_kernel,
        grid_spec=pltpu.PrefetchScalarGridSpec(
            num_scalar_prefetch=1,
            grid=(n_tiles,),
            in_specs=_idx_specs(n_tiles, tile) + [
                pl.BlockSpec(memory_space=pl.ANY),
                wspec(D_MODEL, EXPERT_HIDDEN), wspec(D_MODEL, EXPERT_HIDDEN), wspec(EXPERT_HIDDEN, D_MODEL)],
            out_specs=pl.BlockSpec((tile * OUT_SPAN, LANES), lambda j, grp: (j, 0)),
            scratch_shapes=[pltpu.VMEM((2 * tile * ROW_SPAN, LANES), F32), pltpu.SemaphoreType.DMA((2,))],
        ),
        out_shape=jax.ShapeDtypeStruct((n_tiles * tile * OUT_SPAN, LANES), F32),
        compiler_params=pltpu.CompilerParams(dimension_semantics=("arbitrary",), vmem_limit_bytes=VMEM_LIMIT),
        name="moe_routed",
    )(tile_group, src, src, rows, lw["w_gate"], lw["w_up"], lw["w_down"])


def _combine_kernel(dest_ref, dest_next_ref, xm_ref, sorted_hbm, y_ref, buf, sems):
    tm = xm_ref.shape[0]
    moe = _gathered_tokens(dest_ref, dest_next_ref, sorted_hbm, buf, sems, tm, OUT_SPAN, OUT_SPAN)
    y_ref[...] = xm_ref[...] + jnp.concatenate(moe, axis=1)


def _combine(dest, xm, moe_sorted, tm):
    t = xm.shape[0]
    n = t // tm
    idx = dest.reshape(n, 1, tm)
    return pl.pallas_call(
        _combine_kernel,
        grid=(n,),
        in_specs=_idx_specs(n, tm) + [pl.BlockSpec((tm, D_MODEL), lambda j: (j, 0)),
                                      pl.BlockSpec(memory_space=pl.ANY)],
        out_specs=pl.BlockSpec((tm, D_MODEL), lambda j: (j, 0)),
        out_shape=jax.ShapeDtypeStruct((t, D_MODEL), F32),
        scratch_shapes=[pltpu.VMEM((2 * tm * OUT_SPAN, LANES), F32), pltpu.SemaphoreType.DMA((2,))],
        compiler_params=pltpu.CompilerParams(dimension_semantics=("arbitrary",), vmem_limit_bytes=VMEM_LIMIT),
        name="moe_combine",
    )(idx, idx, xm, moe_sorted)


def _slot_sources_kernel(dest_ref, src_ref):
    def clear(i, carry):
        src_ref[i] = 0
        return carry
    lax.fori_loop(0, src_ref.shape[0], clear, 0, unroll=8)

    def put(token, carry):
        src_ref[dest_ref[token]] = token
        return carry
    lax.fori_loop(0, dest_ref.shape[0], put, 0, unroll=8)


def _slot_sources(dest, n_slots):
    return pl.pallas_call(
        _slot_sources_kernel,
        in_specs=[pl.BlockSpec(memory_space=pltpu.SMEM)],
        out_specs=pl.BlockSpec(memory_space=pltpu.SMEM),
        out_shape=jax.ShapeDtypeStruct((n_slots,), jnp.int32),
        name="moe_slot_sources",
    )(dest)


def _routing_tables(group_id, tile):
    t = group_id.shape[0]
    n_tiles = t // tile + N_GROUPS
    onehot = (group_id[:, None] == jnp.arange(N_GROUPS, dtype=jnp.int32)[None, :]).astype(jnp.int32)
    csum = jnp.cumsum(onehot, axis=0)
    rank = jnp.sum(onehot * csum, axis=1) - 1
    tiles_per_group = (csum[-1] + tile - 1) // tile
    tile_end = jnp.cumsum(tiles_per_group)
    offset = (tile_end - tiles_per_group) * tile
    dest = jnp.sum(onehot * offset[None, :], axis=1) + rank
    src = _slot_sources(dest.astype(jnp.int32), n_tiles * tile)
    tile_group = jnp.sum(jnp.arange(n_tiles, dtype=jnp.int32)[:, None] >= tile_end[None, :], axis=1)
    tile_group = jnp.minimum(tile_group, N_GROUPS - 1).astype(jnp.int32)
    return dest.astype(jnp.int32), src.reshape(n_tiles, 1, tile), tile_group


def _moe_kernel(h2_ref, comb_ref, xm_ref, wg_ref, wu_ref, wd_ref, y_ref):
    e = pl.program_id(1)

    @pl.when(e == 0)
    def _():
        y_ref[...] = xm_ref[...]

    t = h2_ref[...]
    gate = jnp.dot(t, wg_ref[0], preferred_element_type=F32)
    up = jnp.dot(t, wu_ref[0], preferred_element_type=F32)
    he = (gate * _sigmoid(gate) * up).astype(BF16)
    d = jnp.dot(he, wd_ref[0], preferred_element_type=F32)
    comb = comb_ref[...]
    lane = lax.broadcasted_iota(jnp.int32, comb.shape, 1)
    c = jnp.sum(jnp.where(lane == e, comb, 0.0), axis=-1, keepdims=True)
    y_ref[...] += c * d


def _moe(h2, comb, xm, lw, tm):
    t = h2.shape[0]
    tok = lambda width: pl.BlockSpec((tm, width), lambda i, e: (i, 0))
    return pl.pallas_call(
        _moe_kernel,
        grid=(t // tm, N_EXPERTS),
        in_specs=[tok(D_MODEL), tok(LANES), tok(D_MODEL),
                  pl.BlockSpec((1, D_MODEL, EXPERT_HIDDEN), lambda i, e: (e, 0, 0)),
                  pl.BlockSpec((1, D_MODEL, EXPERT_HIDDEN), lambda i, e: (e, 0, 0)),
                  pl.BlockSpec((1, EXPERT_HIDDEN, D_MODEL), lambda i, e: (e, 0, 0))],
        out_specs=tok(D_MODEL),
        out_shape=jax.ShapeDtypeStruct((t, D_MODEL), F32),
        compiler_params=pltpu.CompilerParams(dimension_semantics=("arbitrary", "arbitrary"),
                                             vmem_limit_bytes=VMEM_LIMIT),
        name="moe",
    )(h2, comb, xm, *(lw[n].reshape(N_EXPERTS, *lw[n].shape[2:]) for n in ("w_gate", "w_up", "w_down")))


def _hi_lo_columns(w):
    hi = w.astype(BF16)
    return jnp.concatenate([hi, (w - hi.astype(F32)).astype(BF16)], axis=1)


def _layer_weights(l, norm1_g, w_in, conv_dw, conv_db, conv_ln_g, conv_ln_b, w_conv_out, q_norm_g, k_norm_g,
                   lambda_q1, lambda_k1, lambda_q2, lambda_k2, subln_g, w_attn_out, w_o, norm2_g, w_group, b_group,
                   w_router, b_router, w_gate, w_up, w_down):
    row = lambda a: a.reshape(1, -1).astype(F32)
    grouped = lambda w: w.astype(BF16).reshape(N_GROUPS, EXPERTS_PER_GROUP, *w.shape[1:])
    head_gain = lambda gain: jnp.tile(gain.reshape(-1), N_HEADS).reshape(1, QK_DIM).astype(F32)
    grp = jnp.arange(QK_DIM) // HEAD_DIM
    w_rt = jnp.concatenate([w_group[l], jnp.moveaxis(w_router[l], 0, 1).reshape(D_MODEL, N_EXPERTS)], axis=1)
    b_rt = jnp.concatenate([b_group[l], b_router[l].reshape(-1)])
    pad = LANES - N_GROUPS - N_EXPERTS
    return {
        "g1": row(norm1_g[l]), "w_in": w_in[l].astype(BF16), "conv_dw": conv_dw[l].astype(F32),
        "conv_db": row(conv_db[l]), "conv_ln_g": row(conv_ln_g[l]), "conv_ln_b": row(conv_ln_b[l]),
        "w_conv_out": w_conv_out[l].astype(BF16), "gq": head_gain(q_norm_g[l]), "gk": head_gain(k_norm_g[l]),
        "bd": (grp[:, None] == grp[None, :]).astype(BF16),
        "lq1": row(lambda_q1[l]), "lk1": row(lambda_k1[l]), "lq2": row(lambda_q2[l]), "lk2": row(lambda_k2[l]),
        "subln_g": row(subln_g[l]), "w_attn_out": w_attn_out[l].astype(BF16), "w_o": w_o[l].astype(BF16),
        "g2": row(norm2_g[l]),
        "w_rt": _hi_lo_columns(jnp.pad(w_rt.astype(F32), ((0, 0), (0, pad)))),
        "b_rt": jnp.pad(b_rt.astype(F32), (0, pad)).reshape(1, LANES),
        "w_gate": grouped(w_gate[l]), "w_up": grouped(w_up[l]), "w_down": grouped(w_down[l]),
    }


def _tail_dense(x, attn, p, g, lw, tm):
    b, s, _ = x.shape
    flat = lambda a: a.reshape(b * s, a.shape[-1])
    xm, h2, comb = _post(flat(attn), flat(p), flat(g), flat(x), lw, tm, False)
    return _moe(h2, comb, xm, lw, tm).reshape(b, s, D_MODEL)


def _tail_routed(x, attn, p, g, lw, tm, tile):
    b, s, _ = x.shape
    flat = lambda a: a.reshape(b * s, a.shape[-1])
    xm, rows, gid = _post(flat(attn), flat(p), flat(g), flat(x), lw, tm, True)
    dest, src, tile_group = _routing_tables(gid[:, 0, :].reshape(-1).astype(jnp.int32), tile)
    moe_sorted = _moe_routed(tile_group, src, rows, lw, tile)
    return _combine(dest, xm, moe_sorted, tm).reshape(b, s, D_MODEL)


def kernel(x_prompt, x_sample, cache_k, cache_v, state_conv, norm1_g, w_in, conv_dw, conv_db, conv_ln_g, conv_ln_b, w_conv_out, q_norm_g, k_norm_g, lambda_q1, lambda_k1, lambda_q2, lambda_k2, subln_g, w_attn_out, w_o, norm2_g, w_group, b_group, w_router, b_router, w_gate, w_up, w_down):
    depth = w_in.shape[0]
    bp, sp, _ = x_prompt.shape
    bs, ss, _ = x_sample.shape
    past = cache_k.shape[2]
    xp, xs = x_prompt, x_sample
    kp_l, vp_l, cp_l, ks_l, vs_l, cs_l = [], [], [], [], [], []
    for l in range(depth):
        lw = _layer_weights(l, norm1_g, w_in, conv_dw, conv_db, conv_ln_g, conv_ln_b, w_conv_out, q_norm_g,
                            k_norm_g, lambda_q1, lambda_k1, lambda_q2, lambda_k2, subln_g, w_attn_out, w_o,
                            norm2_g, w_group, b_group, w_router, b_router, w_gate, w_up, w_down)
        lam_init = 0.8 - 0.6 * math.exp(-0.3 * l)

        hist_p = jnp.zeros((bp, HIST_ROWS, CONV_DIM), F32)
        q, kf, kb, vf, vb, p, g, cs = _in_proj(xp, hist_p, lw, 512)
        bound = (HEAD_DIM * SCALE) * jnp.max(jnp.abs(lw["gq"])) * jnp.max(jnp.abs(lw["gk"]))
        shift = (bound * LOG2E).reshape(1)
        attn = lax.cond(bound <= MAX_FIXED_SHIFT,
                        lambda *a: _attn_prompt(*a, lw, lam_init, ATTN_QUERY_BLOCK, False),
                        lambda *a: _attn_prompt(*a, lw, lam_init, ATTN_QUERY_BLOCK, True),
                        shift, q, kb, vb)
        xp = _tail_routed(xp, attn, p, g, lw, 512, MOE_TILE)
        kp_l.append(kf.reshape(bp, sp, N_HEADS, 2, HEAD_DIM))
        vp_l.append(vf.reshape(bp, sp, N_HEADS, V_DIM))
        cp_l.append(cs[:, HIST_PAD:, :])

        hist_s = jnp.pad(state_conv[l].astype(F32), ((0, 0), (HIST_PAD, 0), (0, 0)))
        q, kf, kb, vf, vb, p, g, cs = _in_proj(xs, hist_s, lw, ss)
        attn = _attn_sample(q, cache_k[l].astype(BF16).reshape(bs, past, QK_DIM),
                            cache_v[l].reshape(bs, past * N_HEADS, V_DIM),
                            kb, vb, lw, lam_init)
        xs = _tail_dense(xs, attn, p, g, lw, bs * ss)
        ks_l.append(kf.reshape(bs, ss, N_HEADS, 2, HEAD_DIM))
        vs_l.append(vf.reshape(bs, ss, N_HEADS, V_DIM))
        cs_l.append(cs[:, HIST_PAD:, :])
    return (xp, xs, jnp.stack(kp_l), jnp.stack(vp_l), jnp.stack(cp_l),
            jnp.stack(ks_l), jnp.stack(vs_l), jnp.stack(cs_l))
```

```python
import functools
import math

import jax
import jax.numpy as jnp
from jax import lax
from jax.experimental import pallas as pl
from jax.experimental.pallas import tpu as pltpu

D_MODEL = 1024
CHUNK = 64
CONV_DIM = 512
CONV_TAPS = 31
CONV_STATE = CONV_TAPS - 1
N_HEADS = 4
HEAD_DIM = 64
V_DIM = 2 * HEAD_DIM
ATTN_DIM = N_HEADS * V_DIM
QK_DIM = N_HEADS * 2 * HEAD_DIM
N_GROUPS = 4
EXPERTS_PER_GROUP = 4
N_EXPERTS = N_GROUPS * EXPERTS_PER_GROUP
TOP_K_INNER = 2
EXPERT_HIDDEN = 512
EPS = 1e-6
SCALE = HEAD_DIM ** -0.5
LOG2E = math.log2(math.e)
Q_SCALE = SCALE * LOG2E
MAX_FIXED_SHIFT = 40.0
COL_GLU = 2 * CONV_DIM
COL_Q = COL_GLU
COL_K = COL_Q + QK_DIM
COL_V = COL_K + QK_DIM
COL_GC = COL_V + ATTN_DIM
COL_GA = COL_GC + D_MODEL
IN_COLS = COL_GA + D_MODEL

LANES = 128
HIST_ROWS = 32
HIST_PAD = HIST_ROWS - CONV_STATE
CONV_ROW_CHUNK = 64
VMEM_LIMIT = 56 * 1024 * 1024
OUT_SPAN = D_MODEL // LANES
ROW_TILES = OUT_SPAN + 1
ROW_SPAN = 16
TOKEN_TILE = 512
MOE_TILE = 512
ATTN_QUERY_BLOCK = 2048
ATTN_KEY_BLOCK = 256
ATTN_KEYS_PER_TRIP = 1024

BF16 = jnp.bfloat16
F32 = jnp.float32


def _sigmoid(x):
    return 0.5 * jnp.tanh(0.5 * x) + 0.5


def _const_spec(shape):
    n = len(shape)
    return pl.BlockSpec(shape, lambda *_: (0,) * n)


def _in_proj_kernel(x_ref, hist_ref, g1_ref, w_in_ref, dw_ref, db_ref, lng_ref, lnb_ref, wco_ref, gq_ref, gk_ref,
                    bd_ref, q_ref, kf_ref, kb_ref, vf_ref, vb_ref, p_ref, g_ref, cs_ref, cbuf, ybuf):
    t = pl.program_id(1)
    tm = x_ref.shape[1]
    x = x_ref[0]
    ms = jnp.mean(x * x, axis=-1, keepdims=True)
    h = (x * lax.rsqrt(ms + EPS) * g1_ref[...]).astype(BF16)

    def proj(lo, hi):
        return jnp.dot(h, w_in_ref[:, lo:hi], preferred_element_type=F32)

    u = proj(0, CONV_DIM) * _sigmoid(proj(CONV_DIM, COL_GLU))

    @pl.when(t == 0)
    def _():
        cbuf[0:HIST_ROWS, :] = hist_ref[0]

    @pl.when(t > 0)
    def _():
        cbuf[0:HIST_ROWS, :] = cbuf[tm:tm + HIST_ROWS, :]

    cbuf[HIST_ROWS:HIST_ROWS + tm, :] = u
    cs_ref[0] = cbuf[tm:tm + HIST_ROWS, :]

    rc = min(CONV_ROW_CHUNK, tm)
    for c in range(tm // rc):
        acc = None
        for res in range(8):
            rows = rc if res == 0 else rc + 8
            z = None
            for off in range(res, CONV_TAPS + HIST_PAD, 8):
                j = off - HIST_PAD
                if j < 0:
                    continue
                lo = c * rc + off - res
                term = dw_ref[j:j + 1, :] * cbuf[lo:lo + rows, :]
                z = term if z is None else z + term
            z = z[res:res + rc, :]
            acc = z if acc is None else acc + z
        y = acc + db_ref[...]
        mu = jnp.mean(y, axis=-1, keepdims=True)
        yc = y - mu
        var = jnp.mean(yc * yc, axis=-1, keepdims=True)
        yn = yc * lax.rsqrt(var + EPS) * lng_ref[...] + lnb_ref[...]
        ybuf[c * rc:(c + 1) * rc, :] = (yn * _sigmoid(yn)).astype(BF16)
    conv_out = jnp.dot(ybuf[...], wco_ref[...], preferred_element_type=F32)
    p_ref[0] = (_sigmoid(proj(COL_GC, COL_GA)) * conv_out).astype(BF16)
    g_ref[0] = _sigmoid(proj(COL_GA, IN_COLS)).astype(BF16)

    def head_norm(z, gain_ref):
        ss = jnp.dot((z * z).astype(BF16), bd_ref[...], preferred_element_type=F32) * (1.0 / HEAD_DIM)
        return z * lax.rsqrt(ss + EPS) * gain_ref[...]

    q_ref[0] = (head_norm(proj(COL_Q, COL_K), gq_ref) * Q_SCALE).astype(BF16)
    kn = head_norm(proj(COL_K, COL_V), gk_ref)
    kb_ref[0] = kn.astype(BF16)
    v = proj(COL_V, COL_GC)
    vb_ref[0] = v.astype(BF16)
    n_maps = QK_DIM // HEAD_DIM
    for c in range(n_maps):
        kf_ref[0, pl.ds(c, tm, stride=n_maps), :] = kn[:, c * HEAD_DIM:(c + 1) * HEAD_DIM]
    for hd in range(N_HEADS):
        vf_ref[0, pl.ds(hd, tm, stride=N_HEADS), :] = v[:, hd * V_DIM:(hd + 1) * V_DIM]


def _in_proj(x, hist, lw, tm):
    b, s, _ = x.shape
    grid = (b, s // tm)
    tok = lambda width: pl.BlockSpec((1, tm, width), lambda i, j: (i, j, 0))
    out_shape = (
        jax.ShapeDtypeStruct((b, s, QK_DIM), BF16),
        jax.ShapeDtypeStruct((b, s * QK_DIM // HEAD_DIM, HEAD_DIM), F32),
        jax.ShapeDtypeStruct((b, s, QK_DIM), BF16),
        jax.ShapeDtypeStruct((b, s * N_HEADS, V_DIM), F32),
        jax.ShapeDtypeStruct((b, s, ATTN_DIM), BF16),
        jax.ShapeDtypeStruct((b, s, D_MODEL), BF16),
        jax.ShapeDtypeStruct((b, s, D_MODEL), BF16),
        jax.ShapeDtypeStruct((b, HIST_ROWS, CONV_DIM), F32),
    )
    return pl.pallas_call(
        _in_proj_kernel,
        grid=grid,
        in_specs=[
            tok(D_MODEL),
            pl.BlockSpec((1, HIST_ROWS, CONV_DIM), lambda i, j: (i, 0, 0)),
            _const_spec((1, D_MODEL)),
            _const_spec((D_MODEL, IN_COLS)),
            _const_spec((CONV_TAPS, CONV_DIM)),
            _const_spec((1, CONV_DIM)),
            _const_spec((1, CONV_DIM)),
            _const_spec((1, CONV_DIM)),
            _const_spec((CONV_DIM, D_MODEL)),
            _const_spec((1, QK_DIM)),
            _const_spec((1, QK_DIM)),
            _const_spec((QK_DIM, QK_DIM)),
        ],
        out_specs=(tok(QK_DIM), pl.BlockSpec((1, tm * QK_DIM // HEAD_DIM, HEAD_DIM), lambda i, j: (i, j, 0)),
                   tok(QK_DIM), pl.BlockSpec((1, tm * N_HEADS, V_DIM), lambda i, j: (i, j, 0)),
                   tok(ATTN_DIM), tok(D_MODEL), tok(D_MODEL),
                   pl.BlockSpec((1, HIST_ROWS, CONV_DIM), lambda i, j: (i, 0, 0))),
        out_shape=out_shape,
        scratch_shapes=[pltpu.VMEM((HIST_ROWS + tm, CONV_DIM), F32), pltpu.VMEM((tm, CONV_DIM), BF16)],
        compiler_params=pltpu.CompilerParams(dimension_semantics=("arbitrary", "arbitrary"),
                                             vmem_limit_bytes=VMEM_LIMIT),
        name="in_proj",
    )(x, hist, lw["g1"], lw["w_in"], lw["conv_dw"], lw["conv_db"], lw["conv_ln_g"], lw["conv_ln_b"],
      lw["w_conv_out"], lw["gq"], lw["gk"], lw["bd"])


def _lambda(lq1_ref, lk1_ref, lq2_ref, lk2_ref, lam_init):
    a = jnp.sum(lq1_ref[...] * lk1_ref[...], axis=-1, keepdims=True)
    b = jnp.sum(lq2_ref[...] * lk2_ref[...], axis=-1, keepdims=True)
    return jnp.exp(a) - jnp.exp(b) + lam_init


def _split_maps(q):
    lane = lax.broadcasted_iota(jnp.int32, q.shape, 1)
    zero = jnp.zeros_like(q)
    return jnp.where(lane < HEAD_DIM, q, zero), jnp.where(lane >= HEAD_DIM, q, zero)


def _scores(qm, k):
    return lax.dot_general(qm, k, (((1,), (1,)), ((), ())), preferred_element_type=F32)


def _sub_ln(o1, o2, lam, sg_ref, lam_init):
    o = o1 - lam * o2
    ms = jnp.mean(o * o, axis=-1, keepdims=True)
    return o * lax.rsqrt(ms + EPS) * sg_ref[...] * (1.0 - lam_init)


def _chunk_mask(s):
    row = lax.broadcasted_iota(jnp.int32, s.shape, 0) // CHUNK
    col = lax.broadcasted_iota(jnp.int32, s.shape, 1) // CHUNK
    return jnp.where(col <= row, s, -jnp.inf)


def _attn_prompt_kernel(shift_ref, q_ref, k_ref, v_ref, lq1_ref, lk1_ref, lq2_ref, lk2_ref, sg_ref, o_ref,
                        qs1, qs2, m1, l1, a1, m2, l2, a2, *, lam_init, online):
    qi = pl.program_id(2)
    tq = q_ref.shape[1]
    q1, q2 = _split_maps(q_ref[0])
    qs1[...] = q1
    qs2[...] = q2
    for m, l, a in ((m1, l1, a1), (m2, l2, a2)):
        m[...] = jnp.full(m.shape, -jnp.inf, F32)
        l[...] = jnp.zeros(l.shape, F32)
        a[...] = jnp.zeros(a.shape, F32)

    def block(kstart, nk, masked, r0=0):
        rows = slice(r0, tq)
        k = k_ref[0, pl.ds(kstart, nk), :]
        v = v_ref[0, pl.ds(kstart, nk), :]
        for qs, m, l, a in ((qs1, m1, l1, a1), (qs2, m2, l2, a2)):
            s = _scores(qs[rows, :], k)
            if masked:
                s = _chunk_mask(s)
            if online:
                m_prev = m[rows, :]
                m_new = jnp.maximum(m_prev, jnp.max(s, axis=-1, keepdims=True))
                alpha = jnp.exp2(m_prev - m_new)
                p = jnp.exp2(s - m_new)
                l[rows, :] = alpha * l[rows, :] + jnp.sum(p, axis=-1, keepdims=True)
                a[rows, :] = alpha * a[rows, :] + jnp.dot(p.astype(BF16), v, preferred_element_type=F32)
                m[rows, :] = m_new
            else:
                p = jnp.exp2(s - shift_ref[0])
                part = p[:, 0:LANES]
                for c in range(1, nk // LANES):
                    part = part + p[:, c * LANES:(c + 1) * LANES]
                l[rows, :] += part
                a[rows, :] += jnp.dot(p.astype(BF16), v, preferred_element_type=F32)

    def body(i, carry):
        block(pl.multiple_of(i * ATTN_KEYS_PER_TRIP, ATTN_KEYS_PER_TRIP), ATTN_KEYS_PER_TRIP, False)
        return carry

    lax.fori_loop(0, qi * (tq // ATTN_KEYS_PER_TRIP), body, 0)
    for d in range(tq // ATTN_KEY_BLOCK):
        block(pl.multiple_of(qi * tq + d * ATTN_KEY_BLOCK, ATTN_KEY_BLOCK), ATTN_KEY_BLOCK, True, d * ATTN_KEY_BLOCK)

    lam = _lambda(lq1_ref, lk1_ref, lq2_ref, lk2_ref, lam_init)
    d1 = jnp.sum(l1[...], axis=-1, keepdims=True)
    d2 = jnp.sum(l2[...], axis=-1, keepdims=True)
    o_ref[0] = _sub_ln(a1[...] / d1, a2[...] / d2, lam, sg_ref, lam_init).astype(BF16)


def _attn_prompt(shift, q, k, v, lw, lam_init, tq, online):
    b, s, _ = q.shape
    grid = (b, N_HEADS, s // tq)
    qspec = pl.BlockSpec((1, tq, LANES), lambda i, h, j: (i, j, h))
    kvspec = pl.BlockSpec((1, s, LANES), lambda i, h, j: (i, 0, h))
    vec = _const_spec((1, HEAD_DIM))
    lw_width = 1 if online else LANES
    return pl.pallas_call(
        functools.partial(_attn_prompt_kernel, lam_init=lam_init, online=online),
        grid=grid,
        in_specs=[pl.BlockSpec(memory_space=pltpu.SMEM), qspec, kvspec, kvspec, vec, vec, vec, vec,
                  _const_spec((1, V_DIM))],
        out_specs=qspec,
        out_shape=jax.ShapeDtypeStruct((b, s, ATTN_DIM), BF16),
        scratch_shapes=[pltpu.VMEM((tq, LANES), BF16), pltpu.VMEM((tq, LANES), BF16),
                        pltpu.VMEM((tq, 1), F32), pltpu.VMEM((tq, lw_width), F32), pltpu.VMEM((tq, V_DIM), F32),
                        pltpu.VMEM((tq, 1), F32), pltpu.VMEM((tq, lw_width), F32), pltpu.VMEM((tq, V_DIM), F32)],
        compiler_params=pltpu.CompilerParams(dimension_semantics=("arbitrary", "arbitrary", "arbitrary"),
                                             vmem_limit_bytes=VMEM_LIMIT),
        name="attn_prompt_online" if online else "attn_prompt",
    )(shift, q, k, v, lw["lq1"], lw["lk1"], lw["lq2"], lw["lk2"], lw["subln_g"])


def _attn_sample_kernel(q_ref, kc_ref, vc_ref, kn_ref, vn_ref, lq1_ref, lk1_ref, lq2_ref, lk2_ref, sg_ref, o_ref,
                        *, lam_init):
    t = q_ref.shape[1]
    qq = jnp.concatenate(_split_maps(q_ref[0]), axis=0)
    kc = kc_ref[0].astype(BF16)
    past = vc_ref.shape[1] // N_HEADS
    vc = vc_ref[0, pl.ds(pl.program_id(1), past, stride=N_HEADS), :].astype(BF16)
    sc = _scores(qq, kc)
    sn = _scores(qq, kn_ref[0])
    m = jnp.maximum(jnp.max(sc, axis=-1, keepdims=True), jnp.max(sn, axis=-1, keepdims=True))
    pc = jnp.exp2(sc - m)
    pn = jnp.exp2(sn - m)
    l = jnp.sum(pc, axis=-1, keepdims=True) + jnp.sum(pn, axis=-1, keepdims=True)
    o = (jnp.dot(pc.astype(BF16), vc, preferred_element_type=F32)
         + jnp.dot(pn.astype(BF16), vn_ref[0], preferred_element_type=F32)) / l
    lam = _lambda(lq1_ref, lk1_ref, lq2_ref, lk2_ref, lam_init)
    o_ref[0] = _sub_ln(o[0:t], o[t:2 * t], lam, sg_ref, lam_init).astype(BF16)


def _attn_sample(q, k_cache, v_cache, k_new, v_new, lw, lam_init):
    b, t, _ = q.shape
    past = k_cache.shape[1]
    new = pl.BlockSpec((1, t, LANES), lambda i, h: (i, 0, h))
    cache = pl.BlockSpec((1, past, LANES), lambda i, h: (i, 0, h))
    cache_v = pl.BlockSpec((1, past * N_HEADS, V_DIM), lambda i, h: (i, 0, 0))
    vec = _const_spec((1, HEAD_DIM))
    return pl.pallas_call(
        functools.partial(_attn_sample_kernel, lam_init=lam_init),
        grid=(b, N_HEADS),
        in_specs=[new, cache, cache_v, new, new, vec, vec, vec, vec, _const_spec((1, V_DIM))],
        out_specs=new,
        out_shape=jax.ShapeDtypeStruct((b, t, ATTN_DIM), BF16),
        compiler_params=pltpu.CompilerParams(dimension_semantics=("arbitrary", "arbitrary"),
                                             vmem_limit_bytes=VMEM_LIMIT),
        name="attn_sample",
    )(q, k_cache, v_cache, k_new, v_new, lw["lq1"], lw["lk1"], lw["lq2"], lw["lk2"], lw["subln_g"])


def _post_kernel(attn_ref, p_ref, g_ref, x_ref, wao_ref, wo_ref, g2_ref, wrt_ref, brt_ref, xm_ref, *out_refs, routed):
    tm = x_ref.shape[0]
    ao = jnp.dot(attn_ref[...], wao_ref[...], preferred_element_type=F32)
    merged = p_ref[...].astype(F32) + g_ref[...].astype(F32) * ao
    xm = x_ref[...] + jnp.dot(merged.astype(BF16), wo_ref[...], preferred_element_type=F32)
    xm_ref[...] = xm
    ms = jnp.mean(xm * xm, axis=-1, keepdims=True)
    h2 = xm * lax.rsqrt(ms + EPS) * g2_ref[...]

    h_hi = h2.astype(BF16)
    h_lo = (h2 - h_hi.astype(F32)).astype(BF16)
    r = (jnp.dot(h_hi, wrt_ref[...], preferred_element_type=F32)
         + jnp.dot(h_lo, wrt_ref[...], preferred_element_type=F32))
    lg = r[:, :LANES] + r[:, LANES:] + brt_ref[...]
    lane = lax.broadcasted_iota(jnp.int32, lg.shape, 1)
    neg = jnp.full_like(lg, -jnp.inf)
    big = jnp.full_like(lane, LANES)

    def first_max(z):
        zmax = jnp.max(z, axis=-1, keepdims=True)
        return zmax, jnp.min(jnp.where(z == zmax, lane, big), axis=-1, keepdims=True)

    is_group = lane < N_GROUPS
    gmax, gi = first_max(jnp.where(is_group, lg, neg))
    g_w = 1.0 / jnp.sum(jnp.where(is_group, jnp.exp(lg - gmax), 0.0), axis=-1, keepdims=True)
    lo = N_GROUPS + EXPERTS_PER_GROUP * gi
    el = jnp.where((lane >= lo) & (lane < lo + EXPERTS_PER_GROUP), lg, neg)
    v1, i1 = first_max(el)
    v2, i2 = first_max(jnp.where(lane == i1, neg, el))
    e2 = jnp.exp(v2 - v1)
    w1 = g_w / (1.0 + e2)
    w2 = g_w * e2 / (1.0 + e2)
    comb = jnp.where(lane == i1 - N_GROUPS, w1, 0.0) + jnp.where(lane == i2 - N_GROUPS, w2, 0.0)
    if routed:
        rows_ref, gid_ref = out_refs
        _store_tokens(rows_ref, jnp.concatenate([h2, comb], axis=1), ROW_SPAN)
        gid = jnp.where(lane == 0, gi.astype(F32), 0.0).astype(BF16)
        pick = (lax.broadcasted_iota(jnp.int32, (8, LANES), 1) == 0).astype(BF16)
        gid_ref[0] = lax.dot_general(pick, gid, (((1,), (1,)), ((), ())), preferred_element_type=F32)
    else:
        h2_ref, comb_ref = out_refs
        h2_ref[...] = h_hi
        comb_ref[...] = comb


def _post(attn, p, g, x, lw, tm, routed):
    t = x.shape[0]
    tok = lambda width: pl.BlockSpec((tm, width), lambda i: (i, 0))
    if routed:
        out_specs = (tok(D_MODEL), pl.BlockSpec((tm * ROW_SPAN, LANES), lambda i: (i, 0)),
                     pl.BlockSpec((1, 8, tm), lambda i: (i, 0, 0)))
        out_shape = (jax.ShapeDtypeStruct((t, D_MODEL), F32), jax.ShapeDtypeStruct((t * ROW_SPAN, LANES), F32),
                     jax.ShapeDtypeStruct((t // tm, 8, tm), F32))
    else:
        out_specs = (tok(D_MODEL), tok(D_MODEL), tok(LANES))
        out_shape = (jax.ShapeDtypeStruct((t, D_MODEL), F32), jax.ShapeDtypeStruct((t, D_MODEL), BF16),
                     jax.ShapeDtypeStruct((t, LANES), F32))
    return pl.pallas_call(
        functools.partial(_post_kernel, routed=routed),
        grid=(t // tm,),
        in_specs=[tok(ATTN_DIM), tok(D_MODEL), tok(D_MODEL), tok(D_MODEL),
                  _const_spec((ATTN_DIM, D_MODEL)), _const_spec((D_MODEL, D_MODEL)), _const_spec((1, D_MODEL)),
                  _const_spec((D_MODEL, 2 * LANES)), _const_spec((1, LANES))],
        out_specs=out_specs,
        out_shape=out_shape,
        compiler_params=pltpu.CompilerParams(dimension_semantics=("arbitrary",), vmem_limit_bytes=VMEM_LIMIT),
        name="post_routed" if routed else "post",
    )(attn, p, g, x, lw["w_attn_out"], lw["w_o"], lw["g2"], lw["w_rt"], lw["b_rt"])


def _token_rows(first_token, n, c, span):
    return pl.ds(first_token * span + c, n, stride=span)


def _load_tokens(ref, first_token, n, tiles, span):
    return [ref[_token_rows(first_token, n, c, span), :] for c in range(tiles)]


def _store_tokens(ref, value, span):
    n, width = value.shape
    for c in range(span):
        tile = value[:, c * LANES:(c + 1) * LANES] if c * LANES < width else jnp.zeros((n, LANES), value.dtype)
        ref[_token_rows(0, n, c, span), :] = tile


def _token_gather(idx_ref, tokens_hbm, buf, sem, base, n, span):
    def start():
        def one(r, carry):
            src = pl.ds(pl.multiple_of(idx_ref[0, 0, r] * span, span), span)
            dst = pl.ds(pl.multiple_of((base + r) * span, span), span)
            pltpu.make_async_copy(tokens_hbm.at[src, :], buf.at[dst, :], sem).start()
            return carry
        lax.fori_loop(0, n, one, 0, unroll=8)

    def wait():
        dst = pl.ds(pl.multiple_of(base * span, span), n * span)
        pltpu.make_async_copy(tokens_hbm.at[pl.ds(0, n * span), :], buf.at[dst, :], sem).wait()

    return start, wait


def _gathered_tokens(idx_ref, idx_next_ref, tokens_hbm, buf, sems, n, tiles, span):
    j = pl.program_id(0)
    slot = j % 2
    start_cur, wait_cur = _token_gather(idx_ref, tokens_hbm, buf, sems.at[slot], slot * n, n, span)
    start_next, _ = _token_gather(idx_next_ref, tokens_hbm, buf, sems.at[1 - slot], (1 - slot) * n, n, span)

    @pl.when(j == 0)
    def _():
        start_cur()

    @pl.when(j + 1 < pl.num_programs(0))
    def _():
        start_next()

    wait_cur()
    return _load_tokens(buf, slot * n, n, tiles, span)


def _idx_specs(n_steps, n):
    cur = lambda j, *_: (j, 0, 0)
    nxt = lambda j, *_: (jnp.minimum(j + 1, n_steps - 1), 0, 0)
    return [pl.BlockSpec((1, 1, n), cur, memory_space=pltpu.SMEM),
            pl.BlockSpec((1, 1, n), nxt, memory_space=pltpu.SMEM)]


def _moe_routed_kernel(group_ref, src_ref, src_next_ref, rows_hbm, wg_ref, wu_ref, wd_ref, out_ref, buf, sems):
    tile = out_ref.shape[0] // OUT_SPAN
    g = group_ref[pl.program_id(0)]
    rows = _gathered_tokens(src_ref, src_next_ref, rows_hbm, buf, sems, tile, ROW_TILES, ROW_SPAN)
    t = jnp.concatenate([x.astype(BF16) for x in rows[:OUT_SPAN]], axis=1)
    comb = rows[OUT_SPAN]
    lane = lax.broadcasted_iota(jnp.int32, comb.shape, 1)
    out = jnp.zeros((tile, D_MODEL), F32)
    for e in range(EXPERTS_PER_GROUP):
        gate = jnp.dot(t, wg_ref[0, e], preferred_element_type=F32)
        up = jnp.dot(t, wu_ref[0, e], preferred_element_type=F32)
        he = (gate * _sigmoid(gate) * up).astype(BF16)
        d = jnp.dot(he, wd_ref[0, e], preferred_element_type=F32)
        c = jnp.sum(jnp.where(lane == g * EXPERTS_PER_GROUP + e, comb, 0.0), axis=-1, keepdims=True)
        out = out + c * d
    _store_tokens(out_ref, out, OUT_SPAN)


def _moe_routed(tile_group, src, rows, lw, tile):
    n_tiles = tile_group.shape[0]
    wspec = lambda a, b: pl.BlockSpec((1, EXPERTS_PER_GROUP, a, b), lambda j, grp: (grp[j], 0, 0, 0))
    return pl.pallas_call(
        _moe_routed_kernel,
        grid_spec=pltpu.PrefetchScalarGridSpec(
            num_scalar_prefetch=1,
            grid=(n_tiles,),
            in_specs=_idx_specs(n_tiles, tile) + [
                pl.BlockSpec(memory_space=pl.ANY),
                wspec(D_MODEL, EXPERT_HIDDEN), wspec(D_MODEL, EXPERT_HIDDEN), wspec(EXPERT_HIDDEN, D_MODEL)],
            out_specs=pl.BlockSpec((tile * OUT_SPAN, LANES), lambda j, grp: (j, 0)),
            scratch_shapes=[pltpu.VMEM((2 * tile * ROW_SPAN, LANES), F32), pltpu.SemaphoreType.DMA((2,))],
        ),
        out_shape=jax.ShapeDtypeStruct((n_tiles * tile * OUT_SPAN, LANES), F32),
        compiler_params=pltpu.CompilerParams(dimension_semantics=("arbitrary",), vmem_limit_bytes=VMEM_LIMIT),
        name="moe_routed",
    )(tile_group, src, src, rows, lw["w_gate"], lw["w_up"], lw["w_down"])


def _combine_kernel(dest_ref, dest_next_ref, xm_ref, sorted_hbm, y_ref, buf, sems):
    tm = xm_ref.shape[0]
    moe = _gathered_tokens(dest_ref, dest_next_ref, sorted_hbm, buf, sems, tm, OUT_SPAN, OUT_SPAN)
    y_ref[...] = xm_ref[...] + jnp.concatenate(moe, axis=1)


def _combine(dest, xm, moe_sorted, tm):
    t = xm.shape[0]
    n = t // tm
    idx = dest.reshape(n, 1, tm)
    return pl.pallas_call(
        _combine_kernel,
        grid=(n,),
        in_specs=_idx_specs(n, tm) + [pl.BlockSpec((tm, D_MODEL), lambda j: (j, 0)),
                                      pl.BlockSpec(memory_space=pl.ANY)],
        out_specs=pl.BlockSpec((tm, D_MODEL), lambda j: (j, 0)),
        out_shape=jax.ShapeDtypeStruct((t, D_MODEL), F32),
        scratch_shapes=[pltpu.VMEM((2 * tm * OUT_SPAN, LANES), F32), pltpu.SemaphoreType.DMA((2,))],
        compiler_params=pltpu.CompilerParams(dimension_semantics=("arbitrary",), vmem_limit_bytes=VMEM_LIMIT),
        name="moe_combine",
    )(idx, idx, xm, moe_sorted)


def _slot_sources_kernel(dest_ref, src_ref):
    def clear(i, carry):
        src_ref[i] = 0
        return carry
    lax.fori_loop(0, src_ref.shape[0], clear, 0, unroll=8)

    def put(token, carry):
        src_ref[dest_ref[token]] = token
        return carry
    lax.fori_loop(0, dest_ref.shape[0], put, 0, unroll=8)


def _slot_sources(dest, n_slots):
    return pl.pallas_call(
        _slot_sources_kernel,
        in_specs=[pl.BlockSpec(memory_space=pltpu.SMEM)],
        out_specs=pl.BlockSpec(memory_space=pltpu.SMEM),
        out_shape=jax.ShapeDtypeStruct((n_slots,), jnp.int32),
        name="moe_slot_sources",
    )(dest)


def _routing_tables(group_id, tile):
    t = group_id.shape[0]
    n_tiles = t // tile + N_GROUPS
    onehot = (group_id[:, None] == jnp.arange(N_GROUPS, dtype=jnp.int32)[None, :]).astype(jnp.int32)
    csum = jnp.cumsum(onehot, axis=0)
    rank = jnp.sum(onehot * csum, axis=1) - 1
    tiles_per_group = (csum[-1] + tile - 1) // tile
    tile_end = jnp.cumsum(tiles_per_group)
    offset = (tile_end - tiles_per_group) * tile
    dest = jnp.sum(onehot * offset[None, :], axis=1) + rank
    src = _slot_sources(dest.astype(jnp.int32), n_tiles * tile)
    tile_group = jnp.sum(jnp.arange(n_tiles, dtype=jnp.int32)[:, None] >= tile_end[None, :], axis=1)
    tile_group = jnp.minimum(tile_group, N_GROUPS - 1).astype(jnp.int32)
    return dest.astype(jnp.int32), src.reshape(n_tiles, 1, tile), tile_group


def _moe_kernel(h2_ref, comb_ref, xm_ref, wg_ref, wu_ref, wd_ref, y_ref):
    e = pl.program_id(1)

    @pl.when(e == 0)
    def _():
        y_ref[...] = xm_ref[...]

    t = h2_ref[...]
    gate = jnp.dot(t, wg_ref[0], preferred_element_type=F32)
    up = jnp.dot(t, wu_ref[0], preferred_element_type=F32)
    he = (gate * _sigmoid(gate) * up).astype(BF16)
    d = jnp.dot(he, wd_ref[0], preferred_element_type=F32)
    comb = comb_ref[...]
    lane = lax.broadcasted_iota(jnp.int32, comb.shape, 1)
    c = jnp.sum(jnp.where(lane == e, comb, 0.0), axis=-1, keepdims=True)
    y_ref[...] += c * d


def _moe(h2, comb, xm, lw, tm):
    t = h2.shape[0]
    tok = lambda width: pl.BlockSpec((tm, width), lambda i, e: (i, 0))
    return pl.pallas_call(
        _moe_kernel,
        grid=(t // tm, N_EXPERTS),
        in_specs=[tok(D_MODEL), tok(LANES), tok(D_MODEL),
                  pl.BlockSpec((1, D_MODEL, EXPERT_HIDDEN), lambda i, e: (e, 0, 0)),
                  pl.BlockSpec((1, D_MODEL, EXPERT_HIDDEN), lambda i, e: (e, 0, 0)),
                  pl.BlockSpec((1, EXPERT_HIDDEN, D_MODEL), lambda i, e: (e, 0, 0))],
        out_specs=tok(D_MODEL),
        out_shape=jax.ShapeDtypeStruct((t, D_MODEL), F32),
        compiler_params=pltpu.CompilerParams(dimension_semantics=("arbitrary", "arbitrary"),
                                             vmem_limit_bytes=VMEM_LIMIT),
        name="moe",
    )(h2, comb, xm, *(lw[n].reshape(N_EXPERTS, *lw[n].shape[2:]) for n in ("w_gate", "w_up", "w_down")))


def _hi_lo_columns(w):
    hi = w.astype(BF16)
    return jnp.concatenate([hi, (w - hi.astype(F32)).astype(BF16)], axis=1)


def _layer_weights(l, norm1_g, w_in, conv_dw, conv_db, conv_ln_g, conv_ln_b, w_conv_out, q_norm_g, k_norm_g,
                   lambda_q1, lambda_k1, lambda_q2, lambda_k2, subln_g, w_attn_out, w_o, norm2_g, w_group, b_group,
                   w_router, b_router, w_gate, w_up, w_down):
    row = lambda a: a.reshape(1, -1).astype(F32)
    grouped = lambda w: w.astype(BF16).reshape(N_GROUPS, EXPERTS_PER_GROUP, *w.shape[1:])
    head_gain = lambda gain: jnp.tile(gain.reshape(-1), N_HEADS).reshape(1, QK_DIM).astype(F32)
    grp = jnp.arange(QK_DIM) // HEAD_DIM
    w_rt = jnp.concatenate([w_group[l], jnp.moveaxis(w_router[l], 0, 1).reshape(D_MODEL, N_EXPERTS)], axis=1)
    b_rt = jnp.concatenate([b_group[l], b_router[l].reshape(-1)])
    pad = LANES - N_GROUPS - N_EXPERTS
    return {
        "g1": row(norm1_g[l]), "w_in": w_in[l].astype(BF16), "conv_dw": conv_dw[l].astype(F32),
        "conv_db": row(conv_db[l]), "conv_ln_g": row(conv_ln_g[l]), "conv_ln_b": row(conv_ln_b[l]),
        "w_conv_out": w_conv_out[l].astype(BF16), "gq": head_gain(q_norm_g[l]), "gk": head_gain(k_norm_g[l]),
        "bd": (grp[:, None] == grp[None, :]).astype(BF16),
        "lq1": row(lambda_q1[l]), "lk1": row(lambda_k1[l]), "lq2": row(lambda_q2[l]), "lk2": row(lambda_k2[l]),
        "subln_g": row(subln_g[l]), "w_attn_out": w_attn_out[l].astype(BF16), "w_o": w_o[l].astype(BF16),
        "g2": row(norm2_g[l]),
        "w_rt": _hi_lo_columns(jnp.pad(w_rt.astype(F32), ((0, 0), (0, pad)))),
        "b_rt": jnp.pad(b_rt.astype(F32), (0, pad)).reshape(1, LANES),
        "w_gate": grouped(w_gate[l]), "w_up": grouped(w_up[l]), "w_down": grouped(w_down[l]),
    }


def _tail_dense(x, attn, p, g, lw, tm):
    b, s, _ = x.shape
    flat = lambda a: a.reshape(b * s, a.shape[-1])
    xm, h2, comb = _post(flat(attn), flat(p), flat(g), flat(x), lw, tm, False)
    return _moe(h2, comb, xm, lw, tm).reshape(b, s, D_MODEL)


def _tail_routed(x, attn, p, g, lw, tm, tile):
    b, s, _ = x.shape
    flat = lambda a: a.reshape(b * s, a.shape[-1])
    xm, rows, gid = _post(flat(attn), flat(p), flat(g), flat(x), lw, tm, True)
    dest, src, tile_group = _routing_tables(gid[:, 0, :].reshape(-1).astype(jnp.int32), tile)
    moe_sorted = _moe_routed(tile_group, src, rows, lw, tile)
    return _combine(dest, xm, moe_sorted, tm).reshape(b, s, D_MODEL)


def kernel(x_prompt, x_sample, cache_k, cache_v, state_conv, norm1_g, w_in, conv_dw, conv_db, conv_ln_g, conv_ln_b, w_conv_out, q_norm_g, k_norm_g, lambda_q1, lambda_k1, lambda_q2, lambda_k2, subln_g, w_attn_out, w_o, norm2_g, w_group, b_group, w_router, b_router, w_gate, w_up, w_down):
    depth = w_in.shape[0]
    bp, sp, _ = x_prompt.shape
    bs, ss, _ = x_sample.shape
    past = cache_k.shape[2]
    xp, xs = x_prompt, x_sample
    kp_l, vp_l, cp_l, ks_l, vs_l, cs_l = [], [], [], [], [], []
    for l in range(depth):
        lw = _layer_weights(l, norm1_g, w_in, conv_dw, conv_db, conv_ln_g, conv_ln_b, w_conv_out, q_norm_g,
                            k_norm_g, lambda_q1, lambda_k1, lambda_q2, lambda_k2, subln_g, w_attn_out, w_o,
                            norm2_g, w_group, b_group, w_router, b_router, w_gate, w_up, w_down)
        lam_init = 0.8 - 0.6 * math.exp(-0.3 * l)

        hist_p = jnp.zeros((bp, HIST_ROWS, CONV_DIM), F32)
        q, kf, kb, vf, vb, p, g, cs = _in_proj(xp, hist_p, lw, TOKEN_TILE)
        bound = (HEAD_DIM * SCALE) * jnp.max(jnp.abs(lw["gq"])) * jnp.max(jnp.abs(lw["gk"]))
        shift = (bound * LOG2E).reshape(1)
        attn = lax.cond(bound <= MAX_FIXED_SHIFT,
                        lambda *a: _attn_prompt(*a, lw, lam_init, ATTN_QUERY_BLOCK, False),
                        lambda *a: _attn_prompt(*a, lw, lam_init, ATTN_QUERY_BLOCK, True),
                        shift, q, kb, vb)
        xp = _tail_routed(xp, attn, p, g, lw, TOKEN_TILE, MOE_TILE)
        kp_l.append(kf.reshape(bp, sp, N_HEADS, 2, HEAD_DIM))
        vp_l.append(vf.reshape(bp, sp, N_HEADS, V_DIM))
        cp_l.append(cs[:, HIST_PAD:, :])

        hist_s = jnp.pad(state_conv[l].astype(F32), ((0, 0), (HIST_PAD, 0), (0, 0)))
        q, kf, kb, vf, vb, p, g, cs = _in_proj(xs, hist_s, lw, ss)
        attn = _attn_sample(q, cache_k[l].astype(BF16).reshape(bs, past, QK_DIM),
                            cache_v[l].reshape(bs, past * N_HEADS, V_DIM),
                            kb, vb, lw, lam_init)
        xs = _tail_dense(xs, attn, p, g, lw, bs * ss)
        ks_l.append(kf.reshape(bs, ss, N_HEADS, 2, HEAD_DIM))
        vs_l.append(vf.reshape(bs, ss, N_HEADS, V_DIM))
        cs_l.append(cs[:, HIST_PAD:, :])
    return (xp, xs, jnp.stack(kp_l), jnp.stack(vp_l), jnp.stack(cp_l),
            jnp.stack(ks_l), jnp.stack(vs_l), jnp.stack(cs_l))
```

```python
import functools
import math

import jax
import jax.numpy as jnp
from jax import lax
from jax.experimental import pallas as pl
from jax.experimental.pallas import tpu as pltpu

D_MODEL = 1024
CHUNK = 64
CONV_DIM = 512
CONV_TAPS = 31
CONV_STATE = CONV_TAPS - 1
N_HEADS = 4
HEAD_DIM = 64
V_DIM = 2 * HEAD_DIM
ATTN_DIM = N_HEADS * V_DIM
QK_DIM = N_HEADS * 2 * HEAD_DIM
N_GROUPS = 4
EXPERTS_PER_GROUP = 4
N_EXPERTS = N_GROUPS * EXPERTS_PER_GROUP
TOP_K_INNER = 2
EXPERT_HIDDEN = 512
EPS = 1e-6
SCALE = HEAD_DIM ** -0.5
LOG2E = math.log2(math.e)
Q_SCALE = SCALE * LOG2E
MAX_FIXED_SHIFT = 40.0
COL_GLU = 2 * CONV_DIM
COL_Q = COL_GLU
COL_K = COL_Q + QK_DIM
COL_V = COL_K + QK_DIM
COL_GC = COL_V + ATTN_DIM
COL_GA = COL_GC + D_MODEL
IN_COLS = COL_GA + D_MODEL

LANES = 128
HIST_ROWS = 32
HIST_PAD = HIST_ROWS - CONV_STATE
CONV_ROW_CHUNK = 64
VMEM_LIMIT = 56 * 1024 * 1024
OUT_SPAN = D_MODEL // LANES
ROW_TILES = OUT_SPAN + 1
ROW_SPAN = 16
TOKEN_TILE = 512
DMA_ISSUE_UNROLL = 8
MOE_TILE = 512
ATTN_QUERY_BLOCK = 2048
ATTN_KEY_BLOCK = 256
ATTN_KEYS_PER_TRIP = 1024

BF16 = jnp.bfloat16
F32 = jnp.float32


def _sigmoid(x):
    return 0.5 * jnp.tanh(0.5 * x) + 0.5


def _const_spec(shape):
    n = len(shape)
    return pl.BlockSpec(shape, lambda *_: (0,) * n)


def _in_proj_kernel(x_ref, hist_ref, g1_ref, w_in_ref, dw_ref, db_ref, lng_ref, lnb_ref, wco_ref, gq_ref, gk_ref,
                    bd_ref, q_ref, kf_ref, kb_ref, vf_ref, vb_ref, p_ref, g_ref, cs_ref, cbuf, ybuf):
    t = pl.program_id(1)
    tm = x_ref.shape[1]
    x = x_ref[0]
    ms = jnp.mean(x * x, axis=-1, keepdims=True)
    h = (x * lax.rsqrt(ms + EPS) * g1_ref[...]).astype(BF16)

    def proj(lo, hi):
        return jnp.dot(h, w_in_ref[:, lo:hi], preferred_element_type=F32)

    u = proj(0, CONV_DIM) * _sigmoid(proj(CONV_DIM, COL_GLU))

    @pl.when(t == 0)
    def _():
        cbuf[0:HIST_ROWS, :] = hist_ref[0]

    @pl.when(t > 0)
    def _():
        cbuf[0:HIST_ROWS, :] = cbuf[tm:tm + HIST_ROWS, :]

    cbuf[HIST_ROWS:HIST_ROWS + tm, :] = u
    cs_ref[0] = cbuf[tm:tm + HIST_ROWS, :]

    rc = min(CONV_ROW_CHUNK, tm)
    for c in range(tm // rc):
        acc = None
        for res in range(8):
            rows = rc if res == 0 else rc + 8
            z = None
            for off in range(res, CONV_TAPS + HIST_PAD, 8):
                j = off - HIST_PAD
                if j < 0:
                    continue
                lo = c * rc + off - res
                term = dw_ref[j:j + 1, :] * cbuf[lo:lo + rows, :]
                z = term if z is None else z + term
            z = z[res:res + rc, :]
            acc = z if acc is None else acc + z
        y = acc + db_ref[...]
        mu = jnp.mean(y, axis=-1, keepdims=True)
        yc = y - mu
        var = jnp.mean(yc * yc, axis=-1, keepdims=True)
        yn = yc * lax.rsqrt(var + EPS) * lng_ref[...] + lnb_ref[...]
        ybuf[c * rc:(c + 1) * rc, :] = (yn * _sigmoid(yn)).astype(BF16)
    conv_out = jnp.dot(ybuf[...], wco_ref[...], preferred_element_type=F32)
    p_ref[0] = (_sigmoid(proj(COL_GC, COL_GA)) * conv_out).astype(BF16)
    g_ref[0] = _sigmoid(proj(COL_GA, IN_COLS)).astype(BF16)

    def head_norm(z, gain_ref):
        ss = jnp.dot((z * z).astype(BF16), bd_ref[...], preferred_element_type=F32) * (1.0 / HEAD_DIM)
        return z * lax.rsqrt(ss + EPS) * gain_ref[...]

    q_ref[0] = (head_norm(proj(COL_Q, COL_K), gq_ref) * Q_SCALE).astype(BF16)
    kn = head_norm(proj(COL_K, COL_V), gk_ref)
    kb_ref[0] = kn.astype(BF16)
    v = proj(COL_V, COL_GC)
    vb_ref[0] = v.astype(BF16)
    n_maps = QK_DIM // HEAD_DIM
    for c in range(n_maps):
        kf_ref[0, pl.ds(c, tm, stride=n_maps), :] = kn[:, c * HEAD_DIM:(c + 1) * HEAD_DIM]
    for hd in range(N_HEADS):
        vf_ref[0, pl.ds(hd, tm, stride=N_HEADS), :] = v[:, hd * V_DIM:(hd + 1) * V_DIM]


def _in_proj(x, hist, lw, tm):
    b, s, _ = x.shape
    grid = (b, s // tm)
    tok = lambda width: pl.BlockSpec((1, tm, width), lambda i, j: (i, j, 0))
    out_shape = (
        jax.ShapeDtypeStruct((b, s, QK_DIM), BF16),
        jax.ShapeDtypeStruct((b, s * QK_DIM // HEAD_DIM, HEAD_DIM), F32),
        jax.ShapeDtypeStruct((b, s, QK_DIM), BF16),
        jax.ShapeDtypeStruct((b, s * N_HEADS, V_DIM), F32),
        jax.ShapeDtypeStruct((b, s, ATTN_DIM), BF16),
        jax.ShapeDtypeStruct((b, s, D_MODEL), BF16),
        jax.ShapeDtypeStruct((b, s, D_MODEL), BF16),
        jax.ShapeDtypeStruct((b, HIST_ROWS, CONV_DIM), F32),
    )
    return pl.pallas_call(
        _in_proj_kernel,
        grid=grid,
        in_specs=[
            tok(D_MODEL),
            pl.BlockSpec((1, HIST_ROWS, CONV_DIM), lambda i, j: (i, 0, 0)),
            _const_spec((1, D_MODEL)),
            _const_spec((D_MODEL, IN_COLS)),
            _const_spec((CONV_TAPS, CONV_DIM)),
            _const_spec((1, CONV_DIM)),
            _const_spec((1, CONV_DIM)),
            _const_spec((1, CONV_DIM)),
            _const_spec((CONV_DIM, D_MODEL)),
            _const_spec((1, QK_DIM)),
            _const_spec((1, QK_DIM)),
            _const_spec((QK_DIM, QK_DIM)),
        ],
        out_specs=(tok(QK_DIM), pl.BlockSpec((1, tm * QK_DIM // HEAD_DIM, HEAD_DIM), lambda i, j: (i, j, 0)),
                   tok(QK_DIM), pl.BlockSpec((1, tm * N_HEADS, V_DIM), lambda i, j: (i, j, 0)),
                   tok(ATTN_DIM), tok(D_MODEL), tok(D_MODEL),
                   pl.BlockSpec((1, HIST_ROWS, CONV_DIM), lambda i, j: (i, 0, 0))),
        out_shape=out_shape,
        scratch_shapes=[pltpu.VMEM((HIST_ROWS + tm, CONV_DIM), F32), pltpu.VMEM((tm, CONV_DIM), BF16)],
        compiler_params=pltpu.CompilerParams(dimension_semantics=("arbitrary", "arbitrary"),
                                             vmem_limit_bytes=VMEM_LIMIT),
        name="in_proj",
    )(x, hist, lw["g1"], lw["w_in"], lw["conv_dw"], lw["conv_db"], lw["conv_ln_g"], lw["conv_ln_b"],
      lw["w_conv_out"], lw["gq"], lw["gk"], lw["bd"])


def _lambda(lq1_ref, lk1_ref, lq2_ref, lk2_ref, lam_init):
    a = jnp.sum(lq1_ref[...] * lk1_ref[...], axis=-1, keepdims=True)
    b = jnp.sum(lq2_ref[...] * lk2_ref[...], axis=-1, keepdims=True)
    return jnp.exp(a) - jnp.exp(b) + lam_init


def _split_maps(q):
    lane = lax.broadcasted_iota(jnp.int32, q.shape, 1)
    zero = jnp.zeros_like(q)
    return jnp.where(lane < HEAD_DIM, q, zero), jnp.where(lane >= HEAD_DIM, q, zero)


def _scores(qm, k):
    return lax.dot_general(qm, k, (((1,), (1,)), ((), ())), preferred_element_type=F32)


def _sub_ln(o1, o2, lam, sg_ref, lam_init):
    o = o1 - lam * o2
    ms = jnp.mean(o * o, axis=-1, keepdims=True)
    return o * lax.rsqrt(ms + EPS) * sg_ref[...] * (1.0 - lam_init)


def _chunk_mask(s):
    row = lax.broadcasted_iota(jnp.int32, s.shape, 0) // CHUNK
    col = lax.broadcasted_iota(jnp.int32, s.shape, 1) // CHUNK
    return jnp.where(col <= row, s, -jnp.inf)


def _attn_prompt_kernel(shift_ref, q_ref, k_ref, v_ref, lq1_ref, lk1_ref, lq2_ref, lk2_ref, sg_ref, o_ref,
                        qs1, qs2, m1, l1, a1, m2, l2, a2, *, lam_init, online):
    qi = pl.program_id(2)
    tq = q_ref.shape[1]
    q1, q2 = _split_maps(q_ref[0])
    qs1[...] = q1
    qs2[...] = q2
    for m, l, a in ((m1, l1, a1), (m2, l2, a2)):
        m[...] = jnp.full(m.shape, -jnp.inf, F32)
        l[...] = jnp.zeros(l.shape, F32)
        a[...] = jnp.zeros(a.shape, F32)

    def block(kstart, nk, masked, r0=0):
        rows = slice(r0, tq)
        k = k_ref[0, pl.ds(kstart, nk), :]
        v = v_ref[0, pl.ds(kstart, nk), :]
        for qs, m, l, a in ((qs1, m1, l1, a1), (qs2, m2, l2, a2)):
            s = _scores(qs[rows, :], k)
            if masked:
                s = _chunk_mask(s)
            if online:
                m_prev = m[rows, :]
                m_new = jnp.maximum(m_prev, jnp.max(s, axis=-1, keepdims=True))
                alpha = jnp.exp2(m_prev - m_new)
                p = jnp.exp2(s - m_new)
                l[rows, :] = alpha * l[rows, :] + jnp.sum(p, axis=-1, keepdims=True)
                a[rows, :] = alpha * a[rows, :] + jnp.dot(p.astype(BF16), v, preferred_element_type=F32)
                m[rows, :] = m_new
            else:
                p = jnp.exp2(s - shift_ref[0])
                part = p[:, 0:LANES]
                for c in range(1, nk // LANES):
                    part = part + p[:, c * LANES:(c + 1) * LANES]
                l[rows, :] += part
                a[rows, :] += jnp.dot(p.astype(BF16), v, preferred_element_type=F32)

    def body(i, carry):
        block(pl.multiple_of(i * ATTN_KEYS_PER_TRIP, ATTN_KEYS_PER_TRIP), ATTN_KEYS_PER_TRIP, False)
        return carry

    lax.fori_loop(0, qi * (tq // ATTN_KEYS_PER_TRIP), body, 0)
    for d in range(tq // ATTN_KEY_BLOCK):
        block(pl.multiple_of(qi * tq + d * ATTN_KEY_BLOCK, ATTN_KEY_BLOCK), ATTN_KEY_BLOCK, True, d * ATTN_KEY_BLOCK)

    lam = _lambda(lq1_ref, lk1_ref, lq2_ref, lk2_ref, lam_init)
    d1 = jnp.sum(l1[...], axis=-1, keepdims=True)
    d2 = jnp.sum(l2[...], axis=-1, keepdims=True)
    o_ref[0] = _sub_ln(a1[...] / d1, a2[...] / d2, lam, sg_ref, lam_init).astype(BF16)


def _attn_prompt(shift, q, k, v, lw, lam_init, tq, online):
    b, s, _ = q.shape
    grid = (b, N_HEADS, s // tq)
    qspec = pl.BlockSpec((1, tq, LANES), lambda i, h, j: (i, j, h))
    kvspec = pl.BlockSpec((1, s, LANES), lambda i, h, j: (i, 0, h))
    vec = _const_spec((1, HEAD_DIM))
    lw_width = 1 if online else LANES
    return pl.pallas_call(
        functools.partial(_attn_prompt_kernel, lam_init=lam_init, online=online),
        grid=grid,
        in_specs=[pl.BlockSpec(memory_space=pltpu.SMEM), qspec, kvspec, kvspec, vec, vec, vec, vec,
                  _const_spec((1, V_DIM))],
        out_specs=qspec,
        out_shape=jax.ShapeDtypeStruct((b, s, ATTN_DIM), BF16),
        scratch_shapes=[pltpu.VMEM((tq, LANES), BF16), pltpu.VMEM((tq, LANES), BF16),
                        pltpu.VMEM((tq, 1), F32), pltpu.VMEM((tq, lw_width), F32), pltpu.VMEM((tq, V_DIM), F32),
                        pltpu.VMEM((tq, 1), F32), pltpu.VMEM((tq, lw_width), F32), pltpu.VMEM((tq, V_DIM), F32)],
        compiler_params=pltpu.CompilerParams(dimension_semantics=("arbitrary", "arbitrary", "arbitrary"),
                                             vmem_limit_bytes=VMEM_LIMIT),
        name="attn_prompt_online" if online else "attn_prompt",
    )(shift, q, k, v, lw["lq1"], lw["lk1"], lw["lq2"], lw["lk2"], lw["subln_g"])


def _attn_sample_kernel(q_ref, kc_ref, vc_ref, kn_ref, vn_ref, lq1_ref, lk1_ref, lq2_ref, lk2_ref, sg_ref, o_ref,
                        *, lam_init):
    t = q_ref.shape[1]
    qq = jnp.concatenate(_split_maps(q_ref[0]), axis=0)
    kc = kc_ref[0].astype(BF16)
    past = vc_ref.shape[1] // N_HEADS
    vc = vc_ref[0, pl.ds(pl.program_id(1), past, stride=N_HEADS), :].astype(BF16)
    sc = _scores(qq, kc)
    sn = _scores(qq, kn_ref[0])
    m = jnp.maximum(jnp.max(sc, axis=-1, keepdims=True), jnp.max(sn, axis=-1, keepdims=True))
    pc = jnp.exp2(sc - m)
    pn = jnp.exp2(sn - m)
    l = jnp.sum(pc, axis=-1, keepdims=True) + jnp.sum(pn, axis=-1, keepdims=True)
    o = (jnp.dot(pc.astype(BF16), vc, preferred_element_type=F32)
         + jnp.dot(pn.astype(BF16), vn_ref[0], preferred_element_type=F32)) / l
    lam = _lambda(lq1_ref, lk1_ref, lq2_ref, lk2_ref, lam_init)
    o_ref[0] = _sub_ln(o[0:t], o[t:2 * t], lam, sg_ref, lam_init).astype(BF16)


def _attn_sample(q, k_cache, v_cache, k_new, v_new, lw, lam_init):
    b, t, _ = q.shape
    past = k_cache.shape[1]
    new = pl.BlockSpec((1, t, LANES), lambda i, h: (i, 0, h))
    cache = pl.BlockSpec((1, past, LANES), lambda i, h: (i, 0, h))
    cache_v = pl.BlockSpec((1, past * N_HEADS, V_DIM), lambda i, h: (i, 0, 0))
    vec = _const_spec((1, HEAD_DIM))
    return pl.pallas_call(
        functools.partial(_attn_sample_kernel, lam_init=lam_init),
        grid=(b, N_HEADS),
        in_specs=[new, cache, cache_v, new, new, vec, vec, vec, vec, _const_spec((1, V_DIM))],
        out_specs=new,
        out_shape=jax.ShapeDtypeStruct((b, t, ATTN_DIM), BF16),
        compiler_params=pltpu.CompilerParams(dimension_semantics=("arbitrary", "arbitrary"),
                                             vmem_limit_bytes=VMEM_LIMIT),
        name="attn_sample",
    )(q, k_cache, v_cache, k_new, v_new, lw["lq1"], lw["lk1"], lw["lq2"], lw["lk2"], lw["subln_g"])


def _post_kernel(attn_ref, p_ref, g_ref, x_ref, wao_ref, wo_ref, g2_ref, wrt_ref, brt_ref, xm_ref, *out_refs, routed):
    tm = x_ref.shape[0]
    ao = jnp.dot(attn_ref[...], wao_ref[...], preferred_element_type=F32)
    merged = p_ref[...].astype(F32) + g_ref[...].astype(F32) * ao
    xm = x_ref[...] + jnp.dot(merged.astype(BF16), wo_ref[...], preferred_element_type=F32)
    xm_ref[...] = xm
    ms = jnp.mean(xm * xm, axis=-1, keepdims=True)
    h2 = xm * lax.rsqrt(ms + EPS) * g2_ref[...]

    h_hi = h2.astype(BF16)
    h_lo = (h2 - h_hi.astype(F32)).astype(BF16)
    r = (jnp.dot(h_hi, wrt_ref[...], preferred_element_type=F32)
         + jnp.dot(h_lo, wrt_ref[...], preferred_element_type=F32))
    lg = r[:, :LANES] + r[:, LANES:] + brt_ref[...]
    lane = lax.broadcasted_iota(jnp.int32, lg.shape, 1)
    neg = jnp.full_like(lg, -jnp.inf)
    big = jnp.full_like(lane, LANES)

    def first_max(z):
        zmax = jnp.max(z, axis=-1, keepdims=True)
        return zmax, jnp.min(jnp.where(z == zmax, lane, big), axis=-1, keepdims=True)

    is_group = lane < N_GROUPS
    gmax, gi = first_max(jnp.where(is_group, lg, neg))
    g_w = 1.0 / jnp.sum(jnp.where(is_group, jnp.exp(lg - gmax), 0.0), axis=-1, keepdims=True)
    lo = N_GROUPS + EXPERTS_PER_GROUP * gi
    el = jnp.where((lane >= lo) & (lane < lo + EXPERTS_PER_GROUP), lg, neg)
    v1, i1 = first_max(el)
    v2, i2 = first_max(jnp.where(lane == i1, neg, el))
    e2 = jnp.exp(v2 - v1)
    w1 = g_w / (1.0 + e2)
    w2 = g_w * e2 / (1.0 + e2)
    comb = jnp.where(lane == i1 - N_GROUPS, w1, 0.0) + jnp.where(lane == i2 - N_GROUPS, w2, 0.0)
    if routed:
        rows_ref, gid_ref = out_refs
        _store_tokens(rows_ref, jnp.concatenate([h2, comb], axis=1), ROW_SPAN)
        gid = jnp.where(lane == 0, gi.astype(F32), 0.0).astype(BF16)
        pick = (lax.broadcasted_iota(jnp.int32, (8, LANES), 1) == 0).astype(BF16)
        gid_ref[0] = lax.dot_general(pick, gid, (((1,), (1,)), ((), ())), preferred_element_type=F32)
    else:
        h2_ref, comb_ref = out_refs
        h2_ref[...] = h_hi
        comb_ref[...] = comb


def _post(attn, p, g, x, lw, tm, routed):
    t = x.shape[0]
    tok = lambda width: pl.BlockSpec((tm, width), lambda i: (i, 0))
    if routed:
        out_specs = (tok(D_MODEL), pl.BlockSpec((tm * ROW_SPAN, LANES), lambda i: (i, 0)),
                     pl.BlockSpec((1, 8, tm), lambda i: (i, 0, 0)))
        out_shape = (jax.ShapeDtypeStruct((t, D_MODEL), F32), jax.ShapeDtypeStruct((t * ROW_SPAN, LANES), F32),
                     jax.ShapeDtypeStruct((t // tm, 8, tm), F32))
    else:
        out_specs = (tok(D_MODEL), tok(D_MODEL), tok(LANES))
        out_shape = (jax.ShapeDtypeStruct((t, D_MODEL), F32), jax.ShapeDtypeStruct((t, D_MODEL), BF16),
                     jax.ShapeDtypeStruct((t, LANES), F32))
    return pl.pallas_call(
        functools.partial(_post_kernel, routed=routed),
        grid=(t // tm,),
        in_specs=[tok(ATTN_DIM), tok(D_MODEL), tok(D_MODEL), tok(D_MODEL),
                  _const_spec((ATTN_DIM, D_MODEL)), _const_spec((D_MODEL, D_MODEL)), _const_spec((1, D_MODEL)),
                  _const_spec((D_MODEL, 2 * LANES)), _const_spec((1, LANES))],
        out_specs=out_specs,
        out_shape=out_shape,
        compiler_params=pltpu.CompilerParams(dimension_semantics=("arbitrary",), vmem_limit_bytes=VMEM_LIMIT),
        name="post_routed" if routed else "post",
    )(attn, p, g, x, lw["w_attn_out"], lw["w_o"], lw["g2"], lw["w_rt"], lw["b_rt"])


def _token_rows(first_token, n, c, span):
    return pl.ds(first_token * span + c, n, stride=span)


def _load_tokens(ref, first_token, n, tiles, span):
    return [ref[_token_rows(first_token, n, c, span), :] for c in range(tiles)]


def _store_tokens(ref, value, span):
    n, width = value.shape
    for c in range(span):
        tile = value[:, c * LANES:(c + 1) * LANES] if c * LANES < width else jnp.zeros((n, LANES), value.dtype)
        ref[_token_rows(0, n, c, span), :] = tile


def _token_gather(idx_ref, tokens_hbm, buf, sem, base, n, span, dma_threads=1):
    def start():
        def some(i, carry):
            for k in range(DMA_ISSUE_UNROLL):
                r = i * DMA_ISSUE_UNROLL + k
                src = pl.ds(pl.multiple_of(idx_ref[0, 0, r] * span, span), span)
                dst = pl.ds(pl.multiple_of((base + r) * span, span), span)
                pltpu.make_async_copy(tokens_hbm.at[src, :], buf.at[dst, :], sem).start(priority=k % dma_threads)
            return carry
        lax.fori_loop(0, n // DMA_ISSUE_UNROLL, some, 0)

    def wait():
        dst = pl.ds(pl.multiple_of(base * span, span), n * span)
        pltpu.make_async_copy(tokens_hbm.at[pl.ds(0, n * span), :], buf.at[dst, :], sem).wait()

    return start, wait


def _gathered_tokens(idx_ref, idx_next_ref, tokens_hbm, buf, sems, n, tiles, span, dma_threads=1):
    j = pl.program_id(0)
    slot = j % 2
    start_cur, wait_cur = _token_gather(idx_ref, tokens_hbm, buf, sems.at[slot], slot * n, n, span, dma_threads)
    start_next, _ = _token_gather(idx_next_ref, tokens_hbm, buf, sems.at[1 - slot], (1 - slot) * n, n, span,
                                  dma_threads)

    @pl.when(j == 0)
    def _():
        start_cur()

    @pl.when(j + 1 < pl.num_programs(0))
    def _():
        start_next()

    wait_cur()
    return _load_tokens(buf, slot * n, n, tiles, span)


def _idx_specs(n_steps, n):
    cur = lambda j, *_: (j, 0, 0)
    nxt = lambda j, *_: (jnp.minimum(j + 1, n_steps - 1), 0, 0)
    return [pl.BlockSpec((1, 1, n), cur, memory_space=pltpu.SMEM),
            pl.BlockSpec((1, 1, n), nxt, memory_space=pltpu.SMEM)]


def _moe_routed_kernel(group_ref, src_ref, src_next_ref, rows_hbm, wg_ref, wu_ref, wd_ref, out_ref, buf, sems):
    tile = out_ref.shape[0] // OUT_SPAN
    g = group_ref[pl.program_id(0)]
    rows = _gathered_tokens(src_ref, src_next_ref, rows_hbm, buf, sems, tile, ROW_TILES, ROW_SPAN)
    t = jnp.concatenate([x.astype(BF16) for x in rows[:OUT_SPAN]], axis=1)
    comb = rows[OUT_SPAN]
    lane = lax.broadcasted_iota(jnp.int32, comb.shape, 1)
    out = jnp.zeros((tile, D_MODEL), F32)
    for e in range(EXPERTS_PER_GROUP):
        gate = jnp.dot(t, wg_ref[0, e], preferred_element_type=F32)
        up = jnp.dot(t, wu_ref[0, e], preferred_element_type=F32)
        he = (gate * _sigmoid(gate) * up).astype(BF16)
        d = jnp.dot(he, wd_ref[0, e], preferred_element_type=F32)
        c = jnp.sum(jnp.where(lane == g * EXPERTS_PER_GROUP + e, comb, 0.0), axis=-1, keepdims=True)
        out = out + c * d
    _store_tokens(out_ref, out, OUT_SPAN)


def _moe_routed(tile_group, src, rows, lw, tile):
    n_tiles = tile_group.shape[0]
    wspec = lambda a, b: pl.BlockSpec((1, EXPERTS_PER_GROUP, a, b), lambda j, grp: (grp[j], 0, 0, 0))
    return pl.pallas_call(
        _moe_routed_kernel,
        grid_spec=pltpu.PrefetchScalarGridSpec(
            num_scalar_prefetch=1,
            grid=(n_tiles,),
            in_specs=_idx_specs(n_tiles, tile) + [
                pl.BlockSpec(memory_space=pl.ANY),
                wspec(D_MODEL, EXPERT_HIDDEN), wspec(D_MODEL, EXPERT_HIDDEN), wspec(EXPERT_HIDDEN, D_MODEL)],
            out_specs=pl.BlockSpec((tile * OUT_SPAN, LANES), lambda j, grp: (j, 0)),
            scratch_shapes=[pltpu.VMEM((2 * tile * ROW_SPAN, LANES), F32), pltpu.SemaphoreType.DMA((2,))],
        ),
        out_shape=jax.ShapeDtypeStruct((n_tiles * tile * OUT_SPAN, LANES), F32),
        compiler_params=pltpu.CompilerParams(dimension_semantics=("arbitrary",), vmem_limit_bytes=VMEM_LIMIT),
        name="moe_routed",
    )(tile_group, src, src, rows, lw["w_gate"], lw["w_up"], lw["w_down"])


def _combine_kernel(dest_ref, dest_next_ref, xm_ref, sorted_hbm, y_ref, buf, sems):
    tm = xm_ref.shape[0]
    moe = _gathered_tokens(dest_ref, dest_next_ref, sorted_hbm, buf, sems, tm, OUT_SPAN, OUT_SPAN, dma_threads=2)
    y_ref[...] = xm_ref[...] + jnp.concatenate(moe, axis=1)


def _combine(dest, xm, moe_sorted, tm):
    t = xm.shape[0]
    n = t // tm
    idx = dest.reshape(n, 1, tm)
    return pl.pallas_call(
        _combine_kernel,
        grid=(n,),
        in_specs=_idx_specs(n, tm) + [pl.BlockSpec((tm, D_MODEL), lambda j: (j, 0)),
                                      pl.BlockSpec(memory_space=pl.ANY)],
        out_specs=pl.BlockSpec((tm, D_MODEL), lambda j: (j, 0)),
        out_shape=jax.ShapeDtypeStruct((t, D_MODEL), F32),
        scratch_shapes=[pltpu.VMEM((2 * tm * OUT_SPAN, LANES), F32), pltpu.SemaphoreType.DMA((2,))],
        compiler_params=pltpu.CompilerParams(dimension_semantics=("arbitrary",), vmem_limit_bytes=VMEM_LIMIT),
        name="moe_combine",
    )(idx, idx, xm, moe_sorted)


def _slot_sources_kernel(dest_ref, src_ref):
    def clear(i, carry):
        src_ref[i] = 0
        return carry
    lax.fori_loop(0, src_ref.shape[0], clear, 0, unroll=8)

    def put(token, carry):
        src_ref[dest_ref[token]] = token
        return carry
    lax.fori_loop(0, dest_ref.shape[0], put, 0, unroll=8)


def _slot_sources(dest, n_slots):
    return pl.pallas_call(
        _slot_sources_kernel,
        in_specs=[pl.BlockSpec(memory_space=pltpu.SMEM)],
        out_specs=pl.BlockSpec(memory_space=pltpu.SMEM),
        out_shape=jax.ShapeDtypeStruct((n_slots,), jnp.int32),
        name="moe_slot_sources",
    )(dest)


def _routing_tables(group_id, tile):
    t = group_id.shape[0]
    n_tiles = t // tile + N_GROUPS
    onehot = (group_id[:, None] == jnp.arange(N_GROUPS, dtype=jnp.int32)[None, :]).astype(jnp.int32)
    csum = jnp.cumsum(onehot, axis=0)
    rank = jnp.sum(onehot * csum, axis=1) - 1
    tiles_per_group = (csum[-1] + tile - 1) // tile
    tile_end = jnp.cumsum(tiles_per_group)
    offset = (tile_end - tiles_per_group) * tile
    dest = jnp.sum(onehot * offset[None, :], axis=1) + rank
    src = _slot_sources(dest.astype(jnp.int32), n_tiles * tile)
    tile_group = jnp.sum(jnp.arange(n_tiles, dtype=jnp.int32)[:, None] >= tile_end[None, :], axis=1)
    tile_group = jnp.minimum(tile_group, N_GROUPS - 1).astype(jnp.int32)
    return dest.astype(jnp.int32), src.reshape(n_tiles, 1, tile), tile_group


def _moe_kernel(h2_ref, comb_ref, xm_ref, wg_ref, wu_ref, wd_ref, y_ref):
    e = pl.program_id(1)

    @pl.when(e == 0)
    def _():
        y_ref[...] = xm_ref[...]

    t = h2_ref[...]
    gate = jnp.dot(t, wg_ref[0], preferred_element_type=F32)
    up = jnp.dot(t, wu_ref[0], preferred_element_type=F32)
    he = (gate * _sigmoid(gate) * up).astype(BF16)
    d = jnp.dot(he, wd_ref[0], preferred_element_type=F32)
    comb = comb_ref[...]
    lane = lax.broadcasted_iota(jnp.int32, comb.shape, 1)
    c = jnp.sum(jnp.where(lane == e, comb, 0.0), axis=-1, keepdims=True)
    y_ref[...] += c * d


def _moe(h2, comb, xm, lw, tm):
    t = h2.shape[0]
    tok = lambda width: pl.BlockSpec((tm, width), lambda i, e: (i, 0))
    return pl.pallas_call(
        _moe_kernel,
        grid=(t // tm, N_EXPERTS),
        in_specs=[tok(D_MODEL), tok(LANES), tok(D_MODEL),
                  pl.BlockSpec((1, D_MODEL, EXPERT_HIDDEN), lambda i, e: (e, 0, 0)),
                  pl.BlockSpec((1, D_MODEL, EXPERT_HIDDEN), lambda i, e: (e, 0, 0)),
                  pl.BlockSpec((1, EXPERT_HIDDEN, D_MODEL), lambda i, e: (e, 0, 0))],
        out_specs=tok(D_MODEL),
        out_shape=jax.ShapeDtypeStruct((t, D_MODEL), F32),
        compiler_params=pltpu.CompilerParams(dimension_semantics=("arbitrary", "arbitrary"),
                                             vmem_limit_bytes=VMEM_LIMIT),
        name="moe",
    )(h2, comb, xm, *(lw[n].reshape(N_EXPERTS, *lw[n].shape[2:]) for n in ("w_gate", "w_up", "w_down")))


def _hi_lo_columns(w):
    hi = w.astype(BF16)
    return jnp.concatenate([hi, (w - hi.astype(F32)).astype(BF16)], axis=1)


def _layer_weights(l, norm1_g, w_in, conv_dw, conv_db, conv_ln_g, conv_ln_b, w_conv_out, q_norm_g, k_norm_g,
                   lambda_q1, lambda_k1, lambda_q2, lambda_k2, subln_g, w_attn_out, w_o, norm2_g, w_group, b_group,
                   w_router, b_router, w_gate, w_up, w_down):
    row = lambda a: a.reshape(1, -1).astype(F32)
    grouped = lambda w: w.astype(BF16).reshape(N_GROUPS, EXPERTS_PER_GROUP, *w.shape[1:])
    head_gain = lambda gain: jnp.tile(gain.reshape(-1), N_HEADS).reshape(1, QK_DIM).astype(F32)
    grp = jnp.arange(QK_DIM) // HEAD_DIM
    w_rt = jnp.concatenate([w_group[l], jnp.moveaxis(w_router[l], 0, 1).reshape(D_MODEL, N_EXPERTS)], axis=1)
    b_rt = jnp.concatenate([b_group[l], b_router[l].reshape(-1)])
    pad = LANES - N_GROUPS - N_EXPERTS
    return {
        "g1": row(norm1_g[l]), "w_in": w_in[l].astype(BF16), "conv_dw": conv_dw[l].astype(F32),
        "conv_db": row(conv_db[l]), "conv_ln_g": row(conv_ln_g[l]), "conv_ln_b": row(conv_ln_b[l]),
        "w_conv_out": w_conv_out[l].astype(BF16), "gq": head_gain(q_norm_g[l]), "gk": head_gain(k_norm_g[l]),
        "bd": (grp[:, None] == grp[None, :]).astype(BF16),
        "lq1": row(lambda_q1[l]), "lk1": row(lambda_k1[l]), "lq2": row(lambda_q2[l]), "lk2": row(lambda_k2[l]),
        "subln_g": row(subln_g[l]), "w_attn_out": w_attn_out[l].astype(BF16), "w_o": w_o[l].astype(BF16),
        "g2": row(norm2_g[l]),
        "w_rt": _hi_lo_columns(jnp.pad(w_rt.astype(F32), ((0, 0), (0, pad)))),
        "b_rt": jnp.pad(b_rt.astype(F32), (0, pad)).reshape(1, LANES),
        "w_gate": grouped(w_gate[l]), "w_up": grouped(w_up[l]), "w_down": grouped(w_down[l]),
    }


def _tail_dense(x, attn, p, g, lw, tm):
    b, s, _ = x.shape
    flat = lambda a: a.reshape(b * s, a.shape[-1])
    xm, h2, comb = _post(flat(attn), flat(p), flat(g), flat(x), lw, tm, False)
    return _moe(h2, comb, xm, lw, tm).reshape(b, s, D_MODEL)


def _tail_routed(x, attn, p, g, lw, tm, tile):
    b, s, _ = x.shape
    flat = lambda a: a.reshape(b * s, a.shape[-1])
    xm, rows, gid = _post(flat(attn), flat(p), flat(g), flat(x), lw, tm, True)
    dest, src, tile_group = _routing_tables(gid[:, 0, :].reshape(-1).astype(jnp.int32), tile)
    moe_sorted = _moe_routed(tile_group, src, rows, lw, tile)
    return _combine(dest, xm, moe_sorted, tm).reshape(b, s, D_MODEL)


def kernel(x_prompt, x_sample, cache_k, cache_v, state_conv, norm1_g, w_in, conv_dw, conv_db, conv_ln_g, conv_ln_b, w_conv_out, q_norm_g, k_norm_g, lambda_q1, lambda_k1, lambda_q2, lambda_k2, subln_g, w_attn_out, w_o, norm2_g, w_group, b_group, w_router, b_router, w_gate, w_up, w_down):
    depth = w_in.shape[0]
    bp, sp, _ = x_prompt.shape
    bs, ss, _ = x_sample.shape
    past = cache_k.shape[2]
    xp, xs = x_prompt, x_sample
    kp_l, vp_l, cp_l, ks_l, vs_l, cs_l = [], [], [], [], [], []
    for l in range(depth):
        lw = _layer_weights(l, norm1_g, w_in, conv_dw, conv_db, conv_ln_g, conv_ln_b, w_conv_out, q_norm_g,
                            k_norm_g, lambda_q1, lambda_k1, lambda_q2, lambda_k2, subln_g, w_attn_out, w_o,
                            norm2_g, w_group, b_group, w_router, b_router, w_gate, w_up, w_down)
        lam_init = 0.8 - 0.6 * math.exp(-0.3 * l)

        hist_p = jnp.zeros((bp, HIST_ROWS, CONV_DIM), F32)
        q, kf, kb, vf, vb, p, g, cs = _in_proj(xp, hist_p, lw, TOKEN_TILE)
        bound = (HEAD_DIM * SCALE) * jnp.max(jnp.abs(lw["gq"])) * jnp.max(jnp.abs(lw["gk"]))
        shift = (bound * LOG2E).reshape(1)
        attn = lax.cond(bound <= MAX_FIXED_SHIFT,
                        lambda *a: _attn_prompt(*a, lw, lam_init, ATTN_QUERY_BLOCK, False),
                        lambda *a: _attn_prompt(*a, lw, lam_init, ATTN_QUERY_BLOCK, True),
                        shift, q, kb, vb)
        xp = _tail_routed(xp, attn, p, g, lw, TOKEN_TILE, MOE_TILE)
        kp_l.append(kf.reshape(bp, sp, N_HEADS, 2, HEAD_DIM))
        vp_l.append(vf.reshape(bp, sp, N_HEADS, V_DIM))
        cp_l.append(cs[:, HIST_PAD:, :])

        hist_s = jnp.pad(state_conv[l].astype(F32), ((0, 0), (HIST_PAD, 0), (0, 0)))
        q, kf, kb, vf, vb, p, g, cs = _in_proj(xs, hist_s, lw, ss)
        attn = _attn_sample(q, cache_k[l].astype(BF16).reshape(bs, past, QK_DIM),
                            cache_v[l].reshape(bs, past * N_HEADS, V_DIM),
                            kb, vb, lw, lam_init)
        xs = _tail_dense(xs, attn, p, g, lw, bs * ss)
        ks_l.append(kf.reshape(bs, ss, N_HEADS, 2, HEAD_DIM))
        vs_l.append(vf.reshape(bs, ss, N_HEADS, V_DIM))
        cs_l.append(cs[:, HIST_PAD:, :])
    return (xp, xs, jnp.stack(kp_l), jnp.stack(vp_l), jnp.stack(cp_l),
            jnp.stack(ks_l), jnp.stack(vs_l), jnp.stack(cs_l))
```
